```python
import math
import jax, jax.numpy as jnp
from jax import lax
import numpy as np

D_MODEL = 1024
BATCH = 4
SEQ = 8192
DEPTH = 1

CHUNK = 64
QBLOCK = 128
ROPE_THETA = 10000.0
EPS = 1e-6
N_ATTN_HEADS = 8
HEAD_DIM = D_MODEL // 16
D_ATTN = N_ATTN_HEADS * HEAD_DIM
N_IDX_HEADS = 8
IDX_DIM = 64
TOPK_MAX = 256
D_RNN = D_MODEL // 2
N_RNN_BLOCKS = 8
RNN_BLOCK = D_RNN // N_RNN_BLOCKS
CONV_WIDTH = 4
RG_C = 8.0
D_MIX = D_ATTN + D_RNN
IN_SPLITS = [D_ATTN, D_ATTN, D_ATTN, N_IDX_HEADS * IDX_DIM, IDX_DIM, N_IDX_HEADS, D_RNN, D_RNN]
D_IN = sum(IN_SPLITS)
IN_OFFSETS = [int(o) for o in np.cumsum(IN_SPLITS)[:-1]]
N_GROUPS = 4
EXP_PER_GROUP = 8
TOP_E = 2
D_EXPERT = D_MODEL // 4
N_MOD = 6

kernel_name = "hybrid_dsa_rglru_hmoe_block"


def rms_norm(x, g):
    xf = x.astype(jnp.float32)
    y = xf * lax.rsqrt(jnp.mean(xf * xf, axis=-1, keepdims=True) + EPS)
    return (y * g.astype(jnp.float32)).astype(x.dtype)


def rope_tables(seq_len, dim):
    pos = jnp.arange(seq_len, dtype=jnp.float32)
    inv = ROPE_THETA ** (-jnp.arange(0, dim, 2, dtype=jnp.float32) / dim)
    ang = pos[:, None] * inv[None, :]
    return jnp.cos(ang), jnp.sin(ang)


def apply_rope(x, cos, sin):
    xf = x.astype(jnp.float32)
    x1, x2 = jnp.split(xf, 2, axis=-1)
    c, s = cos[None, :, None, :], sin[None, :, None, :]
    return jnp.concatenate([x1 * c - x2 * s, x2 * c + x1 * s], axis=-1).astype(x.dtype)


def sparse_indexed_attention(q, k, v, q_idx, k_idx, w_idx):
    B, S, H, Dh = q.shape
    n_sel = min(TOPK_MAX, S // 4)
    key_pos = jnp.arange(S)
    qi_f = q_idx.astype(jnp.float32)
    ki_f = k_idx.astype(jnp.float32)
    wi_f = w_idx.astype(jnp.float32) * (N_IDX_HEADS ** -0.5)
    gather = jax.vmap(lambda t, i: t[i])

    def one_block(start):
        sl = lambda a: lax.dynamic_slice_in_dim(a, start, QBLOCK, axis=1)
        q_b, qi_b, wi_b = sl(q), sl(qi_f), sl(wi_f)
        q_pos = start + jnp.arange(QBLOCK)
        limit = (q_pos // CHUNK + 1) * CHUNK
        dots = jnp.einsum('bqhd,bsd->bqhs', qi_b, ki_f) * (IDX_DIM ** -0.5)
        score = jnp.einsum('bqh,bqhs->bqs', wi_b, jax.nn.relu(dots))
        score = jnp.where(key_pos[None, None, :] < limit[None, :, None], score, -jnp.inf)
        _, idx = lax.top_k(score, n_sel)
        valid = idx < limit[None, :, None]
        k_sel = gather(k, idx)
        v_sel = gather(v, idx)
        logits = jnp.einsum('bqhd,bqkhd->bhqk', q_b, k_sel).astype(jnp.float32) * (Dh ** -0.5)
        logits = jnp.where(valid[:, None], logits, -jnp.inf)
        p = jax.nn.softmax(logits, axis=-1).astype(v.dtype)
        return jnp.einsum('bhqk,bqkhd->bqhd', p, v_sel)

    starts = jnp.arange(S // QBLOCK) * QBLOCK
    out = lax.map(one_block, starts)
    return out.transpose(1, 0, 2, 3, 4).reshape(B, S, H * Dh)


def rglru_branch(xr, conv_w, conv_b, w_a, b_a, w_x, b_x, lam):
    B, S, R = xr.shape
    conv = lax.conv_general_dilated(
        xr, conv_w[:, None, :].astype(xr.dtype), window_strides=(1,),
        padding=[(CONV_WIDTH - 1, 0)], dimension_numbers=('NWC', 'WIO', 'NWC'),
        feature_group_count=R) + conv_b.astype(xr.dtype)
    xf = conv.astype(jnp.float32)
    xb = xf.reshape(B, S, N_RNN_BLOCKS, RNN_BLOCK)
    r = jax.nn.sigmoid(jnp.einsum('bsnc,ncd->bsnd', xb, w_a.astype(jnp.float32)).reshape(B, S, R) + b_a.astype(jnp.float32))
    i = jax.nn.sigmoid(jnp.einsum('bsnc,ncd->bsnd', xb, w_x.astype(jnp.float32)).reshape(B, S, R) + b_x.astype(jnp.float32))
    log_a = -RG_C * r * jax.nn.softplus(-lam.astype(jnp.float32))
    a = jnp.exp(log_a)
    b = jnp.sqrt(-jnp.expm1(2.0 * log_a)) * (i * xf)

    def combine(left, right):
        a_l, b_l = left
        a_r, b_r = right
        return a_l * a_r, a_r * b_l + b_r

    _, h = lax.associative_scan(combine, (a, b), axis=1)
    return h.astype(xr.dtype)


def hierarchical_moe(h, w_rg, b_rg, w_re, b_re, w_gate, w_up, w_down):
    B, S, D = h.shape
    hf = h.reshape(B * S, D)
    h32 = hf.astype(jnp.float32)
    p_groups = jax.nn.softmax(h32 @ w_rg.astype(jnp.float32) + b_rg.astype(jnp.float32), axis=-1)
    g_idx = jnp.argmax(p_groups, axis=-1)
    p_g = jnp.max(p_groups, axis=-1)
    e_logits = (h32 @ w_re.astype(jnp.float32) + b_re.astype(jnp.float32)).reshape(-1, N_GROUPS, EXP_PER_GROUP)
    e_chosen = jnp.take_along_axis(e_logits, g_idx[:, None, None], axis=1)[:, 0]
    e_val, e_idx = lax.top_k(jax.nn.softmax(e_chosen, axis=-1), TOP_E)
    e_val = e_val / jnp.sum(e_val, axis=-1, keepdims=True)
    e_w = jnp.sum(jax.nn.one_hot(e_idx, EXP_PER_GROUP, dtype=jnp.float32) * e_val[..., None], axis=1)
    comb = (jax.nn.one_hot(g_idx, N_GROUPS, dtype=jnp.float32)[:, :, None]
            * (p_g[:, None] * e_w)[:, None, :]).astype(h.dtype)
    out = jnp.zeros_like(hf)
    for g in range(N_GROUPS):
        gate = jnp.einsum('td,edf->tef', hf, w_gate[g])
        up = jnp.einsum('td,edf->tef', hf, w_up[g])
        act = jax.nn.silu(gate) * up * comb[:, g, :, None]
        out = out + jnp.einsum('tef,efd->td', act, w_down[g])
    return out.reshape(B, S, D)


def setup_inputs(seed: int = 0) -> dict:
    key = jax.random.key(seed)
    ks = jax.random.split(key, 24)
    f32 = jnp.float32
    nrm = lambda k, shape, fan_in, s=1.0: jax.random.normal(k, shape, f32) * (s * fan_in ** -0.5)
    small = lambda k, shape: 0.01 * jax.random.normal(k, shape, f32)
    gain = lambda k: 1.0 + 0.05 * jax.random.normal(k, (DEPTH, D_MODEL), f32)
    u = jax.random.uniform(ks[15], (DEPTH, D_RNN), f32, 0.9, 0.999)
    a0 = u ** (1.0 / RG_C)
    return {
        "x": jax.random.normal(ks[0], (BATCH, SEQ, D_MODEL), f32),
        "c": jax.random.normal(ks[1], (BATCH, D_MODEL), f32),
        "w_ada": nrm(ks[2], (DEPTH, D_MODEL, N_MOD * D_MODEL), D_MODEL, 0.5),
        "b_ada": small(ks[3], (DEPTH, N_MOD * D_MODEL)),
        "g_pre_mix": gain(ks[4]),
        "g_post_mix": gain(ks[5]),
        "g_pre_ffn": gain(ks[6]),
        "g_post_ffn": gain(ks[7]),
        "w_in": nrm(ks[8], (DEPTH, D_MODEL, D_IN), D_MODEL),
        "conv_w": nrm(ks[9], (DEPTH, CONV_WIDTH, D_RNN), CONV_WIDTH),
        "conv_b": small(ks[10], (DEPTH, D_RNN)),
        "w_rg_a": nrm(ks[11], (DEPTH, N_RNN_BLOCKS, RNN_BLOCK, RNN_BLOCK), RNN_BLOCK),
        "b_rg_a": small(ks[12], (DEPTH, D_RNN)),
        "w_rg_x": nrm(ks[13], (DEPTH, N_RNN_BLOCKS, RNN_BLOCK, RNN_BLOCK), RNN_BLOCK),
        "b_rg_x": small(ks[14], (DEPTH, D_RNN)),
        "lru_lambda": jnp.log(a0) - jnp.log1p(-a0),
        "w_out": nrm(ks[16], (DEPTH, D_MIX, D_MODEL), D_MIX),
        "w_router_group": nrm(ks[17], (DEPTH, D_MODEL, N_GROUPS), D_MODEL),
        "b_router_group": small(ks[18], (DEPTH, N_GROUPS)),
        "w_router_expert": nrm(ks[19], (DEPTH, D_MODEL, N_GROUPS * EXP_PER_GROUP), D_MODEL),
        "b_router_expert": small(ks[20], (DEPTH, N_GROUPS * EXP_PER_GROUP)),
        "w_gate": nrm(ks[21], (DEPTH, N_GROUPS, EXP_PER_GROUP, D_MODEL, D_EXPERT), D_MODEL),
        "w_up": nrm(ks[22], (DEPTH, N_GROUPS, EXP_PER_GROUP, D_MODEL, D_EXPERT), D_MODEL),
        "w_down": nrm(ks[23], (DEPTH, N_GROUPS, EXP_PER_GROUP, D_EXPERT, D_MODEL), D_EXPERT),
    }


def reference(x, c, w_ada, b_ada, g_pre_mix, g_post_mix, g_pre_ffn, g_post_ffn, w_in,
              conv_w, conv_b, w_rg_a, b_rg_a, w_rg_x, b_rg_x, lru_lambda, w_out,
              w_router_group, b_router_group, w_router_expert, b_router_expert,
              w_gate, w_up, w_down):
    B, S, D = x.shape
    cos, sin = rope_tables(S, HEAD_DIM)
    cos_i, sin_i = rope_tables(S, IDX_DIM)
    for l in range(DEPTH):
        mod = jax.nn.silu(c) @ w_ada[l] + b_ada[l]
        shift1, scale1, gate1, shift2, scale2, gate2 = [m[:, None, :] for m in jnp.split(mod, N_MOD, axis=-1)]

        h = rms_norm(x, g_pre_mix[l]) * (1 + scale1) + shift1
        proj = h @ w_in[l]
        q, k, v, qi, ki, wi, xr, xg = jnp.split(proj, IN_OFFSETS, axis=-1)
        q = apply_rope(q.reshape(B, S, N_ATTN_HEADS, HEAD_DIM), cos, sin)
        k = apply_rope(k.reshape(B, S, N_ATTN_HEADS, HEAD_DIM), cos, sin)
        v = v.reshape(B, S, N_ATTN_HEADS, HEAD_DIM)
        qi = apply_rope(qi.reshape(B, S, N_IDX_HEADS, IDX_DIM), cos_i, sin_i)
        ki = apply_rope(ki.reshape(B, S, 1, IDX_DIM), cos_i, sin_i)[:, :, 0]
        y_attn = sparse_indexed_attention(q, k, v, qi, ki, wi)
        y_rnn = rglru_branch(xr, conv_w[l], conv_b[l], w_rg_a[l], b_rg_a[l],
                             w_rg_x[l], b_rg_x[l], lru_lambda[l]) * jax.nn.gelu(xg)
        mix = jnp.concatenate([y_attn, y_rnn], axis=-1) @ w_out[l]
        x = x + gate1 * rms_norm(mix, g_post_mix[l])

        h2 = rms_norm(x, g_pre_ffn[l]) * (1 + scale2) + shift2
        y = hierarchical_moe(h2, w_router_group[l], b_router_group[l], w_router_expert[l],
                             b_router_expert[l], w_gate[l], w_up[l], w_down[l])
        x = x + gate2 * rms_norm(y, g_post_ffn[l])
    return x
```

```python
import functools
import math

import jax
import jax.numpy as jnp
from jax import lax
from jax.experimental import pallas as pl
from jax.experimental.pallas import tpu as pltpu

F32 = jnp.float32
BF16 = jnp.bfloat16
I32 = jnp.int32

D_MODEL = 1024
CHUNK = 64
ROPE_THETA = 10000.0
EPS = 1e-6
N_HEADS = 8
HEAD_DIM = 64
D_ATTN = N_HEADS * HEAD_DIM
N_IDX_HEADS = 8
IDX_DIM = 64
TOPK_MAX = 256
D_RNN = 512
N_RNN_BLOCKS = 8
CONV_WIDTH = 4
RG_C = 8.0
N_GROUPS = 4
EXP_PER_GROUP = 8
N_EXPERTS = N_GROUPS * EXP_PER_GROUP
D_EXPERT = 256
N_MOD = 6

LANES = 128
INT_MIN = -2 ** 31
NEG_INF = float("-inf")

Q_BLOCK = 128
K_BLOCK = 512
TOKEN_TILE = 512
VMEM_LIMIT = 60 * 1024 * 1024


def _cparams(sem):
    return pltpu.CompilerParams(dimension_semantics=sem, vmem_limit_bytes=VMEM_LIMIT)


def _nt_dot(a, b):
    return lax.dot_general(a, b, (((1,), (1,)), ((), ())), preferred_element_type=F32)


def _rms(x, g):
    return x * lax.rsqrt(jnp.mean(x * x, axis=-1, keepdims=True) + EPS) * g


def _mod_kernel(c_ref, w_ref, b_ref, o_ref):
    c = c_ref[...]
    sc = c * jax.nn.sigmoid(c)
    o_ref[...] = jnp.dot(sc, w_ref[...], preferred_element_type=F32,
                         precision=lax.Precision.HIGHEST) + b_ref[...]


def _mod_call(c_pad, w_ada, b_ada):
    rows, d = c_pad.shape
    n = w_ada.shape[1]
    bn = 1024
    return pl.pallas_call(
        _mod_kernel,
        grid=(n // bn,),
        in_specs=[pl.BlockSpec((rows, d), lambda j: (0, 0)),
                  pl.BlockSpec((d, bn), lambda j: (0, j)),
                  pl.BlockSpec((1, bn), lambda j: (0, j))],
        out_specs=pl.BlockSpec((rows, bn), lambda j: (0, j)),
        out_shape=jax.ShapeDtypeStruct((rows, n), F32),
        compiler_params=_cparams(("arbitrary",)),
        name="mod",
    )(c_pad, w_ada, b_ada)


_C_Q, _C_K, _C_QI, _C_KK, _C_V, _C_WI, _C_XR, _C_XG, _C_END = (
    0, 512, 1024, 1536, 1664, 2176, 2304, 2816, 3328)


def _rope(y, cos, sin):
    lane = lax.broadcasted_iota(I32, (1, LANES), 1)
    low = (lane % HEAD_DIM) < (HEAD_DIM // 2)
    outs = []
    for p in range(y.shape[1] // LANES):
        s = y[:, p * LANES:(p + 1) * LANES]
        swapped = jnp.where(low, pltpu.roll(s, LANES - HEAD_DIM // 2, axis=1),
                            pltpu.roll(s, HEAD_DIM // 2, axis=1))
        outs.append(s * cos + swapped * sin)
    return outs[0] if len(outs) == 1 else jnp.concatenate(outs, axis=1)


def _in_proj_kernel(x_ref, mod_ref, g_ref, w_ref, cos_ref, sin_ref,
                    q_ref, k_ref, qi_ref, kk_ref, v_ref, wi_ref, xr_ref, xg_ref):
    x = x_ref[0]
    shift = mod_ref[0, 0:1, :]
    scale = mod_ref[0, 1:2, :]
    h = (_rms(x, g_ref[...]) * (1.0 + scale) + shift).astype(BF16)
    cos = cos_ref[...]
    sin = sin_ref[...]

    def proj(a, b):
        return jnp.dot(h, w_ref[:, a:b], preferred_element_type=F32)

    q_ref[0] = (_rope(proj(_C_Q, _C_K), cos, sin) * (HEAD_DIM ** -0.5)).astype(BF16)
    k_ref[0] = _rope(proj(_C_K, _C_QI), cos, sin).astype(BF16)
    qi_ref[0] = (_rope(proj(_C_QI, _C_KK), cos, sin) * (IDX_DIM ** -0.5)).astype(BF16)
    kk_ref[0] = _rope(proj(_C_KK, _C_V), cos, sin).astype(BF16)
    v_ref[0] = proj(_C_V, _C_WI).astype(BF16)
    wi_ref[0] = proj(_C_WI, _C_XR) * (N_IDX_HEADS ** -0.5)
    xr_ref[0] = proj(_C_XR, _C_XG)
    xg_ref[0] = proj(_C_XG, _C_END)


def _in_proj_call(x, mod, g, w_packed, cos2, sin2):
    b, s, d = x.shape
    tm = min(TOKEN_TILE, s)
    tile = lambda n: pl.BlockSpec((1, tm, n), lambda bi, ti: (bi, ti, 0))
    shp = lambda n, dt: jax.ShapeDtypeStruct((b, s, n), dt)
    return pl.pallas_call(
        _in_proj_kernel,
        grid=(b, s // tm),
        in_specs=[tile(d),
                  pl.BlockSpec((1, N_MOD, d), lambda bi, ti: (bi, 0, 0)),
                  pl.BlockSpec((1, d), lambda bi, ti: (0, 0)),
                  pl.BlockSpec((d, _C_END), lambda bi, ti: (0, 0)),
                  pl.BlockSpec((tm, LANES), lambda bi, ti: (ti, 0)),
                  pl.BlockSpec((tm, LANES), lambda bi, ti: (ti, 0))],
        out_specs=[tile(512), tile(512), tile(512), tile(LANES), tile(512), tile(LANES),
                   tile(512), tile(512)],
        out_shape=[shp(512, BF16), shp(512, BF16), shp(512, BF16), shp(LANES, BF16),
                   shp(512, BF16), shp(LANES, F32), shp(512, F32), shp(512, F32)],
        compiler_params=_cparams(("parallel", "parallel")),
        name="in_proj",
    )(x, mod, g, w_packed, cos2, sin2)


def _sortable(score):
    score = jnp.where(score == 0.0, 0.0, score)
    bits = pltpu.bitcast(score, I32)
    return jnp.where(bits < 0, bits ^ jnp.int32(0x7FFFFFFF), bits)


def _lane_fold(x, op):
    acc = x[:, 0:LANES]
    for t in range(1, x.shape[1] // LANES):
        acc = op(acc, x[:, t * LANES:(t + 1) * LANES])
    return acc


def _attn_kernel(q_ref, qi_ref, wi_ref, kk_ref, k_ref, v_ref, o_ref,
                 keys_ref, bias_ref, logit_ref, *, n_sel, qb, kb):
    i = pl.program_id(1)
    n_ch = ((i + 1) * qb + kb - 1) // kb
    q_pos = i * qb + lax.broadcasted_iota(I32, (qb, 1), 0)
    limit = (q_pos // CHUNK + 1) * CHUNK
    lane = lax.broadcasted_iota(I32, (1, LANES), 1)
    lo_half = lane < HEAD_DIM
    kcol = lax.broadcasted_iota(I32, (1, kb), 1)

    wi = wi_ref[0]
    qi = qi_ref[0]
    qi_heads = []
    for h in range(N_IDX_HEADS):
        slab = qi[:, (h // 2) * LANES:(h // 2 + 1) * LANES]
        keep = lo_half if h % 2 == 0 else jnp.logical_not(lo_half)
        qi_heads.append(jnp.where(keep, slab, jnp.zeros_like(slab)))

    def score_body(j, carry):
        kk = kk_ref[0, pl.ds(pl.multiple_of(j * kb, kb), kb), :]
        acc = jnp.zeros((qb, kb), F32)
        for h in range(N_IDX_HEADS):
            d = _nt_dot(qi_heads[h], kk)
            acc = acc + wi[:, h:h + 1] * jnp.maximum(d, 0.0)
        key = _sortable(acc)
        keys_ref[j] = jnp.where(j * kb + kcol < limit, key, jnp.int32(INT_MIN))
        return carry

    lax.fori_loop(0, n_ch, score_body, 0)

    def count(pred):
        def body(j, acc):
            return acc + _lane_fold(pred(keys_ref[j]).astype(I32), jnp.add)
        acc = lax.fori_loop(0, n_ch, body, jnp.zeros((qb, LANES), I32))
        return jnp.sum(acc, axis=1, keepdims=True)

    c0 = count(lambda kc: kc >= 0)
    tau = jnp.where(c0 >= n_sel, jnp.int32(0), jnp.int32(INT_MIN))

    def bit_body(bi, tau):
        cand = tau | (jnp.int32(1) << (30 - bi))
        c = count(lambda kc: kc >= cand)
        return jnp.where(c >= n_sel, cand, tau)

    tau = lax.fori_loop(0, 31, bit_body, tau)
    n_gt = count(lambda kc: kc > tau)
    need = (n_sel - n_gt).astype(F32)

    r_i = lax.broadcasted_iota(I32, (kb, kb), 0)
    c_i = lax.broadcasted_iota(I32, (kb, kb), 1)
    tri = (r_i <= c_i).astype(BF16)

    def bias_body(j, run):
        kc = keys_ref[j]
        eq = kc == tau
        rank = jnp.dot(eq.astype(BF16), tri, preferred_element_type=F32) + run
        sel = (kc > tau) | (eq & (rank <= need))
        sel = sel & (j * kb + kcol < limit)
        bias_ref[j] = jnp.where(sel, 0.0, NEG_INF).astype(F32)
        return rank[:, kb - 1:kb]

    lax.fori_loop(0, n_ch, bias_body, jnp.zeros((qb, 1), F32))

    for p in range(N_HEADS // 2):
        cols = slice(p * LANES, (p + 1) * LANES)
        qp = q_ref[0, :, cols]
        outs = []
        for hh in range(2):
            keep = lo_half if hh == 0 else jnp.logical_not(lo_half)
            qh = jnp.where(keep, qp, jnp.zeros_like(qp))

            def logit_body(j, mx):
                rows = pl.ds(pl.multiple_of(j * kb, kb), kb)
                lg = _nt_dot(qh, k_ref[0, rows, cols]) + bias_ref[j]
                logit_ref[j] = lg
                return jnp.maximum(mx, _lane_fold(lg, jnp.maximum))

            mx = lax.fori_loop(0, n_ch, logit_body, jnp.full((qb, LANES), NEG_INF, F32))
            m = jnp.max(mx, axis=1, keepdims=True)

            def pv_body(j, carry):
                l_acc, o_acc = carry
                rows = pl.ds(pl.multiple_of(j * kb, kb), kb)
                pr = jnp.exp(logit_ref[j] - m)
                l_acc = l_acc + _lane_fold(pr, jnp.add)
                o_acc = o_acc + jnp.dot(pr.astype(BF16), v_ref[0, rows, cols],
                                        preferred_element_type=F32)
                return l_acc, o_acc

            l_acc, o_acc = lax.fori_loop(
                0, n_ch, pv_body, (jnp.zeros((qb, LANES), F32), jnp.zeros((qb, LANES), F32)))
            outs.append(o_acc / jnp.sum(l_acc, axis=1, keepdims=True))
        o_ref[0, :, cols] = jnp.where(lo_half, outs[0], outs[1]).astype(o_ref.dtype)


def _attn_call(q, qi, wi, kk, k, v):
    b, s, _ = q.shape
    qb = min(Q_BLOCK, s)
    kb = min(K_BLOCK, s)
    n_sel = min(TOPK_MAX, s // 4)
    blk = lambda n: pl.BlockSpec((1, qb, n), lambda bi, qi_: (bi, qi_, 0))
    full = lambda n: pl.BlockSpec((1, s, n), lambda bi, qi_: (bi, 0, 0),
                                  pipeline_mode=pl.Buffered(1))
    kern = functools.partial(_attn_kernel, n_sel=n_sel, qb=qb, kb=kb)
    return pl.pallas_call(
        kern,
        grid=(b, s // qb),
        in_specs=[blk(D_ATTN), blk(D_ATTN), blk(LANES), full(LANES), full(D_ATTN), full(D_ATTN)],
        out_specs=blk(D_ATTN),
        out_shape=jax.ShapeDtypeStruct((b, s, D_ATTN), BF16),
        scratch_shapes=[pltpu.VMEM((s // kb, qb, kb), I32),
                        pltpu.VMEM((s // kb, qb, kb), F32),
                        pltpu.VMEM((s // kb, qb, kb), F32)],
        compiler_params=_cparams(("parallel", "arbitrary")),
        name="attn",
    )(q, qi, wi, kk, k, v)


def _gelu_tanh(x):
    return 0.5 * x * (1.0 + jnp.tanh(math.sqrt(2.0 / math.pi) * (x + 0.044715 * (x * x * x))))


def _rglru_kernel(xr_ref, xg_ref, cw_ref, cb_ref, wa_ref, ba_ref, wx_ref, bx_ref, lam_ref,
                  o_ref, xbuf_ref, h_ref, *, ts):
    t = pl.program_id(1)
    pad = 8

    @pl.when(t == 0)
    def _():
        xbuf_ref[0:pad, :] = jnp.zeros((pad, D_RNN), F32)
        h_ref[...] = jnp.zeros_like(h_ref)

    xr = xr_ref[0]
    xbuf_ref[pad:pad + ts, :] = xr
    conv = cb_ref[...] + cw_ref[CONV_WIDTH - 1:CONV_WIDTH, :] * xr
    for w in range(CONV_WIDTH - 1):
        off = pad - (CONV_WIDTH - 1) + w
        conv = conv + cw_ref[w:w + 1, :] * xbuf_ref[off:off + ts, :]
    xbuf_ref[0:pad, :] = xr[ts - pad:ts, :]

    cb16 = conv.astype(BF16)
    r = jax.nn.sigmoid(jnp.dot(cb16, wa_ref[...], preferred_element_type=F32) + ba_ref[...])
    ig = jax.nn.sigmoid(jnp.dot(cb16, wx_ref[...], preferred_element_type=F32) + bx_ref[...])
    z = -lam_ref[...]
    softplus = jnp.maximum(z, 0.0) + jnp.log1p(jnp.exp(-jnp.abs(z)))
    log_a = (-RG_C) * r * softplus
    a = jnp.exp(log_a)
    bb = jnp.sqrt(1.0 - a * a) * (ig * conv)

    row = lax.broadcasted_iota(I32, (ts, 1), 0)
    d = 1
    while d < ts:
        live = row >= d
        a_s = jnp.where(live, pltpu.roll(a, d, axis=0), 1.0)
        b_s = jnp.where(live, pltpu.roll(bb, d, axis=0), 0.0)
        bb = bb + a * b_s
        a = a * a_s
        d *= 2
    h = bb + a * h_ref[0:1, :]
    h_ref[0:1, :] = h[ts - 1:ts, :]
    o_ref[0] = (h * _gelu_tanh(xg_ref[0])).astype(o_ref.dtype)


def _rglru_call(xr, xg, conv_w, conv_b, wa, ba, wx, bx, lam):
    b, s, r = xr.shape
    ts = min(TOKEN_TILE, s)
    tile = pl.BlockSpec((1, ts, r), lambda bi, ti: (bi, ti, 0))
    const = lambda shape: pl.BlockSpec(shape, lambda bi, ti: (0, 0))
    return pl.pallas_call(
        functools.partial(_rglru_kernel, ts=ts),
        grid=(b, s // ts),
        in_specs=[tile, tile, const((CONV_WIDTH, r)), const((1, r)), const((r, r)), const((1, r)),
                  const((r, r)), const((1, r)), const((1, r))],
        out_specs=tile,
        out_shape=jax.ShapeDtypeStruct((b, s, r), BF16),
        scratch_shapes=[pltpu.VMEM((ts + 8, r), F32), pltpu.VMEM((8, r), F32)],
        compiler_params=_cparams(("parallel", "arbitrary")),
        name="rglru",
    )(xr, xg, conv_w, conv_b, wa, ba, wx, bx, lam)


def _route(logits):
    lane = lax.broadcasted_iota(I32, logits.shape, 1)
    is_g = (lane >= N_EXPERTS) & (lane < N_EXPERTS + N_GROUPS)
    big = jnp.int32(LANES)

    def first_lane(mask):
        return jnp.min(jnp.where(mask, lane, big), axis=1, keepdims=True)

    gl = jnp.where(is_g, logits, NEG_INF)
    ge = jnp.exp(gl - jnp.max(gl, axis=1, keepdims=True))
    p_groups = ge / jnp.sum(ge, axis=1, keepdims=True)
    p_g = jnp.max(p_groups, axis=1, keepdims=True)
    g_idx = first_lane(is_g & (p_groups == p_g)) - N_EXPERTS

    in_grp = (lane < N_EXPERTS) & ((lane // EXP_PER_GROUP) == g_idx)
    el = jnp.where(in_grp, logits, NEG_INF)
    ee = jnp.exp(el - jnp.max(el, axis=1, keepdims=True))
    es = ee / jnp.sum(ee, axis=1, keepdims=True)
    es = jnp.where(in_grp, es, NEG_INF)
    v1 = jnp.max(es, axis=1, keepdims=True)
    i1 = first_lane(in_grp & (es == v1))
    es2 = jnp.where(lane == i1, NEG_INF, es)
    v2 = jnp.max(es2, axis=1, keepdims=True)
    i2 = first_lane(in_grp & (es2 == v2))
    tot = v1 + v2
    e_w = jnp.where(lane == i1, v1 / tot, 0.0) + jnp.where(lane == i2, v2 / tot, 0.0)
    return p_g * e_w


def _out_proj_kernel(ya_ref, yr_ref, x_ref, mod_ref, wo_ref, gpm_ref, gpf_ref, wr_ref, br_ref,
                     x1_ref, h2_ref, comb_ref):
    mix = (jnp.dot(ya_ref[0], wo_ref[0:D_ATTN, :], preferred_element_type=F32)
           + jnp.dot(yr_ref[0], wo_ref[D_ATTN:D_ATTN + D_RNN, :], preferred_element_type=F32))
    gate1 = mod_ref[0, 2:3, :]
    shift2 = mod_ref[0, 3:4, :]
    scale2 = mod_ref[0, 4:5, :]
    x1 = x_ref[0] + gate1 * _rms(mix, gpm_ref[...])
    x1_ref[0] = x1
    h2 = _rms(x1, gpf_ref[...]) * (1.0 + scale2) + shift2
    h2_ref[0] = h2.astype(BF16)
    logits = jnp.dot(h2, wr_ref[...], preferred_element_type=F32,
                     precision=lax.Precision.HIGHEST) + br_ref[...]
    comb_ref[0] = _route(logits)


def _out_proj_call(ya, yr, x, mod, wo, gpm, gpf, wr, br):
    b, s, d = x.shape
    tm = min(TOKEN_TILE, s)
    tile = lambda n: pl.BlockSpec((1, tm, n), lambda bi, ti: (bi, ti, 0))
    const = lambda shape: pl.BlockSpec(shape, lambda bi, ti: (0,) * len(shape))
    return pl.pallas_call(
        _out_proj_kernel,
        grid=(b, s // tm),
        in_specs=[tile(D_ATTN), tile(D_RNN), tile(d),
                  pl.BlockSpec((1, N_MOD, d), lambda bi, ti: (bi, 0, 0)),
                  const((D_ATTN + D_RNN, d)), const((1, d)), const((1, d)),
                  const((d, LANES)), const((1, LANES))],
        out_specs=[tile(d), tile(d), tile(LANES)],
        out_shape=[jax.ShapeDtypeStruct((b, s, d), F32), jax.ShapeDtypeStruct((b, s, d), BF16),
                   jax.ShapeDtypeStruct((b, s, LANES), F32)],
        compiler_params=_cparams(("parallel", "parallel")),
        name="out_proj",
    )(ya, yr, x, mod, wo, gpm, gpf, wr, br)


def _moe_kernel(h2_ref, comb_ref, x1_ref, mod_ref, wg_ref, wu_ref, wd_ref, gpo_ref,
                o_ref, acc_ref, act_ref):
    g = pl.program_id(2)

    @pl.when(g == 0)
    def _():
        acc_ref[...] = jnp.zeros_like(acc_ref)

    h = h2_ref[0]
    comb = comb_ref[0]
    hi = comb.astype(BF16)
    lo = (comb - hi.astype(F32)).astype(BF16)
    n_act = EXP_PER_GROUP * D_EXPERT
    r_i = lax.broadcasted_iota(I32, (2 * LANES, n_act), 0) % LANES
    c_i = lax.broadcasted_iota(I32, (2 * LANES, n_act), 1) // D_EXPERT
    expand = (r_i == g * EXP_PER_GROUP + c_i).astype(BF16)
    cexp = jnp.dot(jnp.concatenate([hi, lo], axis=1), expand, preferred_element_type=F32)
    for e in range(EXP_PER_GROUP):
        gt = jnp.dot(h, wg_ref[0, e], preferred_element_type=F32)
        up = jnp.dot(h, wu_ref[0, e], preferred_element_type=F32)
        cols = slice(e * D_EXPERT, (e + 1) * D_EXPERT)
        act_ref[:, cols] = (gt * jax.nn.sigmoid(gt) * up * cexp[:, cols]).astype(BF16)
    acc_ref[...] += jnp.dot(act_ref[...], wd_ref[0], preferred_element_type=F32)

    @pl.when(g == N_GROUPS - 1)
    def _():
        gate2 = mod_ref[0, 5:6, :]
        o_ref[0] = x1_ref[0] + gate2 * _rms(acc_ref[...], gpo_ref[...])


def _moe_call(h2, comb, x1, mod, wg, wu, wd, gpo):
    b, s, d = x1.shape
    tm = min(TOKEN_TILE, s)
    tile = lambda n: pl.BlockSpec((1, tm, n), lambda bi, ti, gi: (bi, ti, 0))
    return pl.pallas_call(
        _moe_kernel,
        grid=(b, s // tm, N_GROUPS),
        in_specs=[tile(d), tile(LANES), tile(d),
                  pl.BlockSpec((1, N_MOD, d), lambda bi, ti, gi: (bi, 0, 0)),
                  pl.BlockSpec((1, EXP_PER_GROUP, d, D_EXPERT), lambda bi, ti, gi: (gi, 0, 0, 0)),
                  pl.BlockSpec((1, EXP_PER_GROUP, d, D_EXPERT), lambda bi, ti, gi: (gi, 0, 0, 0)),
                  pl.BlockSpec((1, EXP_PER_GROUP * D_EXPERT, d), lambda bi, ti, gi: (gi, 0, 0)),
                  pl.BlockSpec((1, d), lambda bi, ti, gi: (0, 0))],
        out_specs=tile(d),
        out_shape=jax.ShapeDtypeStruct((b, s, d), F32),
        scratch_shapes=[pltpu.VMEM((tm, d), F32),
                        pltpu.VMEM((tm, EXP_PER_GROUP * D_EXPERT), BF16)],
        compiler_params=_cparams(("parallel", "parallel", "arbitrary")),
        name="moe",
    )(h2, comb, x1, mod, wg, wu, wd, gpo)


def _rope_lane_tables(seq_len):
    pos = jnp.arange(seq_len, dtype=F32)
    inv = ROPE_THETA ** (-jnp.arange(0, HEAD_DIM, 2, dtype=F32) / HEAD_DIM)
    ang = pos[:, None] * inv[None, :]
    cos, sin = jnp.cos(ang), jnp.sin(ang)
    return (jnp.concatenate([cos, cos, cos, cos], axis=1),
            jnp.concatenate([-sin, sin, -sin, sin], axis=1))


def _block_diag(w):
    n, c, d = w.shape
    eye = jnp.eye(n, dtype=w.dtype)
    return (eye[:, None, :, None] * w[:, :, None, :]).reshape(n * c, n * d)


def _pad_cols(w, n):
    return jnp.pad(w, ((0, 0), (0, n - w.shape[1])))


def kernel(x, c, w_ada, b_ada, g_pre_mix, g_post_mix, g_pre_ffn, g_post_ffn, w_in, conv_w, conv_b, w_rg_a, b_rg_a, w_rg_x, b_rg_x, lru_lambda, w_out, w_router_group, b_router_group, w_router_expert, b_router_expert, w_gate, w_up, w_down):
    b, s, d = x.shape
    depth = w_ada.shape[0]
    cos2, sin2 = _rope_lane_tables(s)
    c_pad = jnp.pad(c, ((0, 8 - b % 8 if b % 8 else 0), (0, 0)))
    o_q, o_k, o_v, o_qi, o_ki, o_wi, o_xr, o_xg = 0, 512, 1024, 1536, 2048, 2112, 2120, 2632
    for l in range(depth):
        wl = w_in[l]
        w_ki = wl[:, o_ki:o_wi]
        w_packed = jnp.concatenate(
            [wl[:, o_q:o_k], wl[:, o_k:o_v], wl[:, o_qi:o_ki], w_ki, w_ki, wl[:, o_v:o_qi],
             _pad_cols(wl[:, o_wi:o_xr], LANES), wl[:, o_xr:o_xg], wl[:, o_xg:]],
            axis=1).astype(BF16)
        w_route = _pad_cols(jnp.concatenate([w_router_expert[l], w_router_group[l]], axis=1), LANES)
        b_route = _pad_cols(jnp.concatenate([b_router_expert[l], b_router_group[l]])[None, :], LANES)

        mod = _mod_call(c_pad, w_ada[l], b_ada[l][None, :])[:b].reshape(b, N_MOD, d)
        q, k, qi, kk, v, wi, xr, xg = _in_proj_call(x, mod, g_pre_mix[l][None, :], w_packed, cos2, sin2)
        y_attn = _attn_call(q, qi, wi, kk, k, v)
        y_rnn = _rglru_call(xr, xg, conv_w[l], conv_b[l][None, :],
                            _block_diag(w_rg_a[l]).astype(BF16), b_rg_a[l][None, :],
                            _block_diag(w_rg_x[l]).astype(BF16), b_rg_x[l][None, :],
                            lru_lambda[l][None, :])
        x1, h2, comb = _out_proj_call(y_attn, y_rnn, x, mod, w_out[l].astype(BF16),
                                      g_post_mix[l][None, :], g_pre_ffn[l][None, :], w_route, b_route)
        x = _moe_call(h2, comb, x1, mod, w_gate[l].astype(BF16), w_up[l].astype(BF16),
                      w_down[l].reshape(N_GROUPS, EXP_PER_GROUP * D_EXPERT, d).astype(BF16),
                      g_post_ffn[l][None, :])
    return x
```

```python
import functools
import math

import jax
import jax.numpy as jnp
from jax import lax
from jax.experimental import pallas as pl
from jax.experimental.pallas import tpu as pltpu

F32 = jnp.float32
BF16 = jnp.bfloat16
I32 = jnp.int32
I16 = jnp.int16

D_MODEL = 1024
CHUNK = 64
ROPE_THETA = 10000.0
EPS = 1e-6
N_HEADS = 8
HEAD_DIM = 64
D_ATTN = N_HEADS * HEAD_DIM
N_IDX_HEADS = 8
IDX_DIM = 64
TOPK_MAX = 256
D_RNN = 512
N_RNN_BLOCKS = 8
CONV_WIDTH = 4
RG_C = 8.0
N_GROUPS = 4
EXP_PER_GROUP = 8
N_EXPERTS = N_GROUPS * EXP_PER_GROUP
D_EXPERT = 256
N_MOD = 6

LANES = 128
SUBLANES = 8
INT_MIN = -2 ** 31
INT16_MIN = -2 ** 15
NEG_INF = float("-inf")

Q_BLOCK = 256
K_BLOCK = 512
V_ROWS = HEAD_DIM + 16
TOKEN_TILE = 512
VMEM_LIMIT = 60 * 1024 * 1024


def _cparams(sem):
    return pltpu.CompilerParams(dimension_semantics=sem, vmem_limit_bytes=VMEM_LIMIT)


def _nt_dot(a, b):
    return lax.dot_general(a, b, (((1,), (1,)), ((), ())), preferred_element_type=F32)


def _rms(x, g):
    return x * lax.rsqrt(jnp.mean(x * x, axis=-1, keepdims=True) + EPS) * g


def _mod_kernel(c_ref, w_ref, b_ref, o_ref):
    c = c_ref[...]
    sc = c * jax.nn.sigmoid(c)
    o_ref[...] = jnp.dot(sc, w_ref[...], preferred_element_type=F32,
                         precision=lax.Precision.HIGHEST) + b_ref[...]


def _mod_call(c_pad, w_ada, b_ada):
    rows, d = c_pad.shape
    n = w_ada.shape[1]
    bn = 1024
    return pl.pallas_call(
        _mod_kernel,
        grid=(n // bn,),
        in_specs=[pl.BlockSpec((rows, d), lambda j: (0, 0)),
                  pl.BlockSpec((d, bn), lambda j: (0, j)),
                  pl.BlockSpec((1, bn), lambda j: (0, j))],
        out_specs=pl.BlockSpec((rows, bn), lambda j: (0, j)),
        out_shape=jax.ShapeDtypeStruct((rows, n), F32),
        compiler_params=_cparams(("arbitrary",)),
        name="mod",
    )(c_pad, w_ada, b_ada)


_R_Q, _R_QI, _R_V, _R_WI, _R_END = 0, 512, 1024, 1536, 1552
_C_K, _C_KK, _C_XR, _C_XG, _C_END = 0, 512, 640, 1152, 1664


def _rope(y, cos, sin):
    lane = lax.broadcasted_iota(I32, (1, LANES), 1)
    low = (lane % HEAD_DIM) < (HEAD_DIM // 2)
    outs = []
    for p in range(y.shape[1] // LANES):
        s = y[:, p * LANES:(p + 1) * LANES]
        swapped = jnp.where(low, pltpu.roll(s, LANES - HEAD_DIM // 2, axis=1),
                            pltpu.roll(s, HEAD_DIM // 2, axis=1))
        outs.append(s * cos + swapped * sin)
    return outs[0] if len(outs) == 1 else jnp.concatenate(outs, axis=1)


def _rope_t(y, cos, sin):
    half = HEAD_DIM // 2
    outs = []
    for hd in range(y.shape[0] // HEAD_DIM):
        blk = y[hd * HEAD_DIM:(hd + 1) * HEAD_DIM, :]
        swapped = jnp.concatenate([blk[half:, :], blk[:half, :]], axis=0)
        outs.append(blk * cos + swapped * sin)
    return jnp.concatenate(outs, axis=0)


def _in_proj_kernel(x_ref, mod_ref, g_ref, wn_ref, wt_ref, cos_ref, sin_ref, cost_ref, sint_ref,
                    qt_ref, qit_ref, wit_ref, k_ref, kk_ref, vt_ref, xr_ref, xg_ref):
    x = x_ref[0]
    shift = mod_ref[0, 0:1, :]
    scale = mod_ref[0, 1:2, :]
    h = (_rms(x, g_ref[...]) * (1.0 + scale) + shift).astype(BF16)
    cos, sin = cos_ref[...], sin_ref[...]
    cos_t, sin_t = cost_ref[...], sint_ref[...]

    def proj(a, b):
        return jnp.dot(h, wn_ref[:, a:b], preferred_element_type=F32)

    def proj_t(a, b):
        return _nt_dot(wt_ref[a:b, :], h)

    qt_ref[0] = (_rope_t(proj_t(_R_Q, _R_QI), cos_t, sin_t)
                 * (HEAD_DIM ** -0.5 * math.log2(math.e))).astype(BF16)
    qit_ref[0] = (_rope_t(proj_t(_R_QI, _R_V), cos_t, sin_t) * (IDX_DIM ** -0.5)).astype(BF16)
    vt = proj_t(_R_V, _R_WI).astype(BF16)
    ones = jnp.ones((V_ROWS - HEAD_DIM, vt.shape[1]), BF16)
    for hd in range(N_HEADS):
        vt_ref[0, 0, hd * V_ROWS:hd * V_ROWS + HEAD_DIM, :] = vt[hd * HEAD_DIM:(hd + 1) * HEAD_DIM, :]
        vt_ref[0, 0, hd * V_ROWS + HEAD_DIM:(hd + 1) * V_ROWS, :] = ones
    wit_ref[0] = proj_t(_R_WI, _R_END)[0:N_IDX_HEADS, :] * (N_IDX_HEADS ** -0.5)
    k_ref[0] = _rope(proj(_C_K, _C_KK), cos, sin).astype(BF16)
    kk_ref[0] = _rope(proj(_C_KK, _C_XR), cos, sin).astype(BF16)
    xr_ref[0] = proj(_C_XR, _C_XG)
    xg_ref[0] = proj(_C_XG, _C_END)


def _in_proj_call(x, mod, g, w_nat, w_tr, cos2, sin2, cos_t, sin_t):
    b, s, d = x.shape
    tm = min(K_BLOCK, s)
    nt = s // tm
    tile = lambda n: pl.BlockSpec((1, tm, n), lambda bi, ti: (bi, ti, 0))
    tile_t = lambda n: pl.BlockSpec((1, n, tm), lambda bi, ti: (bi, 0, ti))
    const = lambda shape: pl.BlockSpec(shape, lambda bi, ti: (0, 0))
    shp = lambda n, dt: jax.ShapeDtypeStruct((b, s, n), dt)
    shp_t = lambda n, dt: jax.ShapeDtypeStruct((b, n, s), dt)
    return pl.pallas_call(
        _in_proj_kernel,
        grid=(b, nt),
        in_specs=[tile(d),
                  pl.BlockSpec((1, N_MOD, d), lambda bi, ti: (bi, 0, 0)),
                  const((1, d)), const((d, _C_END)), const((_R_END, d)),
                  pl.BlockSpec((tm, LANES), lambda bi, ti: (ti, 0)),
                  pl.BlockSpec((tm, LANES), lambda bi, ti: (ti, 0)),
                  pl.BlockSpec((HEAD_DIM, tm), lambda bi, ti: (0, ti)),
                  pl.BlockSpec((HEAD_DIM, tm), lambda bi, ti: (0, ti))],
        out_specs=[tile_t(D_ATTN), tile_t(D_ATTN), tile_t(N_IDX_HEADS), tile(D_ATTN), tile(LANES),
                   pl.BlockSpec((1, 1, N_HEADS * V_ROWS, tm), lambda bi, ti: (bi, ti, 0, 0)),
                   tile(D_RNN), tile(D_RNN)],
        out_shape=[shp_t(D_ATTN, BF16), shp_t(D_ATTN, BF16), shp_t(N_IDX_HEADS, F32),
                   shp(D_ATTN, BF16), shp(LANES, BF16),
                   jax.ShapeDtypeStruct((b, nt, N_HEADS * V_ROWS, tm), BF16),
                   shp(D_RNN, F32), shp(D_RNN, F32)],
        compiler_params=_cparams(("parallel", "parallel")),
        name="in_proj",
    )(x, mod, g, w_nat, w_tr, cos2, sin2, cos_t, sin_t)


def _sortable(score):
    score = jnp.where(score == 0.0, 0.0, score)
    bits = pltpu.bitcast(score, I32)
    return jnp.where(bits < 0, bits ^ jnp.int32(0x7FFFFFFF), bits)


def _row_fold(x, op):
    acc = x[0:SUBLANES, :]
    for t in range(1, x.shape[0] // SUBLANES):
        acc = op(acc, x[t * SUBLANES:(t + 1) * SUBLANES, :])
    return acc


def _fold16(x):
    rows = 2 * SUBLANES
    acc = x[0:rows, :]
    for t in range(1, x.shape[0] // rows):
        acc = acc + x[t * rows:(t + 1) * rows, :]
    return acc


def _attn_kernel(qt_ref, qit_ref, wit_ref, kk_ref, k_ref, vt_ref, o_ref,
                 keys_ref, hi_ref, lo_ref, bias_ref, s_ref, qm_ref, qim_ref, m_ref, l_ref, acc_ref,
                 *, n_sel, qb, kb):
    i = pl.program_id(1)
    n_ch = ((i + 1) * qb + kb - 1) // kb
    q_pos = i * qb + lax.broadcasted_iota(I32, (1, qb), 1)
    limit = (q_pos // CHUNK + 1) * CHUNK
    krow = lax.broadcasted_iota(I32, (kb, 1), 0)
    row128 = lax.broadcasted_iota(I32, (LANES, 1), 0)

    for h in range(N_HEADS):
        rows = slice((h // 2) * LANES, (h // 2 + 1) * LANES)
        keep = (row128 < HEAD_DIM) if h % 2 == 0 else (row128 >= HEAD_DIM)
        qs, qis = qt_ref[0, rows, :], qit_ref[0, rows, :]
        qm_ref[h] = jnp.where(keep, qs, jnp.zeros_like(qs))
        qim_ref[h] = jnp.where(keep, qis, jnp.zeros_like(qis))
    m_ref[...] = jnp.full(m_ref.shape, NEG_INF, F32)
    l_ref[...] = jnp.zeros(l_ref.shape, F32)
    acc_ref[...] = jnp.zeros(acc_ref.shape, F32)

    wi = wit_ref[0]

    def score_body(j, carry):
        kk = kk_ref[0, pl.ds(pl.multiple_of(j * kb, kb), kb), :]
        acc = jnp.zeros((kb, qb), F32)
        for h in range(N_IDX_HEADS):
            d = jnp.dot(kk, qim_ref[h], preferred_element_type=F32)
            acc = acc + wi[h:h + 1, :] * jnp.maximum(d, 0.0)
        key = jnp.where(j * kb + krow < limit, _sortable(acc), jnp.int32(INT_MIN))
        keys_ref[j] = key
        hi_ref[j] = (key >> 16).astype(I16)
        return carry

    lax.fori_loop(0, n_ch, score_body, 0)

    def count16(ref, pred):
        def body(j, acc):
            return acc + _fold16(jnp.where(pred(ref[j]), jnp.int16(1), jnp.int16(0)))
        acc = lax.fori_loop(0, n_ch, body, jnp.zeros((2 * SUBLANES, qb), I16))
        return jnp.sum(acc.astype(I32), axis=0, keepdims=True)

    def search16(ref, target):
        c0 = count16(ref, lambda v: v >= jnp.int16(0))
        t0 = jnp.where(c0 >= target, jnp.int32(0), jnp.int32(INT16_MIN))

        def bit_body(bi, t):
            cand = t | (jnp.int32(1) << (14 - bi))
            c = count16(ref, lambda v: v >= cand.astype(I16))
            return jnp.where(c >= target, cand, t)

        return lax.fori_loop(0, 15, bit_body, t0)

    tau_hi = search16(hi_ref, n_sel)
    tau_hi16 = tau_hi.astype(I16)
    n_gt_hi = count16(hi_ref, lambda v: v > tau_hi16)

    def lo_body(j, carry):
        lo = (keys_ref[j] ^ jnp.int32(0x8000)).astype(I16)
        lo_ref[j] = jnp.where(hi_ref[j] == tau_hi16, lo, jnp.int16(INT16_MIN))
        return carry

    lax.fori_loop(0, n_ch, lo_body, 0)
    tau_lo = search16(lo_ref, n_sel - n_gt_hi)
    tau_lo16 = tau_lo.astype(I16)
    n_gt_lo = count16(lo_ref, lambda v: v > tau_lo16)
    n_ge_lo = count16(lo_ref, lambda v: v >= tau_lo16)
    tau = (tau_hi << 16) | ((tau_lo + 32768) & jnp.int32(0xFFFF))
    all_sel = tau == jnp.int32(INT_MIN)
    need_i = jnp.where(all_sel, 0, n_sel - n_gt_hi - n_gt_lo)
    need = need_i.astype(F32)
    ties = jnp.max(jnp.where(all_sel, 0, (n_ge_lo - n_gt_lo) - need_i)) > 0
    thr = jnp.where(all_sel, jnp.int32(INT_MIN + 1), tau)

    def bias_ties(j, run):
        r_i = lax.broadcasted_iota(I32, (kb, kb), 0)
        c_i = lax.broadcasted_iota(I32, (kb, kb), 1)
        tri = (c_i <= r_i).astype(BF16)
        kc = keys_ref[j]
        eq = kc == tau
        rank = jnp.dot(tri, eq.astype(BF16), preferred_element_type=F32) + run
        sel = (kc > tau) | (eq & (rank <= need))
        bias_ref[...] = jnp.where(sel, 0.0, NEG_INF).astype(F32)
        return rank[kb - 1:kb, :]

    def bias_plain(j, run):
        bias_ref[...] = jnp.where(keys_ref[j] >= thr, 0.0, NEG_INF).astype(F32)
        return run

    def attn_body(j, run):
        run = lax.cond(ties, bias_ties, bias_plain, j, run)
        rows = pl.ds(pl.multiple_of(j * kb, kb), kb)

        def logits(h):
            cols = slice((h // 2) * LANES, (h // 2 + 1) * LANES)
            s = jnp.dot(k_ref[0, rows, cols], qm_ref[h], preferred_element_type=F32) + bias_ref[...]
            s_ref[h % 2] = s
            return jnp.max(_row_fold(s, jnp.maximum), axis=0, keepdims=True)

        cm = logits(0)
        for h in range(N_HEADS):
            cm_next = logits(h + 1) if h + 1 < N_HEADS else None
            m_old = m_ref[h]
            m_new = jnp.maximum(m_old, cm)
            m_safe = jnp.where(m_new == NEG_INF, 0.0, m_new)
            alpha = jnp.exp2(m_old - m_safe)
            p = jnp.exp2(s_ref[h % 2] - m_safe)
            pv = jnp.dot(vt_ref[0, j, h * V_ROWS:(h + 1) * V_ROWS, :], p.astype(BF16),
                         preferred_element_type=F32)
            l_ref[h] = alpha * l_ref[h] + pv[HEAD_DIM:HEAD_DIM + 1, :]
            acc_ref[h] = alpha * acc_ref[h] + pv[0:HEAD_DIM, :]
            m_ref[h] = m_new
            cm = cm_next
        return run

    lax.fori_loop(0, n_ch, attn_body, jnp.zeros((1, qb), F32))

    outs = [acc_ref[h] / l_ref[h] for h in range(N_HEADS)]
    o_ref[0] = jnp.concatenate(outs, axis=0).T.astype(o_ref.dtype)


def _attn_call(qt, qit, wit, kk, k, vt):
    b, s, _ = k.shape
    qb = min(Q_BLOCK, s)
    kb = min(K_BLOCK, s)
    n_sel = min(TOPK_MAX, s // 4)
    blk_t = lambda n: pl.BlockSpec((1, n, qb), lambda bi, qi_: (bi, 0, qi_))
    once = pl.Buffered(1)
    kern = functools.partial(_attn_kernel, n_sel=n_sel, qb=qb, kb=kb)
    return pl.pallas_call(
        kern,
        grid=(b, s // qb),
        in_specs=[blk_t(D_ATTN), blk_t(D_ATTN), blk_t(N_IDX_HEADS),
                  pl.BlockSpec((1, s, LANES), lambda bi, qi_: (bi, 0, 0), pipeline_mode=once),
                  pl.BlockSpec((1, s, D_ATTN), lambda bi, qi_: (bi, 0, 0), pipeline_mode=once),
                  pl.BlockSpec((1, s // kb, N_HEADS * V_ROWS, kb), lambda bi, qi_: (bi, 0, 0, 0),
                               pipeline_mode=once)],
        out_specs=pl.BlockSpec((1, qb, D_ATTN), lambda bi, qi_: (bi, qi_, 0)),
        out_shape=jax.ShapeDtypeStruct((b, s, D_ATTN), BF16),
        scratch_shapes=[pltpu.VMEM((s // kb, kb, qb), I32),
                        pltpu.VMEM((s // kb, kb, qb), I16),
                        pltpu.VMEM((s // kb, kb, qb), I16),
                        pltpu.VMEM((kb, qb), F32),
                        pltpu.VMEM((2, kb, qb), F32),
                        pltpu.VMEM((N_HEADS, LANES, qb), BF16),
                        pltpu.VMEM((N_IDX_HEADS, LANES, qb), BF16),
                        pltpu.VMEM((N_HEADS, 1, qb), F32),
                        pltpu.VMEM((N_HEADS, 1, qb), F32),
                        pltpu.VMEM((N_HEADS, HEAD_DIM, qb), F32)],
        compiler_params=_cparams(("parallel", "arbitrary")),
        name="attn",
    )(qt, qit, wit, kk, k, vt)


def _gelu_tanh(x):
    return 0.5 * x * (1.0 + jnp.tanh(math.sqrt(2.0 / math.pi) * (x + 0.044715 * (x * x * x))))


def _rglru_kernel(xr_ref, xg_ref, cw_ref, cb_ref, wa_ref, ba_ref, wx_ref, bx_ref, lam_ref,
                  o_ref, xbuf_ref, h_ref, *, ts):
    t = pl.program_id(1)
    pad = 8

    @pl.when(t == 0)
    def _():
        xbuf_ref[0:pad, :] = jnp.zeros((pad, D_RNN), F32)
        h_ref[...] = jnp.zeros_like(h_ref)

    xr = xr_ref[0]
    xbuf_ref[pad:pad + ts, :] = xr
    conv = cb_ref[...] + cw_ref[CONV_WIDTH - 1:CONV_WIDTH, :] * xr
    for w in range(CONV_WIDTH - 1):
        off = pad - (CONV_WIDTH - 1) + w
        conv = conv + cw_ref[w:w + 1, :] * xbuf_ref[off:off + ts, :]
    xbuf_ref[0:pad, :] = xr[ts - pad:ts, :]

    cb16 = conv.astype(BF16)
    r = jax.nn.sigmoid(jnp.dot(cb16, wa_ref[...], preferred_element_type=F32) + ba_ref[...])
    ig = jax.nn.sigmoid(jnp.dot(cb16, wx_ref[...], preferred_element_type=F32) + bx_ref[...])
    z = -lam_ref[...]
    softplus = jnp.maximum(z, 0.0) + jnp.log1p(jnp.exp(-jnp.abs(z)))
    log_a = (-RG_C) * r * softplus
    a = jnp.exp(log_a)
    bb = jnp.sqrt(1.0 - a * a) * (ig * conv)

    row = lax.broadcasted_iota(I32, (ts, 1), 0)
    d = 1
    while d < ts:
        live = row >= d
        a_s = jnp.where(live, pltpu.roll(a, d, axis=0), 1.0)
        b_s = jnp.where(live, pltpu.roll(bb, d, axis=0), 0.0)
        bb = bb + a * b_s
        a = a * a_s
        d *= 2
    h = bb + a * h_ref[0:1, :]
    h_ref[0:1, :] = h[ts - 1:ts, :]
    o_ref[0] = (h * _gelu_tanh(xg_ref[0])).astype(o_ref.dtype)


def _rglru_call(xr, xg, conv_w, conv_b, wa, ba, wx, bx, lam):
    b, s, r = xr.shape
    ts = min(TOKEN_TILE, s)
    tile = pl.BlockSpec((1, ts, r), lambda bi, ti: (bi, ti, 0))
    const = lambda shape: pl.BlockSpec(shape, lambda bi, ti: (0, 0))
    return pl.pallas_call(
        functools.partial(_rglru_kernel, ts=ts),
        grid=(b, s // ts),
        in_specs=[tile, tile, const((CONV_WIDTH, r)), const((1, r)), const((r, r)), const((1, r)),
                  const((r, r)), const((1, r)), const((1, r))],
        out_specs=tile,
        out_shape=jax.ShapeDtypeStruct((b, s, r), BF16),
        scratch_shapes=[pltpu.VMEM((ts + 8, r), F32), pltpu.VMEM((8, r), F32)],
        compiler_params=_cparams(("parallel", "arbitrary")),
        name="rglru",
    )(xr, xg, conv_w, conv_b, wa, ba, wx, bx, lam)


def _route(logits):
    lane = lax.broadcasted_iota(I32, logits.shape, 1)
    is_g = (lane >= N_EXPERTS) & (lane < N_EXPERTS + N_GROUPS)
    big = jnp.int32(LANES)

    def first_lane(mask):
        return jnp.min(jnp.where(mask, lane, big), axis=1, keepdims=True)

    gl = jnp.where(is_g, logits, NEG_INF)
    ge = jnp.exp(gl - jnp.max(gl, axis=1, keepdims=True))
    p_groups = ge / jnp.sum(ge, axis=1, keepdims=True)
    p_g = jnp.max(p_groups, axis=1, keepdims=True)
    g_idx = first_lane(is_g & (p_groups == p_g)) - N_EXPERTS

    in_grp = (lane < N_EXPERTS) & ((lane // EXP_PER_GROUP) == g_idx)
    el = jnp.where(in_grp, logits, NEG_INF)
    ee = jnp.exp(el - jnp.max(el, axis=1, keepdims=True))
    es = ee / jnp.sum(ee, axis=1, keepdims=True)
    es = jnp.where(in_grp, es, NEG_INF)
    v1 = jnp.max(es, axis=1, keepdims=True)
    i1 = first_lane(in_grp & (es == v1))
    es2 = jnp.where(lane == i1, NEG_INF, es)
    v2 = jnp.max(es2, axis=1, keepdims=True)
    i2 = first_lane(in_grp & (es2 == v2))
    tot = v1 + v2
    e_w = jnp.where(lane == i1, v1 / tot, 0.0) + jnp.where(lane == i2, v2 / tot, 0.0)
    return p_g * e_w


def _out_proj_kernel(ya_ref, yr_ref, x_ref, mod_ref, wo_ref, gpm_ref, gpf_ref, wr_ref, br_ref,
                     x1_ref, h2_ref, comb_ref):
    mix = (jnp.dot(ya_ref[0], wo_ref[0:D_ATTN, :], preferred_element_type=F32)
           + jnp.dot(yr_ref[0], wo_ref[D_ATTN:D_ATTN + D_RNN, :], preferred_element_type=F32))
    gate1 = mod_ref[0, 2:3, :]
    shift2 = mod_ref[0, 3:4, :]
    scale2 = mod_ref[0, 4:5, :]
    x1 = x_ref[0] + gate1 * _rms(mix, gpm_ref[...])
    x1_ref[0] = x1
    h2 = _rms(x1, gpf_ref[...]) * (1.0 + scale2) + shift2
    h2_ref[0] = h2.astype(BF16)
    logits = jnp.dot(h2, wr_ref[...], preferred_element_type=F32,
                     precision=lax.Precision.HIGHEST) + br_ref[...]
    comb_ref[0] = _route(logits)


def _out_proj_call(ya, yr, x, mod, wo, gpm, gpf, wr, br):
    b, s, d = x.shape
    tm = min(TOKEN_TILE, s)
    tile = lambda n: pl.BlockSpec((1, tm, n), lambda bi, ti: (bi, ti, 0))
    const = lambda shape: pl.BlockSpec(shape, lambda bi, ti: (0,) * len(shape))
    return pl.pallas_call(
        _out_proj_kernel,
        grid=(b, s // tm),
        in_specs=[tile(D_ATTN), tile(D_RNN), tile(d),
                  pl.BlockSpec((1, N_MOD, d), lambda bi, ti: (bi, 0, 0)),
                  const((D_ATTN + D_RNN, d)), const((1, d)), const((1, d)),
                  const((d, LANES)), const((1, LANES))],
        out_specs=[tile(d), tile(d), tile(LANES)],
        out_shape=[jax.ShapeDtypeStruct((b, s, d), F32), jax.ShapeDtypeStruct((b, s, d), BF16),
                   jax.ShapeDtypeStruct((b, s, LANES), F32)],
        compiler_params=_cparams(("parallel", "parallel")),
        name="out_proj",
    )(ya, yr, x, mod, wo, gpm, gpf, wr, br)


def _moe_kernel(h2_ref, comb_ref, x1_ref, mod_ref, wg_ref, wu_ref, wd_ref, gpo_ref,
                o_ref, acc_ref, act_ref):
    g = pl.program_id(2)

    @pl.when(g == 0)
    def _():
        acc_ref[...] = jnp.zeros_like(acc_ref)

    h = h2_ref[0]
    comb = comb_ref[0]
    hi = comb.astype(BF16)
    lo = (comb - hi.astype(F32)).astype(BF16)
    n_act = EXP_PER_GROUP * D_EXPERT
    r_i = lax.broadcasted_iota(I32, (2 * LANES, n_act), 0) % LANES
    c_i = lax.broadcasted_iota(I32, (2 * LANES, n_act), 1) // D_EXPERT
    expand = (r_i == g * EXP_PER_GROUP + c_i).astype(BF16)
    cexp = jnp.dot(jnp.concatenate([hi, lo], axis=1), expand, preferred_element_type=F32)
    for e in range(EXP_PER_GROUP):
        gt = jnp.dot(h, wg_ref[0, e], preferred_element_type=F32)
        up = jnp.dot(h, wu_ref[0, e], preferred_element_type=F32)
        cols = slice(e * D_EXPERT, (e + 1) * D_EXPERT)
        act_ref[:, cols] = (gt * jax.nn.sigmoid(gt) * up * cexp[:, cols]).astype(BF16)
    acc_ref[...] += jnp.dot(act_ref[...], wd_ref[0], preferred_element_type=F32)

    @pl.when(g == N_GROUPS - 1)
    def _():
        gate2 = mod_ref[0, 5:6, :]
        o_ref[0] = x1_ref[0] + gate2 * _rms(acc_ref[...], gpo_ref[...])


def _moe_call(h2, comb, x1, mod, wg, wu, wd, gpo):
    b, s, d = x1.shape
    tm = min(TOKEN_TILE, s)
    tile = lambda n: pl.BlockSpec((1, tm, n), lambda bi, ti, gi: (bi, ti, 0))
    return pl.pallas_call(
        _moe_kernel,
        grid=(b, s // tm, N_GROUPS),
        in_specs=[tile(d), tile(LANES), tile(d),
                  pl.BlockSpec((1, N_MOD, d), lambda bi, ti, gi: (bi, 0, 0)),
                  pl.BlockSpec((1, EXP_PER_GROUP, d, D_EXPERT), lambda bi, ti, gi: (gi, 0, 0, 0)),
                  pl.BlockSpec((1, EXP_PER_GROUP, d, D_EXPERT), lambda bi, ti, gi: (gi, 0, 0, 0)),
                  pl.BlockSpec((1, EXP_PER_GROUP * D_EXPERT, d), lambda bi, ti, gi: (gi, 0, 0)),
                  pl.BlockSpec((1, d), lambda bi, ti, gi: (0, 0))],
        out_specs=tile(d),
        out_shape=jax.ShapeDtypeStruct((b, s, d), F32),
        scratch_shapes=[pltpu.VMEM((tm, d), F32),
                        pltpu.VMEM((tm, EXP_PER_GROUP * D_EXPERT), BF16)],
        compiler_params=_cparams(("parallel", "parallel", "arbitrary")),
        name="moe",
    )(h2, comb, x1, mod, wg, wu, wd, gpo)


def _rope_tables(seq_len):
    pos = jnp.arange(seq_len, dtype=F32)
    inv = ROPE_THETA ** (-jnp.arange(0, HEAD_DIM, 2, dtype=F32) / HEAD_DIM)
    ang = pos[:, None] * inv[None, :]
    cos, sin = jnp.cos(ang), jnp.sin(ang)
    cos2 = jnp.concatenate([cos, cos, cos, cos], axis=1)
    sin2 = jnp.concatenate([-sin, sin, -sin, sin], axis=1)
    return cos2, sin2, cos2[:, :HEAD_DIM].T, sin2[:, :HEAD_DIM].T


def _block_diag(w):
    n, c, d = w.shape
    eye = jnp.eye(n, dtype=w.dtype)
    return (eye[:, None, :, None] * w[:, :, None, :]).reshape(n * c, n * d)


def _pad_cols(w, n):
    return jnp.pad(w, ((0, 0), (0, n - w.shape[1])))


def kernel(x, c, w_ada, b_ada, g_pre_mix, g_post_mix, g_pre_ffn, g_post_ffn, w_in, conv_w, conv_b, w_rg_a, b_rg_a, w_rg_x, b_rg_x, lru_lambda, w_out, w_router_group, b_router_group, w_router_expert, b_router_expert, w_gate, w_up, w_down):
    b, s, d = x.shape
    depth = w_ada.shape[0]
    cos2, sin2, cos_t, sin_t = _rope_tables(s)
    c_pad = jnp.pad(c, ((0, (-b) % SUBLANES), (0, 0)))
    o_q, o_k, o_v, o_qi, o_ki, o_wi, o_xr, o_xg = 0, 512, 1024, 1536, 2048, 2112, 2120, 2632
    for l in range(depth):
        wl = w_in[l]
        w_ki = wl[:, o_ki:o_wi]
        w_nat = jnp.concatenate([wl[:, o_k:o_v], w_ki, w_ki, wl[:, o_xr:o_xg], wl[:, o_xg:]],
                                axis=1).astype(BF16)
        w_tr = jnp.concatenate([wl[:, o_q:o_k], wl[:, o_qi:o_ki], wl[:, o_v:o_qi],
                                _pad_cols(wl[:, o_wi:o_xr], _R_END - _R_WI)], axis=1).T.astype(BF16)
        w_route = _pad_cols(jnp.concatenate([w_router_expert[l], w_router_group[l]], axis=1), LANES)
        b_route = _pad_cols(jnp.concatenate([b_router_expert[l], b_router_group[l]])[None, :], LANES)

        mod = _mod_call(c_pad, w_ada[l], b_ada[l][None, :])[:b].reshape(b, N_MOD, d)
        qt, qit, wit, k, kk, vt, xr, xg = _in_proj_call(
            x, mod, g_pre_mix[l][None, :], w_nat, w_tr, cos2, sin2, cos_t, sin_t)
        y_attn = _attn_call(qt, qit, wit, kk, k, vt)
        y_rnn = _rglru_call(xr, xg, conv_w[l], conv_b[l][None, :],
                            _block_diag(w_rg_a[l]).astype(BF16), b_rg_a[l][None, :],
                            _block_diag(w_rg_x[l]).astype(BF16), b_rg_x[l][None, :],
                            lru_lambda[l][None, :])
        x1, h2, comb = _out_proj_call(y_attn, y_rnn, x, mod, w_out[l].astype(BF16),
                                      g_post_mix[l][None, :], g_pre_ffn[l][None, :], w_route, b_route)
        x = _moe_call(h2, comb, x1, mod, w_gate[l].astype(BF16), w_up[l].astype(BF16),
                      w_down[l].reshape(N_GROUPS, EXP_PER_GROUP * D_EXPERT, d).astype(BF16),
                      g_post_ffn[l][None, :])
    return x
```

```python
import functools
import math

import jax
import jax.numpy as jnp
from jax import lax
from jax.experimental import pallas as pl
from jax.experimental.pallas import tpu as pltpu

F32 = jnp.float32
BF16 = jnp.bfloat16
I32 = jnp.int32
I16 = jnp.int16

D_MODEL = 1024
CHUNK = 64
ROPE_THETA = 10000.0
EPS = 1e-6
N_HEADS = 8
HEAD_DIM = 64
D_ATTN = N_HEADS * HEAD_DIM
N_IDX_HEADS = 8
IDX_DIM = 64
TOPK_MAX = 256
D_RNN = 512
N_RNN_BLOCKS = 8
CONV_WIDTH = 4
RG_C = 8.0
N_GROUPS = 4
EXP_PER_GROUP = 8
N_EXPERTS = N_GROUPS * EXP_PER_GROUP
D_EXPERT = 256
N_MOD = 6

LANES = 128
SUBLANES = 8
INT_MIN = -2 ** 31
INT16_MIN = -2 ** 15
NEG_INF = float("-inf")

Q_BLOCK = 256
K_BLOCK = 512
V_ROWS = HEAD_DIM + 16
TOKEN_TILE = 512
MOE_TILE = 1024
MOE_ROWS = 128
ROUTE_GROUP_LANE = 64
VMEM_LIMIT = 60 * 1024 * 1024


def _cparams(sem):
    return pltpu.CompilerParams(dimension_semantics=sem, vmem_limit_bytes=VMEM_LIMIT)


def _nt_dot(a, b):
    return lax.dot_general(a, b, (((1,), (1,)), ((), ())), preferred_element_type=F32)


def _rms(x, g):
    return x * lax.rsqrt(jnp.mean(x * x, axis=-1, keepdims=True) + EPS) * g


def _mod_kernel(c_ref, w_ref, b_ref, o_ref):
    c = c_ref[...]
    sc = c * jax.nn.sigmoid(c)
    o_ref[...] = jnp.dot(sc, w_ref[...], preferred_element_type=F32,
                         precision=lax.Precision.HIGHEST) + b_ref[...]


def _mod_call(c_pad, w_ada, b_ada):
    rows, d = c_pad.shape
    n = w_ada.shape[1]
    bn = 1024
    return pl.pallas_call(
        _mod_kernel,
        grid=(n // bn,),
        in_specs=[pl.BlockSpec((rows, d), lambda j: (0, 0)),
                  pl.BlockSpec((d, bn), lambda j: (0, j)),
                  pl.BlockSpec((1, bn), lambda j: (0, j))],
        out_specs=pl.BlockSpec((rows, bn), lambda j: (0, j)),
        out_shape=jax.ShapeDtypeStruct((rows, n), F32),
        compiler_params=_cparams(("arbitrary",)),
        name="mod",
    )(c_pad, w_ada, b_ada)


_R_Q, _R_QI, _R_V, _R_WI, _R_END = 0, 512, 1024, 1536, 1552
_C_K, _C_KK, _C_XR, _C_XG, _C_END = 0, 512, 640, 1152, 1664


def _rope(y, cos, sin):
    lane = lax.broadcasted_iota(I32, (1, LANES), 1)
    low = (lane % HEAD_DIM) < (HEAD_DIM // 2)
    outs = []
    for p in range(y.shape[1] // LANES):
        s = y[:, p * LANES:(p + 1) * LANES]
        swapped = jnp.where(low, pltpu.roll(s, LANES - HEAD_DIM // 2, axis=1),
                            pltpu.roll(s, HEAD_DIM // 2, axis=1))
        outs.append(s * cos + swapped * sin)
    return outs[0] if len(outs) == 1 else jnp.concatenate(outs, axis=1)


def _rope_t(y, cos, sin):
    half = HEAD_DIM // 2
    outs = []
    for hd in range(y.shape[0] // HEAD_DIM):
        blk = y[hd * HEAD_DIM:(hd + 1) * HEAD_DIM, :]
        swapped = jnp.concatenate([blk[half:, :], blk[:half, :]], axis=0)
        outs.append(blk * cos + swapped * sin)
    return jnp.concatenate(outs, axis=0)


def _in_proj_kernel(x_ref, mod_ref, g_ref, wn_ref, wt_ref, cos_ref, sin_ref, cost_ref, sint_ref,
                    qt_ref, qit_ref, wit_ref, k_ref, kk_ref, vt_ref, xr_ref, xg_ref):
    x = x_ref[0]
    shift = mod_ref[0, 0:1, :]
    scale = mod_ref[0, 1:2, :]
    h = (_rms(x, g_ref[...]) * (1.0 + scale) + shift).astype(BF16)
    cos, sin = cos_ref[...], sin_ref[...]
    cos_t, sin_t = cost_ref[...], sint_ref[...]

    def proj(a, b):
        return jnp.dot(h, wn_ref[:, a:b], preferred_element_type=F32)

    def proj_t(a, b):
        return _nt_dot(wt_ref[a:b, :], h)

    qt_ref[0] = (_rope_t(proj_t(_R_Q, _R_QI), cos_t, sin_t)
                 * (HEAD_DIM ** -0.5 * math.log2(math.e))).astype(BF16)
    qit_ref[0] = (_rope_t(proj_t(_R_QI, _R_V), cos_t, sin_t) * (IDX_DIM ** -0.5)).astype(BF16)
    vt = proj_t(_R_V, _R_WI).astype(BF16)
    ones = jnp.ones((V_ROWS - HEAD_DIM, vt.shape[1]), BF16)
    for hd in range(N_HEADS):
        vt_ref[0, 0, hd * V_ROWS:hd * V_ROWS + HEAD_DIM, :] = vt[hd * HEAD_DIM:(hd + 1) * HEAD_DIM, :]
        vt_ref[0, 0, hd * V_ROWS + HEAD_DIM:(hd + 1) * V_ROWS, :] = ones
    wit_ref[0] = proj_t(_R_WI, _R_END)[0:N_IDX_HEADS, :] * (N_IDX_HEADS ** -0.5)
    k_ref[0] = _rope(proj(_C_K, _C_KK), cos, sin).astype(BF16)
    kk_ref[0] = _rope(proj(_C_KK, _C_XR), cos, sin).astype(BF16)
    xr_ref[0] = proj(_C_XR, _C_XG)
    xg_ref[0] = proj(_C_XG, _C_END)


def _in_proj_call(x, mod, g, w_nat, w_tr, cos2, sin2, cos_t, sin_t):
    b, s, d = x.shape
    tm = min(K_BLOCK, s)
    nt = s // tm
    tile = lambda n: pl.BlockSpec((1, tm, n), lambda bi, ti: (bi, ti, 0))
    tile_t = lambda n: pl.BlockSpec((1, n, tm), lambda bi, ti: (bi, 0, ti))
    const = lambda shape: pl.BlockSpec(shape, lambda bi, ti: (0, 0))
    shp = lambda n, dt: jax.ShapeDtypeStruct((b, s, n), dt)
    shp_t = lambda n, dt: jax.ShapeDtypeStruct((b, n, s), dt)
    return pl.pallas_call(
        _in_proj_kernel,
        grid=(b, nt),
        in_specs=[tile(d),
                  pl.BlockSpec((1, N_MOD, d), lambda bi, ti: (bi, 0, 0)),
                  const((1, d)), const((d, _C_END)), const((_R_END, d)),
                  pl.BlockSpec((tm, LANES), lambda bi, ti: (ti, 0)),
                  pl.BlockSpec((tm, LANES), lambda bi, ti: (ti, 0)),
                  pl.BlockSpec((HEAD_DIM, tm), lambda bi, ti: (0, ti)),
                  pl.BlockSpec((HEAD_DIM, tm), lambda bi, ti: (0, ti))],
        out_specs=[tile_t(D_ATTN), tile_t(D_ATTN), tile_t(N_IDX_HEADS), tile(D_ATTN), tile(LANES),
                   pl.BlockSpec((1, 1, N_HEADS * V_ROWS, tm), lambda bi, ti: (bi, ti, 0, 0)),
                   tile(D_RNN), tile(D_RNN)],
        out_shape=[shp_t(D_ATTN, BF16), shp_t(D_ATTN, BF16), shp_t(N_IDX_HEADS, F32),
                   shp(D_ATTN, BF16), shp(LANES, BF16),
                   jax.ShapeDtypeStruct((b, nt, N_HEADS * V_ROWS, tm), BF16),
                   shp(D_RNN, F32), shp(D_RNN, F32)],
        compiler_params=_cparams(("parallel", "parallel")),
        name="in_proj",
    )(x, mod, g, w_nat, w_tr, cos2, sin2, cos_t, sin_t)


def _sortable(score):
    score = jnp.where(score == 0.0, 0.0, score)
    bits = pltpu.bitcast(score, I32)
    return jnp.where(bits < 0, bits ^ jnp.int32(0x7FFFFFFF), bits)


def _row_fold(x, op):
    acc = x[0:SUBLANES, :]
    for t in range(1, x.shape[0] // SUBLANES):
        acc = op(acc, x[t * SUBLANES:(t + 1) * SUBLANES, :])
    return acc


def _fold16(x):
    rows = 2 * SUBLANES
    acc = x[0:rows, :]
    for t in range(1, x.shape[0] // rows):
        acc = acc + x[t * rows:(t + 1) * rows, :]
    return acc


def _attn_kernel(qt_ref, qit_ref, wit_ref, kk_ref, k_ref, vt_ref, o_ref,
                 keys_ref, hi_ref, lo_ref, bias_ref, s_ref, qm_ref, qim_ref, m_ref, l_ref, acc_ref,
                 *, n_sel, qb, kb):
    i = pl.program_id(1)
    n_ch = ((i + 1) * qb + kb - 1) // kb
    q_pos = i * qb + lax.broadcasted_iota(I32, (1, qb), 1)
    limit = (q_pos // CHUNK + 1) * CHUNK
    krow = lax.broadcasted_iota(I32, (kb, 1), 0)
    row128 = lax.broadcasted_iota(I32, (LANES, 1), 0)

    for h in range(N_HEADS):
        rows = slice((h // 2) * LANES, (h // 2 + 1) * LANES)
        keep = (row128 < HEAD_DIM) if h % 2 == 0 else (row128 >= HEAD_DIM)
        qs, qis = qt_ref[0, rows, :], qit_ref[0, rows, :]
        qm_ref[h] = jnp.where(keep, qs, jnp.zeros_like(qs))
        qim_ref[h] = jnp.where(keep, qis, jnp.zeros_like(qis))
    m_ref[...] = jnp.full(m_ref.shape, NEG_INF, F32)
    l_ref[...] = jnp.zeros(l_ref.shape, F32)
    acc_ref[...] = jnp.zeros(acc_ref.shape, F32)

    wi = wit_ref[0]

    def score_body(j, carry):
        kk = kk_ref[0, pl.ds(pl.multiple_of(j * kb, kb), kb), :]
        acc = jnp.zeros((kb, qb), F32)
        for h in range(N_IDX_HEADS):
            d = jnp.dot(kk, qim_ref[h], preferred_element_type=F32)
            acc = acc + wi[h:h + 1, :] * jnp.maximum(d, 0.0)
        key = jnp.where(j * kb + krow < limit, _sortable(acc), jnp.int32(INT_MIN))
        keys_ref[j] = key
        hi_ref[j] = (key >> 16).astype(I16)
        return carry

    lax.fori_loop(0, n_ch, score_body, 0)

    def count16(ref, pred):
        def body(j, acc):
            return acc + _fold16(jnp.where(pred(ref[j]), jnp.int16(1), jnp.int16(0)))
        acc = lax.fori_loop(0, n_ch, body, jnp.zeros((2 * SUBLANES, qb), I16))
        return jnp.sum(acc.astype(I32), axis=0, keepdims=True)

    def search16(ref, target):
        c0 = count16(ref, lambda v: v >= jnp.int16(0))
        t0 = jnp.where(c0 >= target, jnp.int32(0), jnp.int32(INT16_MIN))

        def bit_body(bi, t):
            cand = t | (jnp.int32(1) << (14 - bi))
            c = count16(ref, lambda v: v >= cand.astype(I16))
            return jnp.where(c >= target, cand, t)

        return lax.fori_loop(0, 15, bit_body, t0)

    tau_hi = search16(hi_ref, n_sel)
    tau_hi16 = tau_hi.astype(I16)
    n_gt_hi = count16(hi_ref, lambda v: v > tau_hi16)

    def lo_body(j, carry):
        lo = (keys_ref[j] ^ jnp.int32(0x8000)).astype(I16)
        lo_ref[j] = jnp.where(hi_ref[j] == tau_hi16, lo, jnp.int16(INT16_MIN))
        return carry

    lax.fori_loop(0, n_ch, lo_body, 0)
    tau_lo = search16(lo_ref, n_sel - n_gt_hi)
    tau_lo16 = tau_lo.astype(I16)
    n_gt_lo = count16(lo_ref, lambda v: v > tau_lo16)
    n_ge_lo = count16(lo_ref, lambda v: v >= tau_lo16)
    tau = (tau_hi << 16) | ((tau_lo + 32768) & jnp.int32(0xFFFF))
    all_sel = tau == jnp.int32(INT_MIN)
    need_i = jnp.where(all_sel, 0, n_sel - n_gt_hi - n_gt_lo)
    need = need_i.astype(F32)
    ties = jnp.max(jnp.where(all_sel, 0, (n_ge_lo - n_gt_lo) - need_i)) > 0
    thr = jnp.where(all_sel, jnp.int32(INT_MIN + 1), tau)

    def bias_ties(j, run):
        r_i = lax.broadcasted_iota(I32, (kb, kb), 0)
        c_i = lax.broadcasted_iota(I32, (kb, kb), 1)
        tri = (c_i <= r_i).astype(BF16)
        kc = keys_ref[j]
        eq = kc == tau
        rank = jnp.dot(tri, eq.astype(BF16), preferred_element_type=F32) + run
        sel = (kc > tau) | (eq & (rank <= need))
        bias_ref[...] = jnp.where(sel, 0.0, NEG_INF).astype(F32)
        return rank[kb - 1:kb, :]

    def bias_plain(j, run):
        bias_ref[...] = jnp.where(keys_ref[j] >= thr, 0.0, NEG_INF).astype(F32)
        return run

    def attn_body(j, run):
        run = lax.cond(ties, bias_ties, bias_plain, j, run)
        rows = pl.ds(pl.multiple_of(j * kb, kb), kb)

        def logits(h):
            cols = slice((h // 2) * LANES, (h // 2 + 1) * LANES)
            s = jnp.dot(k_ref[0, rows, cols], qm_ref[h], preferred_element_type=F32) + bias_ref[...]
            s_ref[h % 2] = s
            return jnp.max(_row_fold(s, jnp.maximum), axis=0, keepdims=True)

        cm = logits(0)
        for h in range(N_HEADS):
            cm_next = logits(h + 1) if h + 1 < N_HEADS else None
            m_old = m_ref[h]
            m_new = jnp.maximum(m_old, cm)
            m_safe = jnp.where(m_new == NEG_INF, 0.0, m_new)
            alpha = jnp.exp2(m_old - m_safe)
            p = jnp.exp2(s_ref[h % 2] - m_safe)
            pv = jnp.dot(vt_ref[0, j, h * V_ROWS:(h + 1) * V_ROWS, :], p.astype(BF16),
                         preferred_element_type=F32)
            l_ref[h] = alpha * l_ref[h] + pv[HEAD_DIM:HEAD_DIM + 1, :]
            acc_ref[h] = alpha * acc_ref[h] + pv[0:HEAD_DIM, :]
            m_ref[h] = m_new
            cm = cm_next
        return run

    lax.fori_loop(0, n_ch, attn_body, jnp.zeros((1, qb), F32))

    outs = [acc_ref[h] / l_ref[h] for h in range(N_HEADS)]
    o_ref[0] = jnp.concatenate(outs, axis=0).T.astype(o_ref.dtype)


def _attn_call(qt, qit, wit, kk, k, vt):
    b, s, _ = k.shape
    qb = min(Q_BLOCK, s)
    kb = min(K_BLOCK, s)
    n_sel = min(TOPK_MAX, s // 4)
    blk_t = lambda n: pl.BlockSpec((1, n, qb), lambda bi, qi_: (bi, 0, qi_))
    once = pl.Buffered(1)
    kern = functools.partial(_attn_kernel, n_sel=n_sel, qb=qb, kb=kb)
    return pl.pallas_call(
        kern,
        grid=(b, s // qb),
        in_specs=[blk_t(D_ATTN), blk_t(D_ATTN), blk_t(N_IDX_HEADS),
                  pl.BlockSpec((1, s, LANES), lambda bi, qi_: (bi, 0, 0), pipeline_mode=once),
                  pl.BlockSpec((1, s, D_ATTN), lambda bi, qi_: (bi, 0, 0), pipeline_mode=once),
                  pl.BlockSpec((1, s // kb, N_HEADS * V_ROWS, kb), lambda bi, qi_: (bi, 0, 0, 0),
                               pipeline_mode=once)],
        out_specs=pl.BlockSpec((1, qb, D_ATTN), lambda bi, qi_: (bi, qi_, 0)),
        out_shape=jax.ShapeDtypeStruct((b, s, D_ATTN), BF16),
        scratch_shapes=[pltpu.VMEM((s // kb, kb, qb), I32),
                        pltpu.VMEM((s // kb, kb, qb), I16),
                        pltpu.VMEM((s // kb, kb, qb), I16),
                        pltpu.VMEM((kb, qb), F32),
                        pltpu.VMEM((2, kb, qb), F32),
                        pltpu.VMEM((N_HEADS, LANES, qb), BF16),
                        pltpu.VMEM((N_IDX_HEADS, LANES, qb), BF16),
                        pltpu.VMEM((N_HEADS, 1, qb), F32),
                        pltpu.VMEM((N_HEADS, 1, qb), F32),
                        pltpu.VMEM((N_HEADS, HEAD_DIM, qb), F32)],
        compiler_params=_cparams(("parallel", "arbitrary")),
        name="attn",
    )(qt, qit, wit, kk, k, vt)


def _gelu_tanh(x):
    return 0.5 * x * (1.0 + jnp.tanh(math.sqrt(2.0 / math.pi) * (x + 0.044715 * (x * x * x))))


def _rglru_kernel(xr_ref, xg_ref, cw_ref, cb_ref, wa_ref, ba_ref, wx_ref, bx_ref, lam_ref,
                  o_ref, xbuf_ref, h_ref, *, ts):
    t = pl.program_id(1)
    pad = 8

    @pl.when(t == 0)
    def _():
        xbuf_ref[0:pad, :] = jnp.zeros((pad, D_RNN), F32)
        h_ref[...] = jnp.zeros_like(h_ref)

    xr = xr_ref[0]
    xbuf_ref[pad:pad + ts, :] = xr
    conv = cb_ref[...] + cw_ref[CONV_WIDTH - 1:CONV_WIDTH, :] * xr
    for w in range(CONV_WIDTH - 1):
        off = pad - (CONV_WIDTH - 1) + w
        conv = conv + cw_ref[w:w + 1, :] * xbuf_ref[off:off + ts, :]
    xbuf_ref[0:pad, :] = xr[ts - pad:ts, :]

    cb16 = conv.astype(BF16)
    r = jax.nn.sigmoid(jnp.dot(cb16, wa_ref[...], preferred_element_type=F32) + ba_ref[...])
    ig = jax.nn.sigmoid(jnp.dot(cb16, wx_ref[...], preferred_element_type=F32) + bx_ref[...])
    z = -lam_ref[...]
    softplus = jnp.maximum(z, 0.0) + jnp.log1p(jnp.exp(-jnp.abs(z)))
    log_a = (-RG_C) * r * softplus
    a = jnp.exp(log_a)
    bb = jnp.sqrt(1.0 - a * a) * (ig * conv)

    row = lax.broadcasted_iota(I32, (ts, 1), 0)
    d = 1
    while d < ts:
        live = row >= d
        a_s = jnp.where(live, pltpu.roll(a, d, axis=0), 1.0)
        b_s = jnp.where(live, pltpu.roll(bb, d, axis=0), 0.0)
        bb = bb + a * b_s
        a = a * a_s
        d *= 2
    h = bb + a * h_ref[0:1, :]
    h_ref[0:1, :] = h[ts - 1:ts, :]
    o_ref[0] = (h * _gelu_tanh(xg_ref[0])).astype(o_ref.dtype)


def _rglru_call(xr, xg, conv_w, conv_b, wa, ba, wx, bx, lam):
    b, s, r = xr.shape
    ts = min(TOKEN_TILE, s)
    tile = pl.BlockSpec((1, ts, r), lambda bi, ti: (bi, ti, 0))
    const = lambda shape: pl.BlockSpec(shape, lambda bi, ti: (0, 0))
    return pl.pallas_call(
        functools.partial(_rglru_kernel, ts=ts),
        grid=(b, s // ts),
        in_specs=[tile, tile, const((CONV_WIDTH, r)), const((1, r)), const((r, r)), const((1, r)),
                  const((r, r)), const((1, r)), const((1, r))],
        out_specs=tile,
        out_shape=jax.ShapeDtypeStruct((b, s, r), BF16),
        scratch_shapes=[pltpu.VMEM((ts + 8, r), F32), pltpu.VMEM((8, r), F32)],
        compiler_params=_cparams(("parallel", "arbitrary")),
        name="rglru",
    )(xr, xg, conv_w, conv_b, wa, ba, wx, bx, lam)


def _route(logits):
    lane = lax.broadcasted_iota(I32, logits.shape, 1)
    is_g = (lane >= N_EXPERTS) & (lane < N_EXPERTS + N_GROUPS)
    big = jnp.int32(LANES)

    def first_lane(mask):
        return jnp.min(jnp.where(mask, lane, big), axis=1, keepdims=True)

    gl = jnp.where(is_g, logits, NEG_INF)
    ge = jnp.exp(gl - jnp.max(gl, axis=1, keepdims=True))
    p_groups = ge / jnp.sum(ge, axis=1, keepdims=True)
    p_g = jnp.max(p_groups, axis=1, keepdims=True)
    g_idx = first_lane(is_g & (p_groups == p_g)) - N_EXPERTS

    in_grp = (lane < N_EXPERTS) & ((lane // EXP_PER_GROUP) == g_idx)
    el = jnp.where(in_grp, logits, NEG_INF)
    ee = jnp.exp(el - jnp.max(el, axis=1, keepdims=True))
    es = ee / jnp.sum(ee, axis=1, keepdims=True)
    es = jnp.where(in_grp, es, NEG_INF)
    v1 = jnp.max(es, axis=1, keepdims=True)
    i1 = first_lane(in_grp & (es == v1))
    es2 = jnp.where(lane == i1, NEG_INF, es)
    v2 = jnp.max(es2, axis=1, keepdims=True)
    i2 = first_lane(in_grp & (es2 == v2))
    tot = v1 + v2
    e_w = jnp.where(lane == i1, v1 / tot, 0.0) + jnp.where(lane == i2, v2 / tot, 0.0)
    return p_g * e_w + jnp.where(lane == ROUTE_GROUP_LANE + g_idx, 1.0, 0.0)


def _out_proj_kernel(ya_ref, yr_ref, x_ref, mod_ref, wo_ref, gpm_ref, gpf_ref, wr_ref, br_ref,
                     x1_ref, h2_ref, comb_ref):
    mix = (jnp.dot(ya_ref[0], wo_ref[0:D_ATTN, :], preferred_element_type=F32)
           + jnp.dot(yr_ref[0], wo_ref[D_ATTN:D_ATTN + D_RNN, :], preferred_element_type=F32))
    gate1 = mod_ref[0, 2:3, :]
    shift2 = mod_ref[0, 3:4, :]
    scale2 = mod_ref[0, 4:5, :]
    x1 = x_ref[0] + gate1 * _rms(mix, gpm_ref[...])
    x1_ref[0] = x1
    h2 = _rms(x1, gpf_ref[...]) * (1.0 + scale2) + shift2
    h2_ref[0] = h2.astype(BF16)
    logits = jnp.dot(h2, wr_ref[...], preferred_element_type=F32,
                     precision=lax.Precision.HIGHEST) + br_ref[...]
    comb_ref[0] = _route(logits)


def _out_proj_call(ya, yr, x, mod, wo, gpm, gpf, wr, br):
    b, s, d = x.shape
    tm = min(TOKEN_TILE, s)
    tile = lambda n: pl.BlockSpec((1, tm, n), lambda bi, ti: (bi, ti, 0))
    const = lambda shape: pl.BlockSpec(shape, lambda bi, ti: (0,) * len(shape))
    return pl.pallas_call(
        _out_proj_kernel,
        grid=(b, s // tm),
        in_specs=[tile(D_ATTN), tile(D_RNN), tile(d),
                  pl.BlockSpec((1, N_MOD, d), lambda bi, ti: (bi, 0, 0)),
                  const((D_ATTN + D_RNN, d)), const((1, d)), const((1, d)),
                  const((d, LANES)), const((1, LANES))],
        out_specs=[tile(d), tile(d), tile(LANES)],
        out_shape=[jax.ShapeDtypeStruct((b, s, d), F32), jax.ShapeDtypeStruct((b, s, d), BF16),
                   jax.ShapeDtypeStruct((b, s, LANES), F32)],
        compiler_params=_cparams(("parallel", "parallel")),
        name="out_proj",
    )(ya, yr, x, mod, wo, gpm, gpf, wr, br)


def _moe_kernel(h2_ref, comb_ref, x1_ref, mod_ref, wg_ref, wu_ref, wd_ref, gpo_ref,
                o_ref, hs_ref, cs_ref, ys_ref, act_ref, slot_ref, seg_ref, *, tm, ns, rb):
    g = pl.program_id(2)
    lane = lax.broadcasted_iota(I32, (1, LANES), 1)
    n_blk = tm // LANES

    @pl.when(g == 0)
    def _dispatch():
        comb = comb_ref[0]
        oh = jnp.where((lane >= ROUTE_GROUP_LANE) & (lane < ROUTE_GROUP_LANE + N_GROUPS), comb, 0.0)
        oh16 = oh.astype(BF16)
        sub = lax.broadcasted_iota(I32, (LANES, 1), 0)
        base = jnp.int32(0)
        base_lane = jnp.zeros((1, LANES), F32)
        base_sub = jnp.zeros((LANES, 1), F32)
        for gg in range(N_GROUPS):
            here = ROUTE_GROUP_LANE + gg
            n = jnp.sum(jnp.where(lane == here, oh, 0.0)).astype(I32)
            nb = (n + rb - 1) // rb
            seg_ref[gg] = base
            seg_ref[N_GROUPS + gg] = nb
            base_f = base.astype(F32)
            base_lane = base_lane + jnp.where(lane == here, base_f, 0.0)
            base_sub = base_sub + jnp.where(sub == here, base_f, 0.0)
            base = base + nb * rb

        tok_c = lax.broadcasted_iota(I32, (1, tm), 1)
        tok_r = lax.broadcasted_iota(I32, (tm, 1), 0)
        blk_r = lax.broadcasted_iota(I32, (LANES, 1), 0)
        blk_c = lax.broadcasted_iota(I32, (1, LANES), 1)
        eye = (blk_r == blk_c).astype(BF16)
        oht16 = _nt_dot(eye, oh16).astype(BF16)
        slot_row = []
        for t in range(n_blk):
            tril = (tok_c <= blk_r + t * LANES).astype(BF16)
            rank = jnp.dot(tril, oh16, preferred_element_type=F32)
            oh_blk = oh[t * LANES:(t + 1) * LANES, :]
            slot_ref[t * LANES:(t + 1) * LANES, :] = jnp.sum(
                oh_blk * (base_lane + rank - 1.0), axis=1, keepdims=True).astype(I32)
            triu = (tok_r <= blk_c + t * LANES).astype(BF16)
            rank_t = jnp.dot(oht16, triu, preferred_element_type=F32)
            oht_blk = oht16[:, t * LANES:(t + 1) * LANES].astype(F32)
            slot_row.append(jnp.sum(oht_blk * (base_sub + rank_t - 1.0), axis=0,
                                    keepdims=True).astype(I32))
        slot_row = jnp.concatenate(slot_row, axis=1)

        hi = comb.astype(BF16)
        lo = (comb - hi.astype(F32)).astype(BF16)
        cw = jnp.concatenate([hi, lo], axis=1)
        h = h2_ref[0]
        for r in range(ns // LANES):
            rows = slice(r * LANES, (r + 1) * LANES)
            p = (blk_r + r * LANES == slot_row).astype(BF16)
            hs_ref[rows, :] = jnp.dot(p, h, preferred_element_type=F32).astype(BF16)
            cs_ref[rows, :] = jnp.dot(p, cw, preferred_element_type=F32).astype(BF16)
        ys_ref[...] = jnp.zeros_like(ys_ref)

    n_act = EXP_PER_GROUP * D_EXPERT
    r_i = lax.broadcasted_iota(I32, (2 * LANES, n_act), 0) % LANES
    c_i = lax.broadcasted_iota(I32, (2 * LANES, n_act), 1) // D_EXPERT
    expand = (r_i == g * EXP_PER_GROUP + c_i).astype(BF16)
    base = seg_ref[g]

    def block(r, carry):
        rows = pl.ds(pl.multiple_of(base + r * rb, rb), rb)
        hb = hs_ref[rows, :]
        cexp = jnp.dot(cs_ref[rows, :], expand, preferred_element_type=F32)
        for e in range(EXP_PER_GROUP):
            gt = jnp.dot(hb, wg_ref[0, e], preferred_element_type=F32)
            up = jnp.dot(hb, wu_ref[0, e], preferred_element_type=F32)
            cols = slice(e * D_EXPERT, (e + 1) * D_EXPERT)
            act_ref[:, cols] = (gt * jax.nn.sigmoid(gt) * up * cexp[:, cols]).astype(BF16)
        ys_ref[rows, :] = jnp.dot(act_ref[...], wd_ref[0], preferred_element_type=F32).astype(BF16)
        return carry

    lax.fori_loop(0, seg_ref[N_GROUPS + g], block, 0)

    @pl.when(g == N_GROUPS - 1)
    def _combine():
        gate2 = mod_ref[0, 5:6, :]
        slot_c = lax.broadcasted_iota(I32, (1, ns), 1)
        half = tm // 2
        for hh in range(2):
            rows = slice(hh * half, (hh + 1) * half)
            pt = (slot_c == slot_ref[rows, :]).astype(BF16)
            y = jnp.dot(pt, ys_ref[...], preferred_element_type=F32)
            o_ref[0, rows, :] = x1_ref[0, rows, :] + gate2 * _rms(y, gpo_ref[...])


def _moe_call(h2, comb, x1, mod, wg, wu, wd, gpo):
    b, s, d = x1.shape
    tm = min(MOE_TILE, s)
    rb = MOE_ROWS
    ns = tm + N_GROUPS * rb
    tile = lambda n, **kw: pl.BlockSpec((1, tm, n), lambda bi, ti, gi: (bi, ti, 0), **kw)
    kern = functools.partial(_moe_kernel, tm=tm, ns=ns, rb=rb)
    return pl.pallas_call(
        kern,
        grid=(b, s // tm, N_GROUPS),
        in_specs=[tile(d), tile(LANES), tile(d, pipeline_mode=pl.Buffered(1)),
                  pl.BlockSpec((1, N_MOD, d), lambda bi, ti, gi: (bi, 0, 0)),
                  pl.BlockSpec((1, EXP_PER_GROUP, d, D_EXPERT), lambda bi, ti, gi: (gi, 0, 0, 0)),
                  pl.BlockSpec((1, EXP_PER_GROUP, d, D_EXPERT), lambda bi, ti, gi: (gi, 0, 0, 0)),
                  pl.BlockSpec((1, EXP_PER_GROUP * D_EXPERT, d), lambda bi, ti, gi: (gi, 0, 0)),
                  pl.BlockSpec((1, d), lambda bi, ti, gi: (0, 0))],
        out_specs=tile(d),
        out_shape=jax.ShapeDtypeStruct((b, s, d), F32),
        scratch_shapes=[pltpu.VMEM((ns, d), BF16),
                        pltpu.VMEM((ns, 2 * LANES), BF16),
                        pltpu.VMEM((ns, d), BF16),
                        pltpu.VMEM((rb, EXP_PER_GROUP * D_EXPERT), BF16),
                        pltpu.VMEM((tm, 1), I32),
                        pltpu.SMEM((2 * N_GROUPS,), I32)],
        compiler_params=_cparams(("parallel", "parallel", "arbitrary")),
        name="moe",
    )(h2, comb, x1, mod, wg, wu, wd, gpo)


def _rope_tables(seq_len):
    pos = jnp.arange(seq_len, dtype=F32)
    inv = ROPE_THETA ** (-jnp.arange(0, HEAD_DIM, 2, dtype=F32) / HEAD_DIM)
    ang = pos[:, None] * inv[None, :]
    cos, sin = jnp.cos(ang), jnp.sin(ang)
    cos2 = jnp.concatenate([cos, cos, cos, cos], axis=1)
    sin2 = jnp.concatenate([-sin, sin, -sin, sin], axis=1)
    return cos2, sin2, cos2[:, :HEAD_DIM].T, sin2[:, :HEAD_DIM].T


def _block_diag(w):
    n, c, d = w.shape
    eye = jnp.eye(n, dtype=w.dtype)
    return (eye[:, None, :, None] * w[:, :, None, :]).reshape(n * c, n * d)


def _pad_cols(w, n):
    return jnp.pad(w, ((0, 0), (0, n - w.shape[1])))


def kernel(x, c, w_ada, b_ada, g_pre_mix, g_post_mix, g_pre_ffn, g_post_ffn, w_in, conv_w, conv_b, w_rg_a, b_rg_a, w_rg_x, b_rg_x, lru_lambda, w_out, w_router_group, b_router_group, w_router_expert, b_router_expert, w_gate, w_up, w_down):
    b, s, d = x.shape
    depth = w_ada.shape[0]
    cos2, sin2, cos_t, sin_t = _rope_tables(s)
    c_pad = jnp.pad(c, ((0, (-b) % SUBLANES), (0, 0)))
    o_q, o_k, o_v, o_qi, o_ki, o_wi, o_xr, o_xg = 0, 512, 1024, 1536, 2048, 2112, 2120, 2632
    for l in range(depth):
        wl = w_in[l]
        w_ki = wl[:, o_ki:o_wi]
        w_nat = jnp.concatenate([wl[:, o_k:o_v], w_ki, w_ki, wl[:, o_xr:o_xg], wl[:, o_xg:]],
                                axis=1).astype(BF16)
        w_tr = jnp.concatenate([wl[:, o_q:o_k], wl[:, o_qi:o_ki], wl[:, o_v:o_qi],
                                _pad_cols(wl[:, o_wi:o_xr], _R_END - _R_WI)], axis=1).T.astype(BF16)
        w_route = _pad_cols(jnp.concatenate([w_router_expert[l], w_router_group[l]], axis=1), LANES)
        b_route = _pad_cols(jnp.concatenate([b_router_expert[l], b_router_group[l]])[None, :], LANES)

        mod = _mod_call(c_pad, w_ada[l], b_ada[l][None, :])[:b].reshape(b, N_MOD, d)
        qt, qit, wit, k, kk, vt, xr, xg = _in_proj_call(
            x, mod, g_pre_mix[l][None, :], w_nat, w_tr, cos2, sin2, cos_t, sin_t)
        y_attn = _attn_call(qt, qit, wit, kk, k, vt)
        y_rnn = _rglru_call(xr, xg, conv_w[l], conv_b[l][None, :],
                            _block_diag(w_rg_a[l]).astype(BF16), b_rg_a[l][None, :],
                            _block_diag(w_rg_x[l]).astype(BF16), b_rg_x[l][None, :],
                            lru_lambda[l][None, :])
        x1, h2, comb = _out_proj_call(y_attn, y_rnn, x, mod, w_out[l].astype(BF16),
                                      g_post_mix[l][None, :], g_pre_ffn[l][None, :], w_route, b_route)
        x = _moe_call(h2, comb, x1, mod, w_gate[l].astype(BF16), w_up[l].astype(BF16),
                      w_down[l].reshape(N_GROUPS, EXP_PER_GROUP * D_EXPERT, d).astype(BF16),
                      g_post_ffn[l][None, :])
    return x
```

```python
import functools
import math

import jax
import jax.numpy as jnp
from jax import lax
from jax.experimental import pallas as pl
from jax.experimental.pallas import tpu as pltpu

F32 = jnp.float32
BF16 = jnp.bfloat16
I32 = jnp.int32
I16 = jnp.int16

D_MODEL = 1024
CHUNK = 64
ROPE_THETA = 10000.0
EPS = 1e-6
N_HEADS = 8
HEAD_DIM = 64
D_ATTN = N_HEADS * HEAD_DIM
N_IDX_HEADS = 8
IDX_DIM = 64
TOPK_MAX = 256
D_RNN = 512
N_RNN_BLOCKS = 8
CONV_WIDTH = 4
RG_C = 8.0
N_GROUPS = 4
EXP_PER_GROUP = 8
N_EXPERTS = N_GROUPS * EXP_PER_GROUP
D_EXPERT = 256
N_MOD = 6

LANES = 128
SUBLANES = 8
INT_MIN = -2 ** 31
INT16_MIN = -2 ** 15
NEG_INF = float("-inf")

Q_BLOCK = 256
K_BLOCK = 512
K_SUB = 128
V_ROWS = HEAD_DIM + 16
TOKEN_TILE = 512
MOE_TILE = 1024
MOE_ROWS = 128
ROUTE_GROUP_LANE = 64
VMEM_LIMIT = 60 * 1024 * 1024


def _cparams(sem):
    return pltpu.CompilerParams(dimension_semantics=sem, vmem_limit_bytes=VMEM_LIMIT)


def _nt_dot(a, b):
    return lax.dot_general(a, b, (((1,), (1,)), ((), ())), preferred_element_type=F32)


def _rms(x, g):
    return x * lax.rsqrt(jnp.mean(x * x, axis=-1, keepdims=True) + EPS) * g


def _mod_kernel(c_ref, w_ref, b_ref, o_ref):
    c = c_ref[...]
    sc = c * jax.nn.sigmoid(c)
    o_ref[...] = jnp.dot(sc, w_ref[...], preferred_element_type=F32,
                         precision=lax.Precision.HIGHEST) + b_ref[...]


def _mod_call(c_pad, w_ada, b_ada):
    rows, d = c_pad.shape
    n = w_ada.shape[1]
    bn = 1024
    return pl.pallas_call(
        _mod_kernel,
        grid=(n // bn,),
        in_specs=[pl.BlockSpec((rows, d), lambda j: (0, 0)),
                  pl.BlockSpec((d, bn), lambda j: (0, j)),
                  pl.BlockSpec((1, bn), lambda j: (0, j))],
        out_specs=pl.BlockSpec((rows, bn), lambda j: (0, j)),
        out_shape=jax.ShapeDtypeStruct((rows, n), F32),
        compiler_params=_cparams(("arbitrary",)),
        name="mod",
    )(c_pad, w_ada, b_ada)


_R_Q, _R_QI, _R_V, _R_WI, _R_END = 0, 512, 1024, 1536, 1552
_C_K, _C_KK, _C_XR, _C_XG, _C_END = 0, 512, 640, 1152, 1664


def _rope(y, cos, sin):
    lane = lax.broadcasted_iota(I32, (1, LANES), 1)
    low = (lane % HEAD_DIM) < (HEAD_DIM // 2)
    outs = []
    for p in range(y.shape[1] // LANES):
        s = y[:, p * LANES:(p + 1) * LANES]
        swapped = jnp.where(low, pltpu.roll(s, LANES - HEAD_DIM // 2, axis=1),
                            pltpu.roll(s, HEAD_DIM // 2, axis=1))
        outs.append(s * cos + swapped * sin)
    return outs[0] if len(outs) == 1 else jnp.concatenate(outs, axis=1)


def _rope_t(y, cos, sin):
    half = HEAD_DIM // 2
    outs = []
    for hd in range(y.shape[0] // HEAD_DIM):
        blk = y[hd * HEAD_DIM:(hd + 1) * HEAD_DIM, :]
        swapped = jnp.concatenate([blk[half:, :], blk[:half, :]], axis=0)
        outs.append(blk * cos + swapped * sin)
    return jnp.concatenate(outs, axis=0)


def _in_proj_kernel(x_ref, mod_ref, g_ref, wn_ref, wt_ref, cos_ref, sin_ref, cost_ref, sint_ref,
                    qt_ref, qit_ref, wit_ref, k_ref, kk_ref, vt_ref, xr_ref, xg_ref):
    x = x_ref[0]
    shift = mod_ref[0, 0:1, :]
    scale = mod_ref[0, 1:2, :]
    h = (_rms(x, g_ref[...]) * (1.0 + scale) + shift).astype(BF16)
    cos, sin = cos_ref[...], sin_ref[...]
    cos_t, sin_t = cost_ref[...], sint_ref[...]

    def proj(a, b):
        return jnp.dot(h, wn_ref[:, a:b], preferred_element_type=F32)

    def proj_t(a, b):
        return _nt_dot(wt_ref[a:b, :], h)

    qt_ref[0] = (_rope_t(proj_t(_R_Q, _R_QI), cos_t, sin_t)
                 * (HEAD_DIM ** -0.5 * math.log2(math.e))).astype(BF16)
    qit_ref[0] = (_rope_t(proj_t(_R_QI, _R_V), cos_t, sin_t) * (IDX_DIM ** -0.5)).astype(BF16)
    vt = proj_t(_R_V, _R_WI).astype(BF16)
    ones = jnp.ones((V_ROWS - HEAD_DIM, vt.shape[1]), BF16)
    for hd in range(N_HEADS):
        vt_ref[0, 0, hd * V_ROWS:hd * V_ROWS + HEAD_DIM, :] = vt[hd * HEAD_DIM:(hd + 1) * HEAD_DIM, :]
        vt_ref[0, 0, hd * V_ROWS + HEAD_DIM:(hd + 1) * V_ROWS, :] = ones
    wit_ref[0] = proj_t(_R_WI, _R_END)[0:N_IDX_HEADS, :] * (N_IDX_HEADS ** -0.5)
    k_ref[0] = _rope(proj(_C_K, _C_KK), cos, sin).astype(BF16)
    kk_ref[0] = _rope(proj(_C_KK, _C_XR), cos, sin).astype(BF16)
    xr_ref[0] = proj(_C_XR, _C_XG)
    xg_ref[0] = proj(_C_XG, _C_END)


def _in_proj_call(x, mod, g, w_nat, w_tr, cos2, sin2, cos_t, sin_t):
    b, s, d = x.shape
    tm = min(K_BLOCK, s)
    nt = s // tm
    tile = lambda n: pl.BlockSpec((1, tm, n), lambda bi, ti: (bi, ti, 0))
    tile_t = lambda n: pl.BlockSpec((1, n, tm), lambda bi, ti: (bi, 0, ti))
    const = lambda shape: pl.BlockSpec(shape, lambda bi, ti: (0, 0))
    shp = lambda n, dt: jax.ShapeDtypeStruct((b, s, n), dt)
    shp_t = lambda n, dt: jax.ShapeDtypeStruct((b, n, s), dt)
    return pl.pallas_call(
        _in_proj_kernel,
        grid=(b, nt),
        in_specs=[tile(d),
                  pl.BlockSpec((1, N_MOD, d), lambda bi, ti: (bi, 0, 0)),
                  const((1, d)), const((d, _C_END)), const((_R_END, d)),
                  pl.BlockSpec((tm, LANES), lambda bi, ti: (ti, 0)),
                  pl.BlockSpec((tm, LANES), lambda bi, ti: (ti, 0)),
                  pl.BlockSpec((HEAD_DIM, tm), lambda bi, ti: (0, ti)),
                  pl.BlockSpec((HEAD_DIM, tm), lambda bi, ti: (0, ti))],
        out_specs=[tile_t(D_ATTN), tile_t(D_ATTN), tile_t(N_IDX_HEADS), tile(D_ATTN), tile(LANES),
                   pl.BlockSpec((1, 1, N_HEADS * V_ROWS, tm), lambda bi, ti: (bi, ti, 0, 0)),
                   tile(D_RNN), tile(D_RNN)],
        out_shape=[shp_t(D_ATTN, BF16), shp_t(D_ATTN, BF16), shp_t(N_IDX_HEADS, F32),
                   shp(D_ATTN, BF16), shp(LANES, BF16),
                   jax.ShapeDtypeStruct((b, nt, N_HEADS * V_ROWS, tm), BF16),
                   shp(D_RNN, F32), shp(D_RNN, F32)],
        compiler_params=_cparams(("parallel", "parallel")),
        name="in_proj",
    )(x, mod, g, w_nat, w_tr, cos2, sin2, cos_t, sin_t)


def _sortable(score):
    bits = pltpu.bitcast(score, I32)
    return jnp.where(bits < 0, jnp.int32(INT_MIN) - bits, bits)


def _tile_fold(x, op, rows):
    parts = [x[t * rows:(t + 1) * rows, :] for t in range(x.shape[0] // rows)]
    while len(parts) > 1:
        nxt = [op(parts[t], parts[t + 1]) for t in range(0, len(parts) - 1, 2)]
        if len(parts) % 2:
            nxt.append(parts[-1])
        parts = nxt
    return parts[0]


def _row_fold(x, op):
    acc = x[0:SUBLANES, :]
    for t in range(1, x.shape[0] // SUBLANES):
        acc = op(acc, x[t * SUBLANES:(t + 1) * SUBLANES, :])
    return acc


def _fold16(x):
    return _tile_fold(x, jnp.add, 2 * SUBLANES)


def _attn_kernel(qt_ref, qit_ref, wit_ref, kk_ref, k_ref, vt_ref, o_ref,
                 keys_ref, hi_ref, lo_ref, bias_ref, s_ref, p_ref, qm_ref, qim_ref, m_ref, l_ref,
                 acc_ref, *, n_sel, qb, kb, sub):
    i = pl.program_id(1)
    n_ch = ((i + 1) * qb + kb - 1) // kb
    q_pos = i * qb + lax.broadcasted_iota(I32, (1, qb), 1)
    limit = (q_pos // CHUNK + 1) * CHUNK
    krow = lax.broadcasted_iota(I32, (kb, 1), 0)
    row128 = lax.broadcasted_iota(I32, (LANES, 1), 0)

    for h in range(N_HEADS):
        rows = slice((h // 2) * LANES, (h // 2 + 1) * LANES)
        keep = (row128 < HEAD_DIM) if h % 2 == 0 else (row128 >= HEAD_DIM)
        qs, qis = qt_ref[0, rows, :], qit_ref[0, rows, :]
        qm_ref[h] = jnp.where(keep, qs, jnp.zeros_like(qs))
        qim_ref[h] = jnp.where(keep, qis, jnp.zeros_like(qis))
    m_ref[...] = jnp.full(m_ref.shape, NEG_INF, F32)
    l_ref[...] = jnp.zeros(l_ref.shape, F32)
    acc_ref[...] = jnp.zeros(acc_ref.shape, F32)

    wi = wit_ref[0]

    def score_body(j, carry, masked):
        kk = kk_ref[0, pl.ds(pl.multiple_of(j * kb, kb), kb), :]
        acc = jnp.zeros((kb, qb), F32)
        for h in range(N_IDX_HEADS):
            d = jnp.dot(kk, qim_ref[h], preferred_element_type=F32)
            acc = acc + wi[h:h + 1, :] * jnp.maximum(d, 0.0)
        key = _sortable(acc)
        if masked:
            key = jnp.where(j * kb + krow < limit, key, jnp.int32(INT_MIN))
        keys_ref[j] = key
        hi_ref[j] = (key >> 16).astype(I16)
        return carry

    n_open = (i * qb + CHUNK) // kb
    lax.fori_loop(0, n_open, functools.partial(score_body, masked=False), 0)
    lax.fori_loop(n_open, n_ch, functools.partial(score_body, masked=True), 0)

    def count16(ref, pred):
        def body(j, acc):
            return acc + _fold16(jnp.where(pred(ref[j]), jnp.int16(1), jnp.int16(0)))
        acc = lax.fori_loop(0, n_ch, body, jnp.zeros((2 * SUBLANES, qb), I16))
        return jnp.sum(acc.astype(I32), axis=0, keepdims=True)

    def search16(ref, target):
        c0 = count16(ref, lambda v: v >= jnp.int16(0))
        ok = c0 >= target
        init = (jnp.where(ok, jnp.int32(0), jnp.int32(INT16_MIN)),
                jnp.where(ok, c0, n_ch * kb), jnp.where(ok, 0, c0))

        def bit_body(bi, carry):
            t, n_ge, n_gt = carry
            cand = t | (jnp.int32(1) << (14 - bi))
            c = count16(ref, lambda v: v >= cand.astype(I16))
            ok = c >= target
            return jnp.where(ok, cand, t), jnp.where(ok, c, n_ge), jnp.where(ok, n_gt, c)

        return lax.fori_loop(0, 15, bit_body, init)

    tau_hi, _, n_gt_hi = search16(hi_ref, n_sel)
    tau_hi16 = tau_hi.astype(I16)

    def lo_body(j, carry):
        lo = (keys_ref[j] ^ jnp.int32(0x8000)).astype(I16)
        lo_ref[j] = jnp.where(hi_ref[j] == tau_hi16, lo, jnp.int16(INT16_MIN))
        return carry

    lax.fori_loop(0, n_ch, lo_body, 0)
    tau_lo, n_ge_lo, n_gt_lo = search16(lo_ref, n_sel - n_gt_hi)
    tau = (tau_hi << 16) | ((tau_lo + 32768) & jnp.int32(0xFFFF))
    all_sel = tau == jnp.int32(INT_MIN)
    need_i = jnp.where(all_sel, 0, n_sel - n_gt_hi - n_gt_lo)
    need = need_i.astype(F32)
    ties = jnp.max(jnp.where(all_sel, 0, (n_ge_lo - n_gt_lo) - need_i)) > 0
    thr = jnp.where(all_sel, jnp.int32(INT_MIN + 1), tau)

    def bias_ties(j, run):
        r_i = lax.broadcasted_iota(I32, (kb, kb), 0)
        c_i = lax.broadcasted_iota(I32, (kb, kb), 1)
        tri = (c_i <= r_i).astype(BF16)
        kc = keys_ref[j]
        eq = kc == tau
        rank = jnp.dot(tri, eq.astype(BF16), preferred_element_type=F32) + run
        sel = (kc > tau) | (eq & (rank <= need))
        bias_ref[...] = jnp.where(sel, 0.0, NEG_INF).astype(F32)
        return rank[kb - 1:kb, :]

    def bias_plain(j, run):
        bias_ref[...] = jnp.where(keys_ref[j] >= thr, 0.0, NEG_INF).astype(F32)
        return run

    def attn_body(j, run):
        run = lax.cond(ties, bias_ties, bias_plain, j, run)
        n_sub = kb // sub

        def logits(h, r, mx):
            cols = slice((h // 2) * LANES, (h // 2 + 1) * LANES)
            rows = pl.ds(pl.multiple_of(j * kb + r * sub, sub), sub)
            x = (jnp.dot(k_ref[0, rows, cols], qm_ref[h], preferred_element_type=F32)
                 + bias_ref[r * sub:(r + 1) * sub, :])
            s_ref[h % 2, r * sub:(r + 1) * sub, :] = x
            return jnp.maximum(mx, _row_fold(x, jnp.maximum))

        no_max = jnp.full((SUBLANES, qb), NEG_INF, F32)
        mx = no_max
        for r in range(n_sub):
            mx = logits(0, r, mx)
        for h in range(N_HEADS):
            m_old = m_ref[h]
            m_new = jnp.maximum(m_old, jnp.max(mx, axis=0, keepdims=True))
            m_safe = jnp.where(m_new == NEG_INF, 0.0, m_new)
            alpha = jnp.exp2(m_old - m_safe)
            mx = no_max
            for r in range(n_sub):
                if h + 1 < N_HEADS:
                    mx = logits(h + 1, r, mx)
                tile = slice(r * sub, (r + 1) * sub)
                p_ref[tile, :] = jnp.exp2(s_ref[h % 2, tile, :] - m_safe).astype(BF16)
            pv = jnp.dot(vt_ref[0, j, h * V_ROWS:(h + 1) * V_ROWS, :], p_ref[...],
                         preferred_element_type=F32)
            l_ref[h] = alpha * l_ref[h] + pv[HEAD_DIM:HEAD_DIM + 1, :]
            acc_ref[h] = alpha * acc_ref[h] + pv[0:HEAD_DIM, :]
            m_ref[h] = m_new
        return run

    lax.fori_loop(0, n_ch, attn_body, jnp.zeros((1, qb), F32))

    outs = [acc_ref[h] / l_ref[h] for h in range(N_HEADS)]
    o_ref[0] = jnp.concatenate(outs, axis=0).T.astype(o_ref.dtype)


def _attn_call(qt, qit, wit, kk, k, vt):
    b, s, _ = k.shape
    qb = min(Q_BLOCK, s)
    kb = min(K_BLOCK, s)
    n_sel = min(TOPK_MAX, s // 4)
    blk_t = lambda n: pl.BlockSpec((1, n, qb), lambda bi, qi_: (bi, 0, qi_))
    once = pl.Buffered(1)
    kern = functools.partial(_attn_kernel, n_sel=n_sel, qb=qb, kb=kb, sub=min(K_SUB, kb))
    return pl.pallas_call(
        kern,
        grid=(b, s // qb),
        in_specs=[blk_t(D_ATTN), blk_t(D_ATTN), blk_t(N_IDX_HEADS),
                  pl.BlockSpec((1, s, LANES), lambda bi, qi_: (bi, 0, 0), pipeline_mode=once),
                  pl.BlockSpec((1, s, D_ATTN), lambda bi, qi_: (bi, 0, 0), pipeline_mode=once),
                  pl.BlockSpec((1, s // kb, N_HEADS * V_ROWS, kb), lambda bi, qi_: (bi, 0, 0, 0),
                               pipeline_mode=once)],
        out_specs=pl.BlockSpec((1, qb, D_ATTN), lambda bi, qi_: (bi, qi_, 0)),
        out_shape=jax.ShapeDtypeStruct((b, s, D_ATTN), BF16),
        scratch_shapes=[pltpu.VMEM((s // kb, kb, qb), I32),
                        pltpu.VMEM((s // kb, kb, qb), I16),
                        pltpu.VMEM((s // kb, kb, qb), I16),
                        pltpu.VMEM((kb, qb), F32),
                        pltpu.VMEM((2, kb, qb), F32),
                        pltpu.VMEM((kb, qb), BF16),
                        pltpu.VMEM((N_HEADS, LANES, qb), BF16),
                        pltpu.VMEM((N_IDX_HEADS, LANES, qb), BF16),
                        pltpu.VMEM((N_HEADS, 1, qb), F32),
                        pltpu.VMEM((N_HEADS, 1, qb), F32),
                        pltpu.VMEM((N_HEADS, HEAD_DIM, qb), F32)],
        compiler_params=_cparams(("parallel", "arbitrary")),
        name="attn",
    )(qt, qit, wit, kk, k, vt)


def _gelu_tanh(x):
    return 0.5 * x * (1.0 + jnp.tanh(math.sqrt(2.0 / math.pi) * (x + 0.044715 * (x * x * x))))


def _rglru_kernel(xr_ref, xg_ref, cw_ref, cb_ref, wa_ref, ba_ref, wx_ref, bx_ref, lam_ref,
                  o_ref, xbuf_ref, h_ref, *, ts):
    t = pl.program_id(1)
    pad = 8

    @pl.when(t == 0)
    def _():
        xbuf_ref[0:pad, :] = jnp.zeros((pad, D_RNN), F32)
        h_ref[...] = jnp.zeros_like(h_ref)

    xr = xr_ref[0]
    xbuf_ref[pad:pad + ts, :] = xr
    conv = cb_ref[...] + cw_ref[CONV_WIDTH - 1:CONV_WIDTH, :] * xr
    for w in range(CONV_WIDTH - 1):
        off = pad - (CONV_WIDTH - 1) + w
        conv = conv + cw_ref[w:w + 1, :] * xbuf_ref[off:off + ts, :]
    xbuf_ref[0:pad, :] = xr[ts - pad:ts, :]

    cb16 = conv.astype(BF16)
    r = jax.nn.sigmoid(jnp.dot(cb16, wa_ref[...], preferred_element_type=F32) + ba_ref[...])
    ig = jax.nn.sigmoid(jnp.dot(cb16, wx_ref[...], preferred_element_type=F32) + bx_ref[...])
    z = -lam_ref[...]
    softplus = jnp.maximum(z, 0.0) + jnp.log1p(jnp.exp(-jnp.abs(z)))
    log_a = (-RG_C) * r * softplus
    a = jnp.exp(log_a)
    bb = jnp.sqrt(1.0 - a * a) * (ig * conv)

    row = lax.broadcasted_iota(I32, (ts, 1), 0)
    d = 1
    while d < ts:
        live = row >= d
        a_s = jnp.where(live, pltpu.roll(a, d, axis=0), 1.0)
        b_s = jnp.where(live, pltpu.roll(bb, d, axis=0), 0.0)
        bb = bb + a * b_s
        a = a * a_s
        d *= 2
    h = bb + a * h_ref[0:1, :]
    h_ref[0:1, :] = h[ts - 1:ts, :]
    o_ref[0] = (h * _gelu_tanh(xg_ref[0])).astype(o_ref.dtype)


def _rglru_call(xr, xg, conv_w, conv_b, wa, ba, wx, bx, lam):
    b, s, r = xr.shape
    ts = min(TOKEN_TILE, s)
    tile = pl.BlockSpec((1, ts, r), lambda bi, ti: (bi, ti, 0))
    const = lambda shape: pl.BlockSpec(shape, lambda bi, ti: (0, 0))
    return pl.pallas_call(
        functools.partial(_rglru_kernel, ts=ts),
        grid=(b, s // ts),
        in_specs=[tile, tile, const((CONV_WIDTH, r)), const((1, r)), const((r, r)), const((1, r)),
                  const((r, r)), const((1, r)), const((1, r))],
        out_specs=tile,
        out_shape=jax.ShapeDtypeStruct((b, s, r), BF16),
        scratch_shapes=[pltpu.VMEM((ts + 8, r), F32), pltpu.VMEM((8, r), F32)],
        compiler_params=_cparams(("parallel", "arbitrary")),
        name="rglru",
    )(xr, xg, conv_w, conv_b, wa, ba, wx, bx, lam)


def _route(logits):
    lane = lax.broadcasted_iota(I32, logits.shape, 1)
    is_g = (lane >= N_EXPERTS) & (lane < N_EXPERTS + N_GROUPS)
    big = jnp.int32(LANES)

    def first_lane(mask):
        return jnp.min(jnp.where(mask, lane, big), axis=1, keepdims=True)

    gl = jnp.where(is_g, logits, NEG_INF)
    ge = jnp.exp(gl - jnp.max(gl, axis=1, keepdims=True))
    p_groups = ge / jnp.sum(ge, axis=1, keepdims=True)
    p_g = jnp.max(p_groups, axis=1, keepdims=True)
    g_idx = first_lane(is_g & (p_groups == p_g)) - N_EXPERTS

    in_grp = (lane < N_EXPERTS) & ((lane // EXP_PER_GROUP) == g_idx)
    el = jnp.where(in_grp, logits, NEG_INF)
    ee = jnp.exp(el - jnp.max(el, axis=1, keepdims=True))
    es = ee / jnp.sum(ee, axis=1, keepdims=True)
    es = jnp.where(in_grp, es, NEG_INF)
    v1 = jnp.max(es, axis=1, keepdims=True)
    i1 = first_lane(in_grp & (es == v1))
    es2 = jnp.where(lane == i1, NEG_INF, es)
    v2 = jnp.max(es2, axis=1, keepdims=True)
    i2 = first_lane(in_grp & (es2 == v2))
    tot = v1 + v2
    e_w = jnp.where(lane == i1, v1 / tot, 0.0) + jnp.where(lane == i2, v2 / tot, 0.0)
    return p_g * e_w + jnp.where(lane == ROUTE_GROUP_LANE + g_idx, 1.0, 0.0)


def _out_proj_kernel(ya_ref, yr_ref, x_ref, mod_ref, wo_ref, gpm_ref, gpf_ref, wr_ref, br_ref,
                     x1_ref, h2_ref, comb_ref):
    mix = (jnp.dot(ya_ref[0], wo_ref[0:D_ATTN, :], preferred_element_type=F32)
           + jnp.dot(yr_ref[0], wo_ref[D_ATTN:D_ATTN + D_RNN, :], preferred_element_type=F32))
    gate1 = mod_ref[0, 2:3, :]
    shift2 = mod_ref[0, 3:4, :]
    scale2 = mod_ref[0, 4:5, :]
    x1 = x_ref[0] + gate1 * _rms(mix, gpm_ref[...])
    x1_ref[0] = x1
    h2 = _rms(x1, gpf_ref[...]) * (1.0 + scale2) + shift2
    h2_ref[0] = h2.astype(BF16)
    logits = jnp.dot(h2, wr_ref[...], preferred_element_type=F32,
                     precision=lax.Precision.HIGHEST) + br_ref[...]
    comb_ref[0] = _route(logits)


def _out_proj_call(ya, yr, x, mod, wo, gpm, gpf, wr, br):
    b, s, d = x.shape
    tm = min(TOKEN_TILE, s)
    tile = lambda n: pl.BlockSpec((1, tm, n), lambda bi, ti: (bi, ti, 0))
    const = lambda shape: pl.BlockSpec(shape, lambda bi, ti: (0,) * len(shape))
    return pl.pallas_call(
        _out_proj_kernel,
        grid=(b, s // tm),
        in_specs=[tile(D_ATTN), tile(D_RNN), tile(d),
                  pl.BlockSpec((1, N_MOD, d), lambda bi, ti: (bi, 0, 0)),
                  const((D_ATTN + D_RNN, d)), const((1, d)), const((1, d)),
                  const((d, LANES)), const((1, LANES))],
        out_specs=[tile(d), tile(d), tile(LANES)],
        out_shape=[jax.ShapeDtypeStruct((b, s, d), F32), jax.ShapeDtypeStruct((b, s, d), BF16),
                   jax.ShapeDtypeStruct((b, s, LANES), F32)],
        compiler_params=_cparams(("parallel", "parallel")),
        name="out_proj",
    )(ya, yr, x, mod, wo, gpm, gpf, wr, br)


def _moe_kernel(h2_ref, comb_ref, x1_ref, mod_ref, wg_ref, wu_ref, wd_ref, gpo_ref,
                o_ref, hs_ref, cs_ref, ys_ref, act_ref, slot_ref, seg_ref, *, tm, ns, rb):
    g = pl.program_id(2)
    lane = lax.broadcasted_iota(I32, (1, LANES), 1)
    n_blk = tm // LANES

    @pl.when(g == 0)
    def _dispatch():
        comb = comb_ref[0]
        oh = jnp.where((lane >= ROUTE_GROUP_LANE) & (lane < ROUTE_GROUP_LANE + N_GROUPS), comb, 0.0)
        oh16 = oh.astype(BF16)
        sub = lax.broadcasted_iota(I32, (LANES, 1), 0)
        base = jnp.int32(0)
        base_lane = jnp.zeros((1, LANES), F32)
        base_sub = jnp.zeros((LANES, 1), F32)
        for gg in range(N_GROUPS):
            here = ROUTE_GROUP_LANE + gg
            n = jnp.sum(jnp.where(lane == here, oh, 0.0)).astype(I32)
            nb = (n + rb - 1) // rb
            seg_ref[gg] = base
            seg_ref[N_GROUPS + gg] = nb
            base_f = base.astype(F32)
            base_lane = base_lane + jnp.where(lane == here, base_f, 0.0)
            base_sub = base_sub + jnp.where(sub == here, base_f, 0.0)
            base = base + nb * rb

        tok_c = lax.broadcasted_iota(I32, (1, tm), 1)
        tok_r = lax.broadcasted_iota(I32, (tm, 1), 0)
        blk_r = lax.broadcasted_iota(I32, (LANES, 1), 0)
        blk_c = lax.broadcasted_iota(I32, (1, LANES), 1)
        eye = (blk_r == blk_c).astype(BF16)
        oht16 = _nt_dot(eye, oh16).astype(BF16)
        slot_row = []
        for t in range(n_blk):
            tril = (tok_c <= blk_r + t * LANES).astype(BF16)
            rank = jnp.dot(tril, oh16, preferred_element_type=F32)
            oh_blk = oh[t * LANES:(t + 1) * LANES, :]
            slot_ref[t * LANES:(t + 1) * LANES, :] = jnp.sum(
                oh_blk * (base_lane + rank - 1.0), axis=1, keepdims=True).astype(I32)
            triu = (tok_r <= blk_c + t * LANES).astype(BF16)
            rank_t = jnp.dot(oht16, triu, preferred_element_type=F32)
            oht_blk = oht16[:, t * LANES:(t + 1) * LANES].astype(F32)
            slot_row.append(jnp.sum(oht_blk * (base_sub + rank_t - 1.0), axis=0,
                                    keepdims=True).astype(I32))
        slot_row = jnp.concatenate(slot_row, axis=1)

        hi = comb.astype(BF16)
        lo = (comb - hi.astype(F32)).astype(BF16)
        cw = jnp.concatenate([hi, lo], axis=1)
        h = h2_ref[0]
        for r in range(ns // LANES):
            rows = slice(r * LANES, (r + 1) * LANES)
            p = (blk_r + r * LANES == slot_row).astype(BF16)
            hs_ref[rows, :] = jnp.dot(p, h, preferred_element_type=F32).astype(BF16)
            cs_ref[rows, :] = jnp.dot(p, cw, preferred_element_type=F32).astype(BF16)
        ys_ref[...] = jnp.zeros_like(ys_ref)

    n_act = EXP_PER_GROUP * D_EXPERT
    r_i = lax.broadcasted_iota(I32, (2 * LANES, n_act), 0) % LANES
    c_i = lax.broadcasted_iota(I32, (2 * LANES, n_act), 1) // D_EXPERT
    expand = (r_i == g * EXP_PER_GROUP + c_i).astype(BF16)
    base = seg_ref[g]

    def block(r, carry):
        rows = pl.ds(pl.multiple_of(base + r * rb, rb), rb)
        hb = hs_ref[rows, :]
        cexp = jnp.dot(cs_ref[rows, :], expand, preferred_element_type=F32)
        for e in range(EXP_PER_GROUP):
            gt = jnp.dot(hb, wg_ref[0, e], preferred_element_type=F32)
            up = jnp.dot(hb, wu_ref[0, e], preferred_element_type=F32)
            cols = slice(e * D_EXPERT, (e + 1) * D_EXPERT)
            act_ref[:, cols] = (gt * jax.nn.sigmoid(gt) * up * cexp[:, cols]).astype(BF16)
        ys_ref[rows, :] = jnp.dot(act_ref[...], wd_ref[0], preferred_element_type=F32).astype(BF16)
        return carry

    lax.fori_loop(0, seg_ref[N_GROUPS + g], block, 0)

    @pl.when(g == N_GROUPS - 1)
    def _combine():
        gate2 = mod_ref[0, 5:6, :]
        slot_c = lax.broadcasted_iota(I32, (1, ns), 1)
        half = tm // 2
        for hh in range(2):
            rows = slice(hh * half, (hh + 1) * half)
            pt = (slot_c == slot_ref[rows, :]).astype(BF16)
            y = jnp.dot(pt, ys_ref[...], preferred_element_type=F32)
            o_ref[0, rows, :] = x1_ref[0, rows, :] + gate2 * _rms(y, gpo_ref[...])


def _moe_call(h2, comb, x1, mod, wg, wu, wd, gpo):
    b, s, d = x1.shape
    tm = min(MOE_TILE, s)
    rb = MOE_ROWS
    ns = tm + N_GROUPS * rb
    tile = lambda n, **kw: pl.BlockSpec((1, tm, n), lambda bi, ti, gi: (bi, ti, 0), **kw)
    kern = functools.partial(_moe_kernel, tm=tm, ns=ns, rb=rb)
    return pl.pallas_call(
        kern,
        grid=(b, s // tm, N_GROUPS),
        in_specs=[tile(d), tile(LANES), tile(d, pipeline_mode=pl.Buffered(1)),
                  pl.BlockSpec((1, N_MOD, d), lambda bi, ti, gi: (bi, 0, 0)),
                  pl.BlockSpec((1, EXP_PER_GROUP, d, D_EXPERT), lambda bi, ti, gi: (gi, 0, 0, 0)),
                  pl.BlockSpec((1, EXP_PER_GROUP, d, D_EXPERT), lambda bi, ti, gi: (gi, 0, 0, 0)),
                  pl.BlockSpec((1, EXP_PER_GROUP * D_EXPERT, d), lambda bi, ti, gi: (gi, 0, 0)),
                  pl.BlockSpec((1, d), lambda bi, ti, gi: (0, 0))],
        out_specs=tile(d),
        out_shape=jax.ShapeDtypeStruct((b, s, d), F32),
        scratch_shapes=[pltpu.VMEM((ns, d), BF16),
                        pltpu.VMEM((ns, 2 * LANES), BF16),
                        pltpu.VMEM((ns, d), BF16),
                        pltpu.VMEM((rb, EXP_PER_GROUP * D_EXPERT), BF16),
                        pltpu.VMEM((tm, 1), I32),
                        pltpu.SMEM((2 * N_GROUPS,), I32)],
        compiler_params=_cparams(("parallel", "parallel", "arbitrary")),
        name="moe",
    )(h2, comb, x1, mod, wg, wu, wd, gpo)


def _rope_tables(seq_len):
    pos = jnp.arange(seq_len, dtype=F32)
    inv = ROPE_THETA ** (-jnp.arange(0, HEAD_DIM, 2, dtype=F32) / HEAD_DIM)
    ang = pos[:, None] * inv[None, :]
    cos, sin = jnp.cos(ang), jnp.sin(ang)
    cos2 = jnp.concatenate([cos, cos, cos, cos], axis=1)
    sin2 = jnp.concatenate([-sin, sin, -sin, sin], axis=1)
    return cos2, sin2, cos2[:, :HEAD_DIM].T, sin2[:, :HEAD_DIM].T


def _block_diag(w):
    n, c, d = w.shape
    eye = jnp.eye(n, dtype=w.dtype)
    return (eye[:, None, :, None] * w[:, :, None, :]).reshape(n * c, n * d)


def _pad_cols(w, n):
    return jnp.pad(w, ((0, 0), (0, n - w.shape[1])))


def kernel(x, c, w_ada, b_ada, g_pre_mix, g_post_mix, g_pre_ffn, g_post_ffn, w_in, conv_w, conv_b, w_rg_a, b_rg_a, w_rg_x, b_rg_x, lru_lambda, w_out, w_router_group, b_router_group, w_router_expert, b_router_expert, w_gate, w_up, w_down):
    b, s, d = x.shape
    depth = w_ada.shape[0]
    cos2, sin2, cos_t, sin_t = _rope_tables(s)
    c_pad = jnp.pad(c, ((0, (-b) % SUBLANES), (0, 0)))
    o_q, o_k, o_v, o_qi, o_ki, o_wi, o_xr, o_xg = 0, 512, 1024, 1536, 2048, 2112, 2120, 2632
    for l in range(depth):
        wl = w_in[l]
        w_ki = wl[:, o_ki:o_wi]
        w_nat = jnp.concatenate([wl[:, o_k:o_v], w_ki, w_ki, wl[:, o_xr:o_xg], wl[:, o_xg:]],
                                axis=1).astype(BF16)
        w_tr = jnp.concatenate([wl[:, o_q:o_k], wl[:, o_qi:o_ki], wl[:, o_v:o_qi],
                                _pad_cols(wl[:, o_wi:o_xr], _R_END - _R_WI)], axis=1).T.astype(BF16)
        w_route = _pad_cols(jnp.concatenate([w_router_expert[l], w_router_group[l]], axis=1), LANES)
        b_route = _pad_cols(jnp.concatenate([b_router_expert[l], b_router_group[l]])[None, :], LANES)

        mod = _mod_call(c_pad, w_ada[l], b_ada[l][None, :])[:b].reshape(b, N_MOD, d)
        qt, qit, wit, k, kk, vt, xr, xg = _in_proj_call(
            x, mod, g_pre_mix[l][None, :], w_nat, w_tr, cos2, sin2, cos_t, sin_t)
        y_attn = _attn_call(qt, qit, wit, kk, k, vt)
        y_rnn = _rglru_call(xr, xg, conv_w[l], conv_b[l][None, :],
                            _block_diag(w_rg_a[l]).astype(BF16), b_rg_a[l][None, :],
                            _block_diag(w_rg_x[l]).astype(BF16), b_rg_x[l][None, :],
                            lru_lambda[l][None, :])
        x1, h2, comb = _out_proj_call(y_attn, y_rnn, x, mod, w_out[l].astype(BF16),
                                      g_post_mix[l][None, :], g_pre_ffn[l][None, :], w_route, b_route)
        x = _moe_call(h2, comb, x1, mod, w_gate[l].astype(BF16), w_up[l].astype(BF16),
                      w_down[l].reshape(N_GROUPS, EXP_PER_GROUP * D_EXPERT, d).astype(BF16),
                      g_post_ffn[l][None, :])
    return x
```

```python
import functools
import math

import jax
import jax.numpy as jnp
from jax import lax
from jax.experimental import pallas as pl
from jax.experimental.pallas import tpu as pltpu

F32 = jnp.float32
BF16 = jnp.bfloat16
I32 = jnp.int32
I16 = jnp.int16

D_MODEL = 1024
CHUNK = 64
ROPE_THETA = 10000.0
EPS = 1e-6
N_HEADS = 8
HEAD_DIM = 64
D_ATTN = N_HEADS * HEAD_DIM
N_IDX_HEADS = 8
IDX_DIM = 64
TOPK_MAX = 256
D_RNN = 512
N_RNN_BLOCKS = 8
CONV_WIDTH = 4
RG_C = 8.0
N_GROUPS = 4
EXP_PER_GROUP = 8
N_EXPERTS = N_GROUPS * EXP_PER_GROUP
D_EXPERT = 256
N_MOD = 6

LANES = 128
SUBLANES = 8
INT_MIN = -2 ** 31
INT16_MIN = -2 ** 15
NEG_INF = float("-inf")

Q_BLOCK = 256
K_BLOCK = 512
K_SUB = 128
V_ROWS = HEAD_DIM + 16
TOKEN_TILE = 512
MOE_TILE = 1024
MOE_ROWS = 128
ROUTE_GROUP_LANE = 64
VMEM_LIMIT = 60 * 1024 * 1024


def _cparams(sem):
    return pltpu.CompilerParams(dimension_semantics=sem, vmem_limit_bytes=VMEM_LIMIT)


def _nt_dot(a, b):
    return lax.dot_general(a, b, (((1,), (1,)), ((), ())), preferred_element_type=F32)


def _rms(x, g):
    return x * lax.rsqrt(jnp.mean(x * x, axis=-1, keepdims=True) + EPS) * g


def _mod_kernel(c_ref, w_ref, b_ref, o_ref):
    c = c_ref[...]
    sc = c * jax.nn.sigmoid(c)
    o_ref[...] = jnp.dot(sc, w_ref[...], preferred_element_type=F32,
                         precision=lax.Precision.HIGHEST) + b_ref[...]


def _mod_call(c_pad, w_ada, b_ada):
    rows, d = c_pad.shape
    n = w_ada.shape[1]
    bn = 1024
    return pl.pallas_call(
        _mod_kernel,
        grid=(n // bn,),
        in_specs=[pl.BlockSpec((rows, d), lambda j: (0, 0)),
                  pl.BlockSpec((d, bn), lambda j: (0, j)),
                  pl.BlockSpec((1, bn), lambda j: (0, j))],
        out_specs=pl.BlockSpec((rows, bn), lambda j: (0, j)),
        out_shape=jax.ShapeDtypeStruct((rows, n), F32),
        compiler_params=_cparams(("arbitrary",)),
        name="mod",
    )(c_pad, w_ada, b_ada)


_R_Q, _R_QI, _R_V, _R_WI, _R_END = 0, 512, 1024, 1536, 1552
_C_K, _C_KK, _C_XR, _C_XG, _C_END = 0, 512, 640, 1152, 1664


def _rope(y, cos, sin):
    lane = lax.broadcasted_iota(I32, (1, LANES), 1)
    low = (lane % HEAD_DIM) < (HEAD_DIM // 2)
    outs = []
    for p in range(y.shape[1] // LANES):
        s = y[:, p * LANES:(p + 1) * LANES]
        swapped = jnp.where(low, pltpu.roll(s, LANES - HEAD_DIM // 2, axis=1),
                            pltpu.roll(s, HEAD_DIM // 2, axis=1))
        outs.append(s * cos + swapped * sin)
    return outs[0] if len(outs) == 1 else jnp.concatenate(outs, axis=1)


def _rope_t(y, cos, sin):
    half = HEAD_DIM // 2
    outs = []
    for hd in range(y.shape[0] // HEAD_DIM):
        blk = y[hd * HEAD_DIM:(hd + 1) * HEAD_DIM, :]
        swapped = jnp.concatenate([blk[half:, :], blk[:half, :]], axis=0)
        outs.append(blk * cos + swapped * sin)
    return jnp.concatenate(outs, axis=0)


def _in_proj_kernel(x_ref, mod_ref, g_ref, wn_ref, wt_ref, cos_ref, sin_ref, cost_ref, sint_ref,
                    qt_ref, qit_ref, wit_ref, k_ref, kk_ref, vt_ref, xr_ref, xg_ref):
    x = x_ref[0]
    shift = mod_ref[0, 0:1, :]
    scale = mod_ref[0, 1:2, :]
    h = (_rms(x, g_ref[...]) * (1.0 + scale) + shift).astype(BF16)
    cos, sin = cos_ref[...], sin_ref[...]
    cos_t, sin_t = cost_ref[...], sint_ref[...]

    def proj(a, b):
        return jnp.dot(h, wn_ref[:, a:b], preferred_element_type=F32)

    def proj_t(a, b):
        return _nt_dot(wt_ref[a:b, :], h)

    qt_ref[0] = (_rope_t(proj_t(_R_Q, _R_QI), cos_t, sin_t)
                 * (HEAD_DIM ** -0.5 * math.log2(math.e))).astype(BF16)
    qit_ref[0] = (_rope_t(proj_t(_R_QI, _R_V), cos_t, sin_t) * (IDX_DIM ** -0.5)).astype(BF16)
    vt = proj_t(_R_V, _R_WI).astype(BF16)
    ones = jnp.ones((V_ROWS - HEAD_DIM, vt.shape[1]), BF16)
    for hd in range(N_HEADS):
        vt_ref[0, 0, hd * V_ROWS:hd * V_ROWS + HEAD_DIM, :] = vt[hd * HEAD_DIM:(hd + 1) * HEAD_DIM, :]
        vt_ref[0, 0, hd * V_ROWS + HEAD_DIM:(hd + 1) * V_ROWS, :] = ones
    wit_ref[0] = proj_t(_R_WI, _R_END)[0:N_IDX_HEADS, :] * (N_IDX_HEADS ** -0.5)
    k_ref[0] = _rope(proj(_C_K, _C_KK), cos, sin).astype(BF16)
    kk_ref[0] = _rope(proj(_C_KK, _C_XR), cos, sin).astype(BF16)
    xr_ref[0] = proj(_C_XR, _C_XG)
    xg_ref[0] = proj(_C_XG, _C_END)


def _in_proj_call(x, mod, g, w_nat, w_tr, cos2, sin2, cos_t, sin_t):
    b, s, d = x.shape
    tm = min(K_BLOCK, s)
    nt = s // tm
    tile = lambda n: pl.BlockSpec((1, tm, n), lambda bi, ti: (bi, ti, 0))
    tile_t = lambda n: pl.BlockSpec((1, n, tm), lambda bi, ti: (bi, 0, ti))
    const = lambda shape: pl.BlockSpec(shape, lambda bi, ti: (0, 0))
    shp = lambda n, dt: jax.ShapeDtypeStruct((b, s, n), dt)
    shp_t = lambda n, dt: jax.ShapeDtypeStruct((b, n, s), dt)
    return pl.pallas_call(
        _in_proj_kernel,
        grid=(b, nt),
        in_specs=[tile(d),
                  pl.BlockSpec((1, N_MOD, d), lambda bi, ti: (bi, 0, 0)),
                  const((1, d)), const((d, _C_END)), const((_R_END, d)),
                  pl.BlockSpec((tm, LANES), lambda bi, ti: (ti, 0)),
                  pl.BlockSpec((tm, LANES), lambda bi, ti: (ti, 0)),
                  pl.BlockSpec((HEAD_DIM, tm), lambda bi, ti: (0, ti)),
                  pl.BlockSpec((HEAD_DIM, tm), lambda bi, ti: (0, ti))],
        out_specs=[tile_t(D_ATTN), tile_t(D_ATTN), tile_t(N_IDX_HEADS), tile(D_ATTN), tile(LANES),
                   pl.BlockSpec((1, 1, N_HEADS * V_ROWS, tm), lambda bi, ti: (bi, ti, 0, 0)),
                   tile(D_RNN), tile(D_RNN)],
        out_shape=[shp_t(D_ATTN, BF16), shp_t(D_ATTN, BF16), shp_t(N_IDX_HEADS, F32),
                   shp(D_ATTN, BF16), shp(LANES, BF16),
                   jax.ShapeDtypeStruct((b, nt, N_HEADS * V_ROWS, tm), BF16),
                   shp(D_RNN, F32), shp(D_RNN, F32)],
        compiler_params=_cparams(("parallel", "parallel")),
        name="in_proj",
    )(x, mod, g, w_nat, w_tr, cos2, sin2, cos_t, sin_t)


def _sortable(score):
    bits = pltpu.bitcast(score, I32)
    return jnp.where(bits < 0, jnp.int32(INT_MIN) - bits, bits)


def _tile_fold(x, op, rows):
    parts = [x[t * rows:(t + 1) * rows, :] for t in range(x.shape[0] // rows)]
    while len(parts) > 1:
        nxt = [op(parts[t], parts[t + 1]) for t in range(0, len(parts) - 1, 2)]
        if len(parts) % 2:
            nxt.append(parts[-1])
        parts = nxt
    return parts[0]


def _row_fold(x, op):
    acc = x[0:SUBLANES, :]
    for t in range(1, x.shape[0] // SUBLANES):
        acc = op(acc, x[t * SUBLANES:(t + 1) * SUBLANES, :])
    return acc


def _fold16(x):
    return _tile_fold(x, jnp.add, 2 * SUBLANES)


def _attn_kernel(qt_ref, qit_ref, wit_ref, kk_ref, k_ref, vt_ref, o_ref,
                 keys_ref, hi_ref, lo_ref, bias_ref, s_ref, p_ref, qm_ref, qim_ref, m_ref, l_ref,
                 acc_ref, *, n_sel, qb, kb, sub):
    i = pl.program_id(1)
    n_ch = ((i + 1) * qb + kb - 1) // kb
    q_pos = i * qb + lax.broadcasted_iota(I32, (1, qb), 1)
    limit = (q_pos // CHUNK + 1) * CHUNK
    krow = lax.broadcasted_iota(I32, (kb, 1), 0)
    row128 = lax.broadcasted_iota(I32, (LANES, 1), 0)

    for h in range(N_HEADS):
        rows = slice((h // 2) * LANES, (h // 2 + 1) * LANES)
        keep = (row128 < HEAD_DIM) if h % 2 == 0 else (row128 >= HEAD_DIM)
        qs, qis = qt_ref[0, rows, :], qit_ref[0, rows, :]
        qm_ref[h] = jnp.where(keep, qs, jnp.zeros_like(qs))
        qim_ref[h] = jnp.where(keep, qis, jnp.zeros_like(qis))
    m_ref[...] = jnp.full(m_ref.shape, NEG_INF, F32)
    l_ref[...] = jnp.zeros(l_ref.shape, F32)
    acc_ref[...] = jnp.zeros(acc_ref.shape, F32)

    wi = wit_ref[0]

    def score_body(j, carry, masked):
        kk = kk_ref[0, pl.ds(pl.multiple_of(j * kb, kb), kb), :]
        acc = jnp.zeros((kb, qb), F32)
        for h in range(N_IDX_HEADS):
            d = jnp.dot(kk, qim_ref[h], preferred_element_type=F32)
            acc = acc + wi[h:h + 1, :] * jnp.maximum(d, 0.0)
        key = _sortable(acc)
        if masked:
            key = jnp.where(j * kb + krow < limit, key, jnp.int32(INT_MIN))
        keys_ref[j] = key
        hi_ref[j] = (key >> 16).astype(I16)
        return carry

    n_open = (i * qb + CHUNK) // kb
    lax.fori_loop(0, n_open, functools.partial(score_body, masked=False), 0)
    lax.fori_loop(n_open, n_ch, functools.partial(score_body, masked=True), 0)

    def count16(ref, pred):
        def one(j):
            return _fold16(jnp.where(pred(ref[j]), jnp.int16(1), jnp.int16(0)))

        def body(jj, acc):
            return acc + (one(2 * jj) + one(2 * jj + 1))

        acc = lax.fori_loop(0, n_ch // 2, body, jnp.zeros((2 * SUBLANES, qb), I16))
        acc = lax.cond(n_ch % 2 == 1, lambda a: a + one(n_ch - 1), lambda a: a, acc)
        return jnp.sum(acc.astype(I32), axis=0, keepdims=True)

    def search16(ref, target):
        c0 = count16(ref, lambda v: v >= jnp.int16(0))
        ok = c0 >= target
        init = (jnp.where(ok, jnp.int32(0), jnp.int32(INT16_MIN)),
                jnp.where(ok, c0, n_ch * kb), jnp.where(ok, 0, c0))

        def bit_body(bi, carry):
            t, n_ge, n_gt = carry
            cand = t | (jnp.int32(1) << (14 - bi))
            c = count16(ref, lambda v: v >= cand.astype(I16))
            ok = c >= target
            return jnp.where(ok, cand, t), jnp.where(ok, c, n_ge), jnp.where(ok, n_gt, c)

        return lax.fori_loop(0, 15, bit_body, init)

    tau_hi, _, n_gt_hi = search16(hi_ref, n_sel)
    tau_hi16 = tau_hi.astype(I16)

    def lo_body(j, carry):
        lo = (keys_ref[j] ^ jnp.int32(0x8000)).astype(I16)
        lo_ref[j] = jnp.where(hi_ref[j] == tau_hi16, lo, jnp.int16(INT16_MIN))
        return carry

    lax.fori_loop(0, n_ch, lo_body, 0)
    tau_lo, n_ge_lo, n_gt_lo = search16(lo_ref, n_sel - n_gt_hi)
    tau = (tau_hi << 16) | ((tau_lo + 32768) & jnp.int32(0xFFFF))
    all_sel = tau == jnp.int32(INT_MIN)
    need_i = jnp.where(all_sel, 0, n_sel - n_gt_hi - n_gt_lo)
    need = need_i.astype(F32)
    ties = jnp.max(jnp.where(all_sel, 0, (n_ge_lo - n_gt_lo) - need_i)) > 0
    thr = jnp.where(all_sel, jnp.int32(INT_MIN + 1), tau)

    n_sub = kb // sub
    no_max = jnp.full((SUBLANES, qb), NEG_INF, F32)

    def bias_ties(j, slot, run):
        r_i = lax.broadcasted_iota(I32, (kb, kb), 0)
        c_i = lax.broadcasted_iota(I32, (kb, kb), 1)
        tri = (c_i <= r_i).astype(BF16)
        kc = keys_ref[j]
        eq = kc == tau
        rank = jnp.dot(tri, eq.astype(BF16), preferred_element_type=F32) + run
        sel = (kc > tau) | (eq & (rank <= need))
        bias_ref[slot] = jnp.where(sel, 0.0, NEG_INF).astype(F32)
        return rank[kb - 1:kb, :]

    def bias_plain(j, slot, run):
        bias_ref[slot] = jnp.where(keys_ref[j] >= thr, 0.0, NEG_INF).astype(F32)
        return run

    def logits(j, h, r, mx):
        cols = slice((h // 2) * LANES, (h // 2 + 1) * LANES)
        rows = pl.ds(pl.multiple_of(j * kb + r * sub, sub), sub)
        x = (jnp.dot(k_ref[0, rows, cols], qm_ref[h], preferred_element_type=F32)
             + bias_ref[j % 2, r * sub:(r + 1) * sub, :])
        s_ref[h % 2, r * sub:(r + 1) * sub, :] = x
        return jnp.maximum(mx, _row_fold(x, jnp.maximum))

    run0 = lax.cond(ties, bias_ties, bias_plain, 0, 0, jnp.zeros((1, qb), F32))
    mx0 = no_max
    for r in range(n_sub):
        mx0 = logits(0, 0, r, mx0)

    def attn_body(j, carry):
        run, mx = carry
        jn = jnp.minimum(j + 1, n_ch - 1)
        run = lax.cond(ties, bias_ties, bias_plain, jn, (j + 1) % 2, run)
        for h in range(N_HEADS):
            m_old = m_ref[h]
            m_new = jnp.maximum(m_old, jnp.max(mx, axis=0, keepdims=True))
            m_safe = jnp.where(m_new == NEG_INF, 0.0, m_new)
            alpha = jnp.exp2(m_old - m_safe)
            mx = no_max
            for r in range(n_sub):
                mx = logits(j, h + 1, r, mx) if h + 1 < N_HEADS else logits_next(jn, j, r, mx)
                tile = slice(r * sub, (r + 1) * sub)
                p_ref[tile, :] = jnp.exp2(s_ref[h % 2, tile, :] - m_safe).astype(BF16)
            pv = jnp.dot(vt_ref[0, j, h * V_ROWS:(h + 1) * V_ROWS, :], p_ref[...],
                         preferred_element_type=F32)
            l_ref[h] = alpha * l_ref[h] + pv[HEAD_DIM:HEAD_DIM + 1, :]
            acc_ref[h] = alpha * acc_ref[h] + pv[0:HEAD_DIM, :]
            m_ref[h] = m_new
        return run, mx

    def logits_next(jn, j, r, mx):
        rows = pl.ds(pl.multiple_of(jn * kb + r * sub, sub), sub)
        x = (jnp.dot(k_ref[0, rows, 0:LANES], qm_ref[0], preferred_element_type=F32)
             + bias_ref[(j + 1) % 2, r * sub:(r + 1) * sub, :])
        s_ref[0, r * sub:(r + 1) * sub, :] = x
        return jnp.maximum(mx, _row_fold(x, jnp.maximum))

    lax.fori_loop(0, n_ch, attn_body, (run0, mx0))

    outs = [acc_ref[h] / l_ref[h] for h in range(N_HEADS)]
    o_ref[0] = jnp.concatenate(outs, axis=0).T.astype(o_ref.dtype)


def _attn_call(qt, qit, wit, kk, k, vt):
    b, s, _ = k.shape
    qb = min(Q_BLOCK, s)
    kb = min(K_BLOCK, s)
    n_sel = min(TOPK_MAX, s // 4)
    blk_t = lambda n: pl.BlockSpec((1, n, qb), lambda bi, qi_: (bi, 0, qi_))
    once = pl.Buffered(1)
    kern = functools.partial(_attn_kernel, n_sel=n_sel, qb=qb, kb=kb, sub=min(K_SUB, kb))
    return pl.pallas_call(
        kern,
        grid=(b, s // qb),
        in_specs=[blk_t(D_ATTN), blk_t(D_ATTN), blk_t(N_IDX_HEADS),
                  pl.BlockSpec((1, s, LANES), lambda bi, qi_: (bi, 0, 0), pipeline_mode=once),
                  pl.BlockSpec((1, s, D_ATTN), lambda bi, qi_: (bi, 0, 0), pipeline_mode=once),
                  pl.BlockSpec((1, s // kb, N_HEADS * V_ROWS, kb), lambda bi, qi_: (bi, 0, 0, 0),
                               pipeline_mode=once)],
        out_specs=pl.BlockSpec((1, qb, D_ATTN), lambda bi, qi_: (bi, qi_, 0)),
        out_shape=jax.ShapeDtypeStruct((b, s, D_ATTN), BF16),
        scratch_shapes=[pltpu.VMEM((s // kb, kb, qb), I32),
                        pltpu.VMEM((s // kb, kb, qb), I16),
                        pltpu.VMEM((s // kb, kb, qb), I16),
                        pltpu.VMEM((2, kb, qb), F32),
                        pltpu.VMEM((2, kb, qb), F32),
                        pltpu.VMEM((kb, qb), BF16),
                        pltpu.VMEM((N_HEADS, LANES, qb), BF16),
                        pltpu.VMEM((N_IDX_HEADS, LANES, qb), BF16),
                        pltpu.VMEM((N_HEADS, 1, qb), F32),
                        pltpu.VMEM((N_HEADS, 1, qb), F32),
                        pltpu.VMEM((N_HEADS, HEAD_DIM, qb), F32)],
        compiler_params=_cparams(("parallel", "arbitrary")),
        name="attn",
    )(qt, qit, wit, kk, k, vt)


def _gelu_tanh(x):
    return 0.5 * x * (1.0 + jnp.tanh(math.sqrt(2.0 / math.pi) * (x + 0.044715 * (x * x * x))))


def _rglru_kernel(xr_ref, xg_ref, cw_ref, cb_ref, wa_ref, ba_ref, wx_ref, bx_ref, lam_ref,
                  o_ref, xbuf_ref, h_ref, *, ts):
    t = pl.program_id(1)
    pad = 8

    @pl.when(t == 0)
    def _():
        xbuf_ref[0:pad, :] = jnp.zeros((pad, D_RNN), F32)
        h_ref[...] = jnp.zeros_like(h_ref)

    xr = xr_ref[0]
    xbuf_ref[pad:pad + ts, :] = xr
    conv = cb_ref[...] + cw_ref[CONV_WIDTH - 1:CONV_WIDTH, :] * xr
    for w in range(CONV_WIDTH - 1):
        off = pad - (CONV_WIDTH - 1) + w
        conv = conv + cw_ref[w:w + 1, :] * xbuf_ref[off:off + ts, :]
    xbuf_ref[0:pad, :] = xr[ts - pad:ts, :]

    cb16 = conv.astype(BF16)
    r = jax.nn.sigmoid(jnp.dot(cb16, wa_ref[...], preferred_element_type=F32) + ba_ref[...])
    ig = jax.nn.sigmoid(jnp.dot(cb16, wx_ref[...], preferred_element_type=F32) + bx_ref[...])
    z = -lam_ref[...]
    softplus = jnp.maximum(z, 0.0) + jnp.log1p(jnp.exp(-jnp.abs(z)))
    log_a = (-RG_C) * r * softplus
    a = jnp.exp(log_a)
    bb = jnp.sqrt(1.0 - a * a) * (ig * conv)

    row = lax.broadcasted_iota(I32, (ts, 1), 0)
    d = 1
    while d < ts:
        if d < SUBLANES:
            live = row >= d
            a_s = jnp.where(live, pltpu.roll(a, d, axis=0), 1.0)
            b_s = jnp.where(live, pltpu.roll(bb, d, axis=0), 0.0)
            bb = bb + a * b_s
            a = a * a_s
        else:
            lo_a, hi_a = a[:d, :], a[d:, :]
            bb = jnp.concatenate([bb[:d, :], bb[d:, :] + hi_a * bb[:ts - d, :]], axis=0)
            a = jnp.concatenate([lo_a, hi_a * a[:ts - d, :]], axis=0)
        d *= 2
    h = bb + a * h_ref[0:1, :]
    h_ref[0:1, :] = h[ts - 1:ts, :]
    o_ref[0] = (h * _gelu_tanh(xg_ref[0])).astype(o_ref.dtype)


def _rglru_call(xr, xg, conv_w, conv_b, wa, ba, wx, bx, lam):
    b, s, r = xr.shape
    ts = min(TOKEN_TILE, s)
    tile = pl.BlockSpec((1, ts, r), lambda bi, ti: (bi, ti, 0))
    const = lambda shape: pl.BlockSpec(shape, lambda bi, ti: (0, 0))
    return pl.pallas_call(
        functools.partial(_rglru_kernel, ts=ts),
        grid=(b, s // ts),
        in_specs=[tile, tile, const((CONV_WIDTH, r)), const((1, r)), const((r, r)), const((1, r)),
                  const((r, r)), const((1, r)), const((1, r))],
        out_specs=tile,
        out_shape=jax.ShapeDtypeStruct((b, s, r), BF16),
        scratch_shapes=[pltpu.VMEM((ts + 8, r), F32), pltpu.VMEM((8, r), F32)],
        compiler_params=_cparams(("parallel", "arbitrary")),
        name="rglru",
    )(xr, xg, conv_w, conv_b, wa, ba, wx, bx, lam)


def _route(logits):
    lane = lax.broadcasted_iota(I32, logits.shape, 1)
    is_g = (lane >= N_EXPERTS) & (lane < N_EXPERTS + N_GROUPS)
    big = jnp.int32(LANES)

    def first_lane(mask):
        return jnp.min(jnp.where(mask, lane, big), axis=1, keepdims=True)

    gl = jnp.where(is_g, logits, NEG_INF)
    ge = jnp.exp(gl - jnp.max(gl, axis=1, keepdims=True))
    p_groups = ge / jnp.sum(ge, axis=1, keepdims=True)
    p_g = jnp.max(p_groups, axis=1, keepdims=True)
    g_idx = first_lane(is_g & (p_groups == p_g)) - N_EXPERTS

    in_grp = (lane < N_EXPERTS) & ((lane // EXP_PER_GROUP) == g_idx)
    el = jnp.where(in_grp, logits, NEG_INF)
    ee = jnp.exp(el - jnp.max(el, axis=1, keepdims=True))
    es = ee / jnp.sum(ee, axis=1, keepdims=True)
    es = jnp.where(in_grp, es, NEG_INF)
    v1 = jnp.max(es, axis=1, keepdims=True)
    i1 = first_lane(in_grp & (es == v1))
    es2 = jnp.where(lane == i1, NEG_INF, es)
    v2 = jnp.max(es2, axis=1, keepdims=True)
    i2 = first_lane(in_grp & (es2 == v2))
    tot = v1 + v2
    e_w = jnp.where(lane == i1, v1 / tot, 0.0) + jnp.where(lane == i2, v2 / tot, 0.0)
    return p_g * e_w + jnp.where(lane == ROUTE_GROUP_LANE + g_idx, 1.0, 0.0)


def _out_proj_kernel(ya_ref, yr_ref, x_ref, mod_ref, wo_ref, gpm_ref, gpf_ref, wr_ref, br_ref,
                     x1_ref, h2_ref, comb_ref):
    mix = (jnp.dot(ya_ref[0], wo_ref[0:D_ATTN, :], preferred_element_type=F32)
           + jnp.dot(yr_ref[0], wo_ref[D_ATTN:D_ATTN + D_RNN, :], preferred_element_type=F32))
    gate1 = mod_ref[0, 2:3, :]
    shift2 = mod_ref[0, 3:4, :]
    scale2 = mod_ref[0, 4:5, :]
    x1 = x_ref[0] + gate1 * _rms(mix, gpm_ref[...])
    x1_ref[0] = x1
    h2 = _rms(x1, gpf_ref[...]) * (1.0 + scale2) + shift2
    h2_ref[0] = h2.astype(BF16)
    logits = jnp.dot(h2, wr_ref[...], preferred_element_type=F32,
                     precision=lax.Precision.HIGHEST) + br_ref[...]
    comb_ref[0] = _route(logits)


def _out_proj_call(ya, yr, x, mod, wo, gpm, gpf, wr, br):
    b, s, d = x.shape
    tm = min(TOKEN_TILE, s)
    tile = lambda n: pl.BlockSpec((1, tm, n), lambda bi, ti: (bi, ti, 0))
    const = lambda shape: pl.BlockSpec(shape, lambda bi, ti: (0,) * len(shape))
    return pl.pallas_call(
        _out_proj_kernel,
        grid=(b, s // tm),
        in_specs=[tile(D_ATTN), tile(D_RNN), tile(d),
                  pl.BlockSpec((1, N_MOD, d), lambda bi, ti: (bi, 0, 0)),
                  const((D_ATTN + D_RNN, d)), const((1, d)), const((1, d)),
                  const((d, LANES)), const((1, LANES))],
        out_specs=[tile(d), tile(d), tile(LANES)],
        out_shape=[jax.ShapeDtypeStruct((b, s, d), F32), jax.ShapeDtypeStruct((b, s, d), BF16),
                   jax.ShapeDtypeStruct((b, s, LANES), F32)],
        compiler_params=_cparams(("parallel", "parallel")),
        name="out_proj",
    )(ya, yr, x, mod, wo, gpm, gpf, wr, br)


def _moe_kernel(h2_ref, comb_ref, x1_ref, mod_ref, wg_ref, wu_ref, wd_ref, gpo_ref,
                o_ref, hs_ref, cs_ref, ys_ref, act_ref, slot_ref, seg_ref, *, tm, ns, rb):
    g = pl.program_id(2)
    lane = lax.broadcasted_iota(I32, (1, LANES), 1)
    n_blk = tm // LANES

    @pl.when(g == 0)
    def _dispatch():
        comb = comb_ref[0]
        oh = jnp.where((lane >= ROUTE_GROUP_LANE) & (lane < ROUTE_GROUP_LANE + N_GROUPS), comb, 0.0)
        oh16 = oh.astype(BF16)
        sub = lax.broadcasted_iota(I32, (LANES, 1), 0)
        base = jnp.int32(0)
        base_lane = jnp.zeros((1, LANES), F32)
        base_sub = jnp.zeros((LANES, 1), F32)
        for gg in range(N_GROUPS):
            here = ROUTE_GROUP_LANE + gg
            n = jnp.sum(jnp.where(lane == here, oh, 0.0)).astype(I32)
            nb = (n + rb - 1) // rb
            seg_ref[gg] = base
            seg_ref[N_GROUPS + gg] = nb
            base_f = base.astype(F32)
            base_lane = base_lane + jnp.where(lane == here, base_f, 0.0)
            base_sub = base_sub + jnp.where(sub == here, base_f, 0.0)
            base = base + nb * rb

        tok_c = lax.broadcasted_iota(I32, (1, tm), 1)
        tok_r = lax.broadcasted_iota(I32, (tm, 1), 0)
        blk_r = lax.broadcasted_iota(I32, (LANES, 1), 0)
        blk_c = lax.broadcasted_iota(I32, (1, LANES), 1)
        eye = (blk_r == blk_c).astype(BF16)
        oht16 = _nt_dot(eye, oh16).astype(BF16)
        slot_row = []
        for t in range(n_blk):
            tril = (tok_c <= blk_r + t * LANES).astype(BF16)
            rank = jnp.dot(tril, oh16, preferred_element_type=F32)
            oh_blk = oh[t * LANES:(t + 1) * LANES, :]
            slot_ref[t * LANES:(t + 1) * LANES, :] = jnp.sum(
                oh_blk * (base_lane + rank - 1.0), axis=1, keepdims=True).astype(I32)
            triu = (tok_r <= blk_c + t * LANES).astype(BF16)
            rank_t = jnp.dot(oht16, triu, preferred_element_type=F32)
            oht_blk = oht16[:, t * LANES:(t + 1) * LANES].astype(F32)
            slot_row.append(jnp.sum(oht_blk * (base_sub + rank_t - 1.0), axis=0,
                                    keepdims=True).astype(I32))
        slot_row = jnp.concatenate(slot_row, axis=1)

        hi = comb.astype(BF16)
        lo = (comb - hi.astype(F32)).astype(BF16)
        cw = jnp.concatenate([hi, lo], axis=1)
        h = h2_ref[0]
        for r in range(ns // LANES):
            rows = slice(r * LANES, (r + 1) * LANES)
            p = (blk_r + r * LANES == slot_row).astype(BF16)
            hs_ref[rows, :] = jnp.dot(p, h, preferred_element_type=F32).astype(BF16)
            cs_ref[rows, :] = jnp.dot(p, cw, preferred_element_type=F32).astype(BF16)
        ys_ref[...] = jnp.zeros_like(ys_ref)

    n_act = EXP_PER_GROUP * D_EXPERT
    r_i = lax.broadcasted_iota(I32, (2 * LANES, n_act), 0) % LANES
    c_i = lax.broadcasted_iota(I32, (2 * LANES, n_act), 1) // D_EXPERT
    expand = (r_i == g * EXP_PER_GROUP + c_i).astype(BF16)
    base = seg_ref[g]

    def block(r, carry):
        rows = pl.ds(pl.multiple_of(base + r * rb, rb), rb)
        hb = hs_ref[rows, :]
        cexp = jnp.dot(cs_ref[rows, :], expand, preferred_element_type=F32)
        for e in range(EXP_PER_GROUP):
            gt = jnp.dot(hb, wg_ref[0, e], preferred_element_type=F32)
            up = jnp.dot(hb, wu_ref[0, e], preferred_element_type=F32)
            cols = slice(e * D_EXPERT, (e + 1) * D_EXPERT)
            act_ref[:, cols] = (gt * jax.nn.sigmoid(gt) * up * cexp[:, cols]).astype(BF16)
        ys_ref[rows, :] = jnp.dot(act_ref[...], wd_ref[0], preferred_element_type=F32).astype(BF16)
        return carry

    lax.fori_loop(0, seg_ref[N_GROUPS + g], block, 0)

    @pl.when(g == N_GROUPS - 1)
    def _combine():
        gate2 = mod_ref[0, 5:6, :]
        slot_c = lax.broadcasted_iota(I32, (1, ns), 1)
        half = tm // 2
        for hh in range(2):
            rows = slice(hh * half, (hh + 1) * half)
            pt = (slot_c == slot_ref[rows, :]).astype(BF16)
            y = jnp.dot(pt, ys_ref[...], preferred_element_type=F32)
            o_ref[0, rows, :] = x1_ref[0, rows, :] + gate2 * _rms(y, gpo_ref[...])


def _moe_call(h2, comb, x1, mod, wg, wu, wd, gpo):
    b, s, d = x1.shape
    tm = min(MOE_TILE, s)
    rb = MOE_ROWS
    ns = tm + N_GROUPS * rb
    tile = lambda n, **kw: pl.BlockSpec((1, tm, n), lambda bi, ti, gi: (bi, ti, 0), **kw)
    kern = functools.partial(_moe_kernel, tm=tm, ns=ns, rb=rb)
    return pl.pallas_call(
        kern,
        grid=(b, s // tm, N_GROUPS),
        in_specs=[tile(d), tile(LANES), tile(d, pipeline_mode=pl.Buffered(1)),
                  pl.BlockSpec((1, N_MOD, d), lambda bi, ti, gi: (bi, 0, 0)),
                  pl.BlockSpec((1, EXP_PER_GROUP, d, D_EXPERT), lambda bi, ti, gi: (gi, 0, 0, 0)),
                  pl.BlockSpec((1, EXP_PER_GROUP, d, D_EXPERT), lambda bi, ti, gi: (gi, 0, 0, 0)),
                  pl.BlockSpec((1, EXP_PER_GROUP * D_EXPERT, d), lambda bi, ti, gi: (gi, 0, 0)),
                  pl.BlockSpec((1, d), lambda bi, ti, gi: (0, 0))],
        out_specs=tile(d),
        out_shape=jax.ShapeDtypeStruct((b, s, d), F32),
        scratch_shapes=[pltpu.VMEM((ns, d), BF16),
                        pltpu.VMEM((ns, 2 * LANES), BF16),
                        pltpu.VMEM((ns, d), BF16),
                        pltpu.VMEM((rb, EXP_PER_GROUP * D_EXPERT), BF16),
                        pltpu.VMEM((tm, 1), I32),
                        pltpu.SMEM((2 * N_GROUPS,), I32)],
        compiler_params=_cparams(("parallel", "parallel", "arbitrary")),
        name="moe",
    )(h2, comb, x1, mod, wg, wu, wd, gpo)


def _rope_tables(seq_len):
    pos = jnp.arange(seq_len, dtype=F32)
    inv = ROPE_THETA ** (-jnp.arange(0, HEAD_DIM, 2, dtype=F32) / HEAD_DIM)
    ang = pos[:, None] * inv[None, :]
    cos, sin = jnp.cos(ang), jnp.sin(ang)
    cos2 = jnp.concatenate([cos, cos, cos, cos], axis=1)
    sin2 = jnp.concatenate([-sin, sin, -sin, sin], axis=1)
    return cos2, sin2, cos2[:, :HEAD_DIM].T, sin2[:, :HEAD_DIM].T


def _block_diag(w):
    n, c, d = w.shape
    eye = jnp.eye(n, dtype=w.dtype)
    return (eye[:, None, :, None] * w[:, :, None, :]).reshape(n * c, n * d)


def _pad_cols(w, n):
    return jnp.pad(w, ((0, 0), (0, n - w.shape[1])))


def kernel(x, c, w_ada, b_ada, g_pre_mix, g_post_mix, g_pre_ffn, g_post_ffn, w_in, conv_w, conv_b, w_rg_a, b_rg_a, w_rg_x, b_rg_x, lru_lambda, w_out, w_router_group, b_router_group, w_router_expert, b_router_expert, w_gate, w_up, w_down):
    b, s, d = x.shape
    depth = w_ada.shape[0]
    cos2, sin2, cos_t, sin_t = _rope_tables(s)
    c_pad = jnp.pad(c, ((0, (-b) % SUBLANES), (0, 0)))
    o_q, o_k, o_v, o_qi, o_ki, o_wi, o_xr, o_xg = 0, 512, 1024, 1536, 2048, 2112, 2120, 2632
    for l in range(depth):
        wl = w_in[l]
        w_ki = wl[:, o_ki:o_wi]
        w_nat = jnp.concatenate([wl[:, o_k:o_v], w_ki, w_ki, wl[:, o_xr:o_xg], wl[:, o_xg:]],
                                axis=1).astype(BF16)
        w_tr = jnp.concatenate([wl[:, o_q:o_k], wl[:, o_qi:o_ki], wl[:, o_v:o_qi],
                                _pad_cols(wl[:, o_wi:o_xr], _R_END - _R_WI)], axis=1).T.astype(BF16)
        w_route = _pad_cols(jnp.concatenate([w_router_expert[l], w_router_group[l]], axis=1), LANES)
        b_route = _pad_cols(jnp.concatenate([b_router_expert[l], b_router_group[l]])[None, :], LANES)

        mod = _mod_call(c_pad, w_ada[l], b_ada[l][None, :])[:b].reshape(b, N_MOD, d)
        qt, qit, wit, k, kk, vt, xr, xg = _in_proj_call(
            x, mod, g_pre_mix[l][None, :], w_nat, w_tr, cos2, sin2, cos_t, sin_t)
        y_attn = _attn_call(qt, qit, wit, kk, k, vt)
        y_rnn = _rglru_call(xr, xg, conv_w[l], conv_b[l][None, :],
                            _block_diag(w_rg_a[l]).astype(BF16), b_rg_a[l][None, :],
                            _block_diag(w_rg_x[l]).astype(BF16), b_rg_x[l][None, :],
                            lru_lambda[l][None, :])
        x1, h2, comb = _out_proj_call(y_attn, y_rnn, x, mod, w_out[l].astype(BF16),
                                      g_post_mix[l][None, :], g_pre_ffn[l][None, :], w_route, b_route)
        x = _moe_call(h2, comb, x1, mod, w_gate[l].astype(BF16), w_up[l].astype(BF16),
                      w_down[l].reshape(N_GROUPS, EXP_PER_GROUP * D_EXPERT, d).astype(BF16),
                      g_post_ffn[l][None, :])
    return x
```

```python
import functools
import math

import jax
import jax.numpy as jnp
from jax import lax
from jax.experimental import pallas as pl
from jax.experimental.pallas import tpu as pltpu

F32 = jnp.float32
BF16 = jnp.bfloat16
I32 = jnp.int32
I16 = jnp.int16

D_MODEL = 1024
CHUNK = 64
ROPE_THETA = 10000.0
EPS = 1e-6
N_HEADS = 8
HEAD_DIM = 64
D_ATTN = N_HEADS * HEAD_DIM
N_IDX_HEADS = 8
IDX_DIM = 64
TOPK_MAX = 256
D_RNN = 512
N_RNN_BLOCKS = 8
CONV_WIDTH = 4
RG_C = 8.0
N_GROUPS = 4
EXP_PER_GROUP = 8
N_EXPERTS = N_GROUPS * EXP_PER_GROUP
D_EXPERT = 256
N_MOD = 6

LANES = 128
SUBLANES = 8
INT_MIN = -2 ** 31
INT16_MIN = -2 ** 15
NEG_INF = float("-inf")

Q_BLOCK = 256
K_BLOCK = 512
K_SUB = 256
HEADS_PER_STAGE = 2
V_ROWS = HEAD_DIM + 16
TOKEN_TILE = 512
MOE_TILE = 1024
MOE_ROWS = 128
ROUTE_GROUP_LANE = 64
VMEM_LIMIT = 60 * 1024 * 1024


def _cparams(sem):
    return pltpu.CompilerParams(dimension_semantics=sem, vmem_limit_bytes=VMEM_LIMIT)


def _nt_dot(a, b):
    return lax.dot_general(a, b, (((1,), (1,)), ((), ())), preferred_element_type=F32)


def _rms(x, g):
    return x * lax.rsqrt(jnp.mean(x * x, axis=-1, keepdims=True) + EPS) * g


def _mod_kernel(c_ref, w_ref, b_ref, o_ref):
    c = c_ref[...]
    sc = c * jax.nn.sigmoid(c)
    o_ref[...] = jnp.dot(sc, w_ref[...], preferred_element_type=F32,
                         precision=lax.Precision.HIGHEST) + b_ref[...]


def _mod_call(c_pad, w_ada, b_ada):
    rows, d = c_pad.shape
    n = w_ada.shape[1]
    bn = 1024
    return pl.pallas_call(
        _mod_kernel,
        grid=(n // bn,),
        in_specs=[pl.BlockSpec((rows, d), lambda j: (0, 0)),
                  pl.BlockSpec((d, bn), lambda j: (0, j)),
                  pl.BlockSpec((1, bn), lambda j: (0, j))],
        out_specs=pl.BlockSpec((rows, bn), lambda j: (0, j)),
        out_shape=jax.ShapeDtypeStruct((rows, n), F32),
        compiler_params=_cparams(("arbitrary",)),
        name="mod",
    )(c_pad, w_ada, b_ada)


_R_Q, _R_QI, _R_V, _R_WI, _R_END = 0, 512, 1024, 1536, 1552
_C_K, _C_KK, _C_XR, _C_XG, _C_END = 0, 512, 640, 1152, 1664


def _rope(y, cos, sin):
    lane = lax.broadcasted_iota(I32, (1, LANES), 1)
    low = (lane % HEAD_DIM) < (HEAD_DIM // 2)
    outs = []
    for p in range(y.shape[1] // LANES):
        s = y[:, p * LANES:(p + 1) * LANES]
        swapped = jnp.where(low, pltpu.roll(s, LANES - HEAD_DIM // 2, axis=1),
                            pltpu.roll(s, HEAD_DIM // 2, axis=1))
        outs.append(s * cos + swapped * sin)
    return outs[0] if len(outs) == 1 else jnp.concatenate(outs, axis=1)


def _rope_t(y, cos, sin):
    half = HEAD_DIM // 2
    outs = []
    for hd in range(y.shape[0] // HEAD_DIM):
        blk = y[hd * HEAD_DIM:(hd + 1) * HEAD_DIM, :]
        swapped = jnp.concatenate([blk[half:, :], blk[:half, :]], axis=0)
        outs.append(blk * cos + swapped * sin)
    return jnp.concatenate(outs, axis=0)


def _in_proj_kernel(x_ref, mod_ref, g_ref, wn_ref, wt_ref, cos_ref, sin_ref, cost_ref, sint_ref,
                    qt_ref, qit_ref, wit_ref, k_ref, kk_ref, vt_ref, xr_ref, xg_ref):
    x = x_ref[0]
    shift = mod_ref[0, 0:1, :]
    scale = mod_ref[0, 1:2, :]
    h = (_rms(x, g_ref[...]) * (1.0 + scale) + shift).astype(BF16)
    cos, sin = cos_ref[...], sin_ref[...]
    cos_t, sin_t = cost_ref[...], sint_ref[...]

    def proj(a, b):
        return jnp.dot(h, wn_ref[:, a:b], preferred_element_type=F32)

    def proj_t(a, b):
        return _nt_dot(wt_ref[a:b, :], h)

    qt_ref[0] = (_rope_t(proj_t(_R_Q, _R_QI), cos_t, sin_t)
                 * (HEAD_DIM ** -0.5 * math.log2(math.e))).astype(BF16)
    qit_ref[0] = (_rope_t(proj_t(_R_QI, _R_V), cos_t, sin_t) * (IDX_DIM ** -0.5)).astype(BF16)
    vt = proj_t(_R_V, _R_WI).astype(BF16)
    ones = jnp.ones((V_ROWS - HEAD_DIM, vt.shape[1]), BF16)
    for hd in range(N_HEADS):
        vt_ref[0, 0, hd * V_ROWS:hd * V_ROWS + HEAD_DIM, :] = vt[hd * HEAD_DIM:(hd + 1) * HEAD_DIM, :]
        vt_ref[0, 0, hd * V_ROWS + HEAD_DIM:(hd + 1) * V_ROWS, :] = ones
    wit_ref[0] = proj_t(_R_WI, _R_END)[0:N_IDX_HEADS, :] * (N_IDX_HEADS ** -0.5)
    k_ref[0] = _rope(proj(_C_K, _C_KK), cos, sin).astype(BF16)
    kk_ref[0] = _rope(proj(_C_KK, _C_XR), cos, sin).astype(BF16)
    xr_ref[0] = proj(_C_XR, _C_XG)
    xg_ref[0] = proj(_C_XG, _C_END)


def _in_proj_call(x, mod, g, w_nat, w_tr, cos2, sin2, cos_t, sin_t):
    b, s, d = x.shape
    tm = min(K_BLOCK, s)
    nt = s // tm
    tile = lambda n: pl.BlockSpec((1, tm, n), lambda bi, ti: (bi, ti, 0))
    tile_t = lambda n: pl.BlockSpec((1, n, tm), lambda bi, ti: (bi, 0, ti))
    const = lambda shape: pl.BlockSpec(shape, lambda bi, ti: (0, 0))
    shp = lambda n, dt: jax.ShapeDtypeStruct((b, s, n), dt)
    shp_t = lambda n, dt: jax.ShapeDtypeStruct((b, n, s), dt)
    return pl.pallas_call(
        _in_proj_kernel,
        grid=(b, nt),
        in_specs=[tile(d),
                  pl.BlockSpec((1, N_MOD, d), lambda bi, ti: (bi, 0, 0)),
                  const((1, d)), const((d, _C_END)), const((_R_END, d)),
                  pl.BlockSpec((tm, LANES), lambda bi, ti: (ti, 0)),
                  pl.BlockSpec((tm, LANES), lambda bi, ti: (ti, 0)),
                  pl.BlockSpec((HEAD_DIM, tm), lambda bi, ti: (0, ti)),
                  pl.BlockSpec((HEAD_DIM, tm), lambda bi, ti: (0, ti))],
        out_specs=[tile_t(D_ATTN), tile_t(D_ATTN), tile_t(N_IDX_HEADS), tile(D_ATTN), tile(LANES),
                   pl.BlockSpec((1, 1, N_HEADS * V_ROWS, tm), lambda bi, ti: (bi, ti, 0, 0)),
                   tile(D_RNN), tile(D_RNN)],
        out_shape=[shp_t(D_ATTN, BF16), shp_t(D_ATTN, BF16), shp_t(N_IDX_HEADS, F32),
                   shp(D_ATTN, BF16), shp(LANES, BF16),
                   jax.ShapeDtypeStruct((b, nt, N_HEADS * V_ROWS, tm), BF16),
                   shp(D_RNN, F32), shp(D_RNN, F32)],
        compiler_params=_cparams(("parallel", "parallel")),
        name="in_proj",
    )(x, mod, g, w_nat, w_tr, cos2, sin2, cos_t, sin_t)


def _sortable(score):
    bits = pltpu.bitcast(score, I32)
    return jnp.where(bits < 0, jnp.int32(INT_MIN) - bits, bits)


def _tile_fold(x, op, rows):
    parts = [x[t * rows:(t + 1) * rows, :] for t in range(x.shape[0] // rows)]
    while len(parts) > 1:
        nxt = [op(parts[t], parts[t + 1]) for t in range(0, len(parts) - 1, 2)]
        if len(parts) % 2:
            nxt.append(parts[-1])
        parts = nxt
    return parts[0]


def _row_fold(x, op):
    acc = x[0:SUBLANES, :]
    for t in range(1, x.shape[0] // SUBLANES):
        acc = op(acc, x[t * SUBLANES:(t + 1) * SUBLANES, :])
    return acc


def _fold16(x):
    return _tile_fold(x, jnp.add, 2 * SUBLANES)


def _attn_kernel(qt_ref, qit_ref, wit_ref, kk_ref, k_ref, vt_ref, o_ref,
                 keys_ref, hi_ref, lo_ref, bias_ref, s_ref, p_ref, qm_ref, qim_ref, m_ref, l_ref,
                 acc_ref, *, n_sel, qb, kb, sub):
    i = pl.program_id(1)
    n_ch = ((i + 1) * qb + kb - 1) // kb
    q_pos = i * qb + lax.broadcasted_iota(I32, (1, qb), 1)
    limit = (q_pos // CHUNK + 1) * CHUNK
    krow = lax.broadcasted_iota(I32, (kb, 1), 0)
    row128 = lax.broadcasted_iota(I32, (LANES, 1), 0)

    for h in range(N_HEADS):
        rows = slice((h // 2) * LANES, (h // 2 + 1) * LANES)
        keep = (row128 < HEAD_DIM) if h % 2 == 0 else (row128 >= HEAD_DIM)
        qs, qis = qt_ref[0, rows, :], qit_ref[0, rows, :]
        qm_ref[h] = jnp.where(keep, qs, jnp.zeros_like(qs))
        qim_ref[h] = jnp.where(keep, qis, jnp.zeros_like(qis))
    m_ref[...] = jnp.full(m_ref.shape, NEG_INF, F32)
    l_ref[...] = jnp.zeros(l_ref.shape, F32)
    acc_ref[...] = jnp.zeros(acc_ref.shape, F32)

    wi = wit_ref[0]

    def score_body(j, carry, masked):
        kk = kk_ref[0, pl.ds(pl.multiple_of(j * kb, kb), kb), :]
        acc = jnp.zeros((kb, qb), F32)
        for h in range(N_IDX_HEADS):
            d = jnp.dot(kk, qim_ref[h], preferred_element_type=F32)
            acc = acc + wi[h:h + 1, :] * jnp.maximum(d, 0.0)
        key = _sortable(acc)
        if masked:
            key = jnp.where(j * kb + krow < limit, key, jnp.int32(INT_MIN))
        keys_ref[j] = key
        hi_ref[j] = (key >> 16).astype(I16)
        return carry

    n_open = (i * qb + CHUNK) // kb
    lax.fori_loop(0, n_open, functools.partial(score_body, masked=False), 0)
    lax.fori_loop(n_open, n_ch, functools.partial(score_body, masked=True), 0)

    def count16(ref, pred):
        def one(j):
            return _fold16(jnp.where(pred(ref[j]), jnp.int16(1), jnp.int16(0)))

        def body(jj, acc):
            return acc + (one(2 * jj) + one(2 * jj + 1))

        acc = lax.fori_loop(0, n_ch // 2, body, jnp.zeros((2 * SUBLANES, qb), I16))
        acc = lax.cond(n_ch % 2 == 1, lambda a: a + one(n_ch - 1), lambda a: a, acc)
        return jnp.sum(acc.astype(I32), axis=0, keepdims=True)

    def search16(ref, target):
        c0 = count16(ref, lambda v: v >= jnp.int16(0))
        ok = c0 >= target
        init = (jnp.where(ok, jnp.int32(0), jnp.int32(INT16_MIN)),
                jnp.where(ok, c0, n_ch * kb), jnp.where(ok, 0, c0))

        def bit_body(bi, carry):
            t, n_ge, n_gt = carry
            cand = t | (jnp.int32(1) << (14 - bi))
            c = count16(ref, lambda v: v >= cand.astype(I16))
            ok = c >= target
            return jnp.where(ok, cand, t), jnp.where(ok, c, n_ge), jnp.where(ok, n_gt, c)

        return lax.fori_loop(0, 15, bit_body, init)

    tau_hi, _, n_gt_hi = search16(hi_ref, n_sel)
    tau_hi16 = tau_hi.astype(I16)

    def lo_body(j, carry):
        lo = (keys_ref[j] ^ jnp.int32(0x8000)).astype(I16)
        lo_ref[j] = jnp.where(hi_ref[j] == tau_hi16, lo, jnp.int16(INT16_MIN))
        return carry

    lax.fori_loop(0, n_ch, lo_body, 0)
    tau_lo, n_ge_lo, n_gt_lo = search16(lo_ref, n_sel - n_gt_hi)
    tau = (tau_hi << 16) | ((tau_lo + 32768) & jnp.int32(0xFFFF))
    all_sel = tau == jnp.int32(INT_MIN)
    need_i = jnp.where(all_sel, 0, n_sel - n_gt_hi - n_gt_lo)
    need = need_i.astype(F32)
    ties = jnp.max(jnp.where(all_sel, 0, (n_ge_lo - n_gt_lo) - need_i)) > 0
    thr = jnp.where(all_sel, jnp.int32(INT_MIN + 1), tau)

    n_sub = kb // sub
    no_max = jnp.full((SUBLANES, qb), NEG_INF, F32)

    def bias_ties(j, slot, run):
        r_i = lax.broadcasted_iota(I32, (kb, kb), 0)
        c_i = lax.broadcasted_iota(I32, (kb, kb), 1)
        tri = (c_i <= r_i).astype(BF16)
        kc = keys_ref[j]
        eq = kc == tau
        rank = jnp.dot(tri, eq.astype(BF16), preferred_element_type=F32) + run
        sel = (kc > tau) | (eq & (rank <= need))
        bias_ref[slot] = jnp.where(sel, 0.0, NEG_INF).astype(F32)
        return rank[kb - 1:kb, :]

    def bias_plain(j, slot, run):
        bias_ref[slot] = jnp.where(keys_ref[j] >= thr, 0.0, NEG_INF).astype(F32)
        return run

    gs = HEADS_PER_STAGE
    n_stage = N_HEADS // gs

    def logits(jb, bslot, h, r, mx):
        cols = slice((h // 2) * LANES, (h // 2 + 1) * LANES)
        rows = pl.ds(pl.multiple_of(jb * kb + r * sub, sub), sub)
        x = (jnp.dot(k_ref[0, rows, cols], qm_ref[h], preferred_element_type=F32)
             + bias_ref[bslot, r * sub:(r + 1) * sub, :])
        s_ref[(h // gs % 2) * gs + h % gs, r * sub:(r + 1) * sub, :] = x
        return jnp.maximum(mx, _row_fold(x, jnp.maximum))

    run0 = lax.cond(ties, bias_ties, bias_plain, 0, 0, jnp.zeros((1, qb), F32))
    mx0 = [no_max] * gs
    for r in range(n_sub):
        mx0 = [logits(0, 0, g, r, mx0[g]) for g in range(gs)]

    def attn_body(j, carry):
        run, mx = carry
        mx = list(mx)
        jn = jnp.minimum(j + 1, n_ch - 1)
        run = lax.cond(ties, bias_ties, bias_plain, jn, (j + 1) % 2, run)
        for st in range(n_stage):
            heads = [st * gs + g for g in range(gs)]
            m_new, m_safe, alpha = [], [], []
            for g, h in enumerate(heads):
                m_old = m_ref[h]
                m_new.append(jnp.maximum(m_old, jnp.max(mx[g], axis=0, keepdims=True)))
                m_safe.append(jnp.where(m_new[g] == NEG_INF, 0.0, m_new[g]))
                alpha.append(jnp.exp2(m_old - m_safe[g]))
            mx = [no_max] * gs
            for r in range(n_sub):
                tile = slice(r * sub, (r + 1) * sub)
                for g in range(gs):
                    if st + 1 < n_stage:
                        mx[g] = logits(j, j % 2, heads[g] + gs, r, mx[g])
                    else:
                        mx[g] = logits(jn, (j + 1) % 2, g, r, mx[g])
                for g in range(gs):
                    p_ref[g, tile, :] = jnp.exp2(
                        s_ref[(st % 2) * gs + g, tile, :] - m_safe[g]).astype(BF16)
            for g, h in enumerate(heads):
                pv = jnp.dot(vt_ref[0, j, h * V_ROWS:(h + 1) * V_ROWS, :], p_ref[g],
                             preferred_element_type=F32)
                l_ref[h] = alpha[g] * l_ref[h] + pv[HEAD_DIM:HEAD_DIM + 1, :]
                acc_ref[h] = alpha[g] * acc_ref[h] + pv[0:HEAD_DIM, :]
                m_ref[h] = m_new[g]
        return run, tuple(mx)

    lax.fori_loop(0, n_ch, attn_body, (run0, tuple(mx0)))

    outs = [acc_ref[h] / l_ref[h] for h in range(N_HEADS)]
    o_ref[0] = jnp.concatenate(outs, axis=0).T.astype(o_ref.dtype)


def _attn_call(qt, qit, wit, kk, k, vt):
    b, s, _ = k.shape
    qb = min(Q_BLOCK, s)
    kb = min(K_BLOCK, s)
    n_sel = min(TOPK_MAX, s // 4)
    blk_t = lambda n: pl.BlockSpec((1, n, qb), lambda bi, qi_: (bi, 0, qi_))
    once = pl.Buffered(1)
    kern = functools.partial(_attn_kernel, n_sel=n_sel, qb=qb, kb=kb, sub=min(K_SUB, kb))
    return pl.pallas_call(
        kern,
        grid=(b, s // qb),
        in_specs=[blk_t(D_ATTN), blk_t(D_ATTN), blk_t(N_IDX_HEADS),
                  pl.BlockSpec((1, s, LANES), lambda bi, qi_: (bi, 0, 0), pipeline_mode=once),
                  pl.BlockSpec((1, s, D_ATTN), lambda bi, qi_: (bi, 0, 0), pipeline_mode=once),
                  pl.BlockSpec((1, s // kb, N_HEADS * V_ROWS, kb), lambda bi, qi_: (bi, 0, 0, 0),
                               pipeline_mode=once)],
        out_specs=pl.BlockSpec((1, qb, D_ATTN), lambda bi, qi_: (bi, qi_, 0)),
        out_shape=jax.ShapeDtypeStruct((b, s, D_ATTN), BF16),
        scratch_shapes=[pltpu.VMEM((s // kb, kb, qb), I32),
                        pltpu.VMEM((s // kb, kb, qb), I16),
                        pltpu.VMEM((s // kb, kb, qb), I16),
                        pltpu.VMEM((2, kb, qb), F32),
                        pltpu.VMEM((2 * HEADS_PER_STAGE, kb, qb), F32),
                        pltpu.VMEM((HEADS_PER_STAGE, kb, qb), BF16),
                        pltpu.VMEM((N_HEADS, LANES, qb), BF16),
                        pltpu.VMEM((N_IDX_HEADS, LANES, qb), BF16),
                        pltpu.VMEM((N_HEADS, 1, qb), F32),
                        pltpu.VMEM((N_HEADS, 1, qb), F32),
                        pltpu.VMEM((N_HEADS, HEAD_DIM, qb), F32)],
        compiler_params=_cparams(("parallel", "arbitrary")),
        name="attn",
    )(qt, qit, wit, kk, k, vt)


def _gelu_tanh(x):
    return 0.5 * x * (1.0 + jnp.tanh(math.sqrt(2.0 / math.pi) * (x + 0.044715 * (x * x * x))))


def _rglru_kernel(xr_ref, xg_ref, cw_ref, cb_ref, wa_ref, ba_ref, wx_ref, bx_ref, lam_ref,
                  o_ref, xbuf_ref, h_ref, *, ts):
    t = pl.program_id(1)
    pad = 8

    @pl.when(t == 0)
    def _():
        xbuf_ref[0:pad, :] = jnp.zeros((pad, D_RNN), F32)
        h_ref[...] = jnp.zeros_like(h_ref)

    xr = xr_ref[0]
    xbuf_ref[pad:pad + ts, :] = xr
    conv = cb_ref[...] + cw_ref[CONV_WIDTH - 1:CONV_WIDTH, :] * xr
    for w in range(CONV_WIDTH - 1):
        off = pad - (CONV_WIDTH - 1) + w
        conv = conv + cw_ref[w:w + 1, :] * xbuf_ref[off:off + ts, :]
    xbuf_ref[0:pad, :] = xr[ts - pad:ts, :]

    cb16 = conv.astype(BF16)
    r = jax.nn.sigmoid(jnp.dot(cb16, wa_ref[...], preferred_element_type=F32) + ba_ref[...])
    ig = jax.nn.sigmoid(jnp.dot(cb16, wx_ref[...], preferred_element_type=F32) + bx_ref[...])
    z = -lam_ref[...]
    softplus = jnp.maximum(z, 0.0) + jnp.log1p(jnp.exp(-jnp.abs(z)))
    log_a = (-RG_C) * r * softplus
    a = jnp.exp(log_a)
    bb = jnp.sqrt(1.0 - a * a) * (ig * conv)

    row = lax.broadcasted_iota(I32, (ts, 1), 0)
    d = 1
    while d < ts:
        if d < SUBLANES:
            live = row >= d
            a_s = jnp.where(live, pltpu.roll(a, d, axis=0), 1.0)
            b_s = jnp.where(live, pltpu.roll(bb, d, axis=0), 0.0)
            bb = bb + a * b_s
            a = a * a_s
        else:
            lo_a, hi_a = a[:d, :], a[d:, :]
            bb = jnp.concatenate([bb[:d, :], bb[d:, :] + hi_a * bb[:ts - d, :]], axis=0)
            a = jnp.concatenate([lo_a, hi_a * a[:ts - d, :]], axis=0)
        d *= 2
    h = bb + a * h_ref[0:1, :]
    h_ref[0:1, :] = h[ts - 1:ts, :]
    o_ref[0] = (h * _gelu_tanh(xg_ref[0])).astype(o_ref.dtype)


def _rglru_call(xr, xg, conv_w, conv_b, wa, ba, wx, bx, lam):
    b, s, r = xr.shape
    ts = min(TOKEN_TILE, s)
    tile = pl.BlockSpec((1, ts, r), lambda bi, ti: (bi, ti, 0))
    const = lambda shape: pl.BlockSpec(shape, lambda bi, ti: (0, 0))
    return pl.pallas_call(
        functools.partial(_rglru_kernel, ts=ts),
        grid=(b, s // ts),
        in_specs=[tile, tile, const((CONV_WIDTH, r)), const((1, r)), const((r, r)), const((1, r)),
                  const((r, r)), const((1, r)), const((1, r))],
        out_specs=tile,
        out_shape=jax.ShapeDtypeStruct((b, s, r), BF16),
        scratch_shapes=[pltpu.VMEM((ts + 8, r), F32), pltpu.VMEM((8, r), F32)],
        compiler_params=_cparams(("parallel", "arbitrary")),
        name="rglru",
    )(xr, xg, conv_w, conv_b, wa, ba, wx, bx, lam)


def _route(logits):
    lane = lax.broadcasted_iota(I32, logits.shape, 1)
    is_g = (lane >= N_EXPERTS) & (lane < N_EXPERTS + N_GROUPS)
    big = jnp.int32(LANES)

    def first_lane(mask):
        return jnp.min(jnp.where(mask, lane, big), axis=1, keepdims=True)

    gl = jnp.where(is_g, logits, NEG_INF)
    ge = jnp.exp(gl - jnp.max(gl, axis=1, keepdims=True))
    p_groups = ge / jnp.sum(ge, axis=1, keepdims=True)
    p_g = jnp.max(p_groups, axis=1, keepdims=True)
    g_idx = first_lane(is_g & (p_groups == p_g)) - N_EXPERTS

    in_grp = (lane < N_EXPERTS) & ((lane // EXP_PER_GROUP) == g_idx)
    el = jnp.where(in_grp, logits, NEG_INF)
    ee = jnp.exp(el - jnp.max(el, axis=1, keepdims=True))
    es = ee / jnp.sum(ee, axis=1, keepdims=True)
    es = jnp.where(in_grp, es, NEG_INF)
    v1 = jnp.max(es, axis=1, keepdims=True)
    i1 = first_lane(in_grp & (es == v1))
    es2 = jnp.where(lane == i1, NEG_INF, es)
    v2 = jnp.max(es2, axis=1, keepdims=True)
    i2 = first_lane(in_grp & (es2 == v2))
    tot = v1 + v2
    e_w = jnp.where(lane == i1, v1 / tot, 0.0) + jnp.where(lane == i2, v2 / tot, 0.0)
    return p_g * e_w + jnp.where(lane == ROUTE_GROUP_LANE + g_idx, 1.0, 0.0)


def _out_proj_kernel(ya_ref, yr_ref, x_ref, mod_ref, wo_ref, gpm_ref, gpf_ref, wr_ref, br_ref,
                     x1_ref, h2_ref, comb_ref):
    mix = (jnp.dot(ya_ref[0], wo_ref[0:D_ATTN, :], preferred_element_type=F32)
           + jnp.dot(yr_ref[0], wo_ref[D_ATTN:D_ATTN + D_RNN, :], preferred_element_type=F32))
    gate1 = mod_ref[0, 2:3, :]
    shift2 = mod_ref[0, 3:4, :]
    scale2 = mod_ref[0, 4:5, :]
    x1 = x_ref[0] + gate1 * _rms(mix, gpm_ref[...])
    x1_ref[0] = x1
    h2 = _rms(x1, gpf_ref[...]) * (1.0 + scale2) + shift2
    h_hi = h2.astype(BF16)
    h2_ref[0] = h_hi
    h_lo = (h2 - h_hi.astype(F32)).astype(BF16)
    part = jnp.dot(h_hi, wr_ref[...], preferred_element_type=F32)
    logits = (part[:, 0:LANES] + (part[:, LANES:2 * LANES]
              + jnp.dot(h_lo, wr_ref[:, 0:LANES], preferred_element_type=F32))) + br_ref[...]
    comb_ref[0] = _route(logits)


def _out_proj_call(ya, yr, x, mod, wo, gpm, gpf, wr, br):
    b, s, d = x.shape
    tm = min(TOKEN_TILE, s)
    tile = lambda n: pl.BlockSpec((1, tm, n), lambda bi, ti: (bi, ti, 0))
    const = lambda shape: pl.BlockSpec(shape, lambda bi, ti: (0,) * len(shape))
    return pl.pallas_call(
        _out_proj_kernel,
        grid=(b, s // tm),
        in_specs=[tile(D_ATTN), tile(D_RNN), tile(d),
                  pl.BlockSpec((1, N_MOD, d), lambda bi, ti: (bi, 0, 0)),
                  const((D_ATTN + D_RNN, d)), const((1, d)), const((1, d)),
                  const((d, 2 * LANES)), const((1, LANES))],
        out_specs=[tile(d), tile(d), tile(LANES)],
        out_shape=[jax.ShapeDtypeStruct((b, s, d), F32), jax.ShapeDtypeStruct((b, s, d), BF16),
                   jax.ShapeDtypeStruct((b, s, LANES), F32)],
        compiler_params=_cparams(("parallel", "parallel")),
        name="out_proj",
    )(ya, yr, x, mod, wo, gpm, gpf, wr, br)


def _moe_kernel(h2_ref, comb_ref, x1_ref, mod_ref, wg_ref, wu_ref, wd_ref, gpo_ref,
                o_ref, hs_ref, cs_ref, ys_ref, act_ref, slot_ref, seg_ref, *, tm, ns, rb):
    g = pl.program_id(2)
    lane = lax.broadcasted_iota(I32, (1, LANES), 1)
    n_blk = tm // LANES

    @pl.when(g == 0)
    def _dispatch():
        comb = comb_ref[0]
        oh = jnp.where((lane >= ROUTE_GROUP_LANE) & (lane < ROUTE_GROUP_LANE + N_GROUPS), comb, 0.0)
        oh16 = oh.astype(BF16)
        sub = lax.broadcasted_iota(I32, (LANES, 1), 0)
        base = jnp.int32(0)
        base_lane = jnp.zeros((1, LANES), F32)
        base_sub = jnp.zeros((LANES, 1), F32)
        for gg in range(N_GROUPS):
            here = ROUTE_GROUP_LANE + gg
            n = jnp.sum(jnp.where(lane == here, oh, 0.0)).astype(I32)
            nb = (n + rb - 1) // rb
            seg_ref[gg] = base
            seg_ref[N_GROUPS + gg] = nb
            base_f = base.astype(F32)
            base_lane = base_lane + jnp.where(lane == here, base_f, 0.0)
            base_sub = base_sub + jnp.where(sub == here, base_f, 0.0)
            base = base + nb * rb

        tok_c = lax.broadcasted_iota(I32, (1, tm), 1)
        tok_r = lax.broadcasted_iota(I32, (tm, 1), 0)
        blk_r = lax.broadcasted_iota(I32, (LANES, 1), 0)
        blk_c = lax.broadcasted_iota(I32, (1, LANES), 1)
        eye = (blk_r == blk_c).astype(BF16)
        oht16 = _nt_dot(eye, oh16).astype(BF16)
        slot_row = []
        for t in range(n_blk):
            tril = (tok_c <= blk_r + t * LANES).astype(BF16)
            rank = jnp.dot(tril, oh16, preferred_element_type=F32)
            oh_blk = oh[t * LANES:(t + 1) * LANES, :]
            slot_ref[t * LANES:(t + 1) * LANES, :] = jnp.sum(
                oh_blk * (base_lane + rank - 1.0), axis=1, keepdims=True).astype(I32)
            triu = (tok_r <= blk_c + t * LANES).astype(BF16)
            rank_t = jnp.dot(oht16, triu, preferred_element_type=F32)
            oht_blk = oht16[:, t * LANES:(t + 1) * LANES].astype(F32)
            slot_row.append(jnp.sum(oht_blk * (base_sub + rank_t - 1.0), axis=0,
                                    keepdims=True).astype(I32))
        slot_row = jnp.concatenate(slot_row, axis=1)

        hi = comb.astype(BF16)
        lo = (comb - hi.astype(F32)).astype(BF16)
        cw = jnp.concatenate([hi, lo], axis=1)
        h = h2_ref[0]
        for r in range(ns // LANES):
            rows = slice(r * LANES, (r + 1) * LANES)
            p = (blk_r + r * LANES == slot_row).astype(BF16)
            hs_ref[rows, :] = jnp.dot(p, h, preferred_element_type=F32).astype(BF16)
            cs_ref[rows, :] = jnp.dot(p, cw, preferred_element_type=F32).astype(BF16)
        ys_ref[...] = jnp.zeros_like(ys_ref)

    n_act = EXP_PER_GROUP * D_EXPERT
    r_i = lax.broadcasted_iota(I32, (2 * LANES, n_act), 0) % LANES
    c_i = lax.broadcasted_iota(I32, (2 * LANES, n_act), 1) // D_EXPERT
    expand = (r_i == g * EXP_PER_GROUP + c_i).astype(BF16)
    base = seg_ref[g]

    def block(r, carry):
        rows = pl.ds(pl.multiple_of(base + r * rb, rb), rb)
        hb = hs_ref[rows, :]
        cexp = jnp.dot(cs_ref[rows, :], expand, preferred_element_type=F32)
        for e in range(EXP_PER_GROUP):
            gt = jnp.dot(hb, wg_ref[0, e], preferred_element_type=F32)
            up = jnp.dot(hb, wu_ref[0, e], preferred_element_type=F32)
            cols = slice(e * D_EXPERT, (e + 1) * D_EXPERT)
            act_ref[:, cols] = (gt * jax.nn.sigmoid(gt) * up * cexp[:, cols]).astype(BF16)
        ys_ref[rows, :] = jnp.dot(act_ref[...], wd_ref[0], preferred_element_type=F32).astype(BF16)
        return carry

    lax.fori_loop(0, seg_ref[N_GROUPS + g], block, 0)

    @pl.when(g == N_GROUPS - 1)
    def _combine():
        gate2 = mod_ref[0, 5:6, :]
        slot_c = lax.broadcasted_iota(I32, (1, ns), 1)
        half = tm // 2
        for hh in range(2):
            rows = slice(hh * half, (hh + 1) * half)
            pt = (slot_c == slot_ref[rows, :]).astype(BF16)
            y = jnp.dot(pt, ys_ref[...], preferred_element_type=F32)
            o_ref[0, rows, :] = x1_ref[0, rows, :] + gate2 * _rms(y, gpo_ref[...])


def _moe_call(h2, comb, x1, mod, wg, wu, wd, gpo):
    b, s, d = x1.shape
    tm = min(MOE_TILE, s)
    rb = MOE_ROWS
    ns = tm + N_GROUPS * rb
    tile = lambda n, **kw: pl.BlockSpec((1, tm, n), lambda bi, ti, gi: (bi, ti, 0), **kw)
    kern = functools.partial(_moe_kernel, tm=tm, ns=ns, rb=rb)
    return pl.pallas_call(
        kern,
        grid=(b, s // tm, N_GROUPS),
        in_specs=[tile(d), tile(LANES), tile(d, pipeline_mode=pl.Buffered(1)),
                  pl.BlockSpec((1, N_MOD, d), lambda bi, ti, gi: (bi, 0, 0)),
                  pl.BlockSpec((1, EXP_PER_GROUP, d, D_EXPERT), lambda bi, ti, gi: (gi, 0, 0, 0)),
                  pl.BlockSpec((1, EXP_PER_GROUP, d, D_EXPERT), lambda bi, ti, gi: (gi, 0, 0, 0)),
                  pl.BlockSpec((1, EXP_PER_GROUP * D_EXPERT, d), lambda bi, ti, gi: (gi, 0, 0)),
                  pl.BlockSpec((1, d), lambda bi, ti, gi: (0, 0))],
        out_specs=tile(d),
        out_shape=jax.ShapeDtypeStruct((b, s, d), F32),
        scratch_shapes=[pltpu.VMEM((ns, d), BF16),
                        pltpu.VMEM((ns, 2 * LANES), BF16),
                        pltpu.VMEM((ns, d), BF16),
                        pltpu.VMEM((rb, EXP_PER_GROUP * D_EXPERT), BF16),
                        pltpu.VMEM((tm, 1), I32),
                        pltpu.SMEM((2 * N_GROUPS,), I32)],
        compiler_params=_cparams(("parallel", "parallel", "arbitrary")),
        name="moe",
    )(h2, comb, x1, mod, wg, wu, wd, gpo)


def _rope_tables(seq_len):
    pos = jnp.arange(seq_len, dtype=F32)
    inv = ROPE_THETA ** (-jnp.arange(0, HEAD_DIM, 2, dtype=F32) / HEAD_DIM)
    ang = pos[:, None] * inv[None, :]
    cos, sin = jnp.cos(ang), jnp.sin(ang)
    cos2 = jnp.concatenate([cos, cos, cos, cos], axis=1)
    sin2 = jnp.concatenate([-sin, sin, -sin, sin], axis=1)
    return cos2, sin2, cos2[:, :HEAD_DIM].T, sin2[:, :HEAD_DIM].T


def _block_diag(w):
    n, c, d = w.shape
    eye = jnp.eye(n, dtype=w.dtype)
    return (eye[:, None, :, None] * w[:, :, None, :]).reshape(n * c, n * d)


def _pad_cols(w, n):
    return jnp.pad(w, ((0, 0), (0, n - w.shape[1])))


def kernel(x, c, w_ada, b_ada, g_pre_mix, g_post_mix, g_pre_ffn, g_post_ffn, w_in, conv_w, conv_b, w_rg_a, b_rg_a, w_rg_x, b_rg_x, lru_lambda, w_out, w_router_group, b_router_group, w_router_expert, b_router_expert, w_gate, w_up, w_down):
    b, s, d = x.shape
    depth = w_ada.shape[0]
    cos2, sin2, cos_t, sin_t = _rope_tables(s)
    c_pad = jnp.pad(c, ((0, (-b) % SUBLANES), (0, 0)))
    o_q, o_k, o_v, o_qi, o_ki, o_wi, o_xr, o_xg = 0, 512, 1024, 1536, 2048, 2112, 2120, 2632
    for l in range(depth):
        wl = w_in[l]
        w_ki = wl[:, o_ki:o_wi]
        w_nat = jnp.concatenate([wl[:, o_k:o_v], w_ki, w_ki, wl[:, o_xr:o_xg], wl[:, o_xg:]],
                                axis=1).astype(BF16)
        w_tr = jnp.concatenate([wl[:, o_q:o_k], wl[:, o_qi:o_ki], wl[:, o_v:o_qi],
                                _pad_cols(wl[:, o_wi:o_xr], _R_END - _R_WI)], axis=1).T.astype(BF16)
        w_route = _pad_cols(jnp.concatenate([w_router_expert[l], w_router_group[l]], axis=1), LANES)
        w_route_hi = w_route.astype(BF16)
        w_route = jnp.concatenate(
            [w_route_hi, (w_route - w_route_hi.astype(F32)).astype(BF16)], axis=1)
        b_route = _pad_cols(jnp.concatenate([b_router_expert[l], b_router_group[l]])[None, :], LANES)

        mod = _mod_call(c_pad, w_ada[l], b_ada[l][None, :])[:b].reshape(b, N_MOD, d)
        qt, qit, wit, k, kk, vt, xr, xg = _in_proj_call(
            x, mod, g_pre_mix[l][None, :], w_nat, w_tr, cos2, sin2, cos_t, sin_t)
        y_attn = _attn_call(qt, qit, wit, kk, k, vt)
        y_rnn = _rglru_call(xr, xg, conv_w[l], conv_b[l][None, :],
                            _block_diag(w_rg_a[l]).astype(BF16), b_rg_a[l][None, :],
                            _block_diag(w_rg_x[l]).astype(BF16), b_rg_x[l][None, :],
                            lru_lambda[l][None, :])
        x1, h2, comb = _out_proj_call(y_attn, y_rnn, x, mod, w_out[l].astype(BF16),
                                      g_post_mix[l][None, :], g_pre_ffn[l][None, :], w_route, b_route)
        x = _moe_call(h2, comb, x1, mod, w_gate[l].astype(BF16), w_up[l].astype(BF16),
                      w_down[l].reshape(N_GROUPS, EXP_PER_GROUP * D_EXPERT, d).astype(BF16),
                      g_post_ffn[l][None, :])
    return x
```

```python
import functools
import math

import jax
import jax.numpy as jnp
from jax import lax
from jax.experimental import pallas as pl
from jax.experimental.pallas import tpu as pltpu

F32 = jnp.float32
BF16 = jnp.bfloat16
I32 = jnp.int32
I16 = jnp.int16

D_MODEL = 1024
CHUNK = 64
ROPE_THETA = 10000.0
EPS = 1e-6
N_HEADS = 8
HEAD_DIM = 64
D_ATTN = N_HEADS * HEAD_DIM
N_IDX_HEADS = 8
IDX_DIM = 64
TOPK_MAX = 256
D_RNN = 512
N_RNN_BLOCKS = 8
CONV_WIDTH = 4
RG_C = 8.0
N_GROUPS = 4
EXP_PER_GROUP = 8
N_EXPERTS = N_GROUPS * EXP_PER_GROUP
D_EXPERT = 256
N_MOD = 6

LANES = 128
SUBLANES = 8
INT_MIN = -2 ** 31
INT16_MIN = -2 ** 15
NEG_INF = float("-inf")

Q_BLOCK = 256
K_BLOCK = 512
K_SUB = 256
HEADS_PER_STAGE = 2
V_ROWS = HEAD_DIM + 16
TOKEN_TILE = 512
MOE_TILE = 1024
MOE_ROWS = 144
ROUTE_GROUP_LANE = 64
VMEM_LIMIT = 60 * 1024 * 1024


def _cparams(sem):
    return pltpu.CompilerParams(dimension_semantics=sem, vmem_limit_bytes=VMEM_LIMIT)


def _nt_dot(a, b):
    return lax.dot_general(a, b, (((1,), (1,)), ((), ())), preferred_element_type=F32)


def _rms(x, g):
    return x * lax.rsqrt(jnp.mean(x * x, axis=-1, keepdims=True) + EPS) * g


def _mod_kernel(c_ref, w_ref, b_ref, o_ref):
    c = c_ref[...]
    sc = c * jax.nn.sigmoid(c)
    o_ref[...] = jnp.dot(sc, w_ref[...], preferred_element_type=F32,
                         precision=lax.Precision.HIGHEST) + b_ref[...]


def _mod_call(c_pad, w_ada, b_ada):
    rows, d = c_pad.shape
    n = w_ada.shape[1]
    bn = 1024
    return pl.pallas_call(
        _mod_kernel,
        grid=(n // bn,),
        in_specs=[pl.BlockSpec((rows, d), lambda j: (0, 0)),
                  pl.BlockSpec((d, bn), lambda j: (0, j)),
                  pl.BlockSpec((1, bn), lambda j: (0, j))],
        out_specs=pl.BlockSpec((rows, bn), lambda j: (0, j)),
        out_shape=jax.ShapeDtypeStruct((rows, n), F32),
        compiler_params=_cparams(("arbitrary",)),
        name="mod",
    )(c_pad, w_ada, b_ada)


_R_Q, _R_QI, _R_V, _R_WI, _R_END = 0, 512, 1024, 1536, 1552
_C_K, _C_KK, _C_XR, _C_XG, _C_END = 0, 512, 640, 1152, 1664


def _rope(y, cos, sin):
    lane = lax.broadcasted_iota(I32, (1, LANES), 1)
    low = (lane % HEAD_DIM) < (HEAD_DIM // 2)
    outs = []
    for p in range(y.shape[1] // LANES):
        s = y[:, p * LANES:(p + 1) * LANES]
        swapped = jnp.where(low, pltpu.roll(s, LANES - HEAD_DIM // 2, axis=1),
                            pltpu.roll(s, HEAD_DIM // 2, axis=1))
        outs.append(s * cos + swapped * sin)
    return outs[0] if len(outs) == 1 else jnp.concatenate(outs, axis=1)


def _rope_t(y, cos, sin):
    half = HEAD_DIM // 2
    outs = []
    for hd in range(y.shape[0] // HEAD_DIM):
        blk = y[hd * HEAD_DIM:(hd + 1) * HEAD_DIM, :]
        swapped = jnp.concatenate([blk[half:, :], blk[:half, :]], axis=0)
        outs.append(blk * cos + swapped * sin)
    return jnp.concatenate(outs, axis=0)


def _in_proj_kernel(x_ref, mod_ref, g_ref, wn_ref, wt_ref, cos_ref, sin_ref, cost_ref, sint_ref,
                    qt_ref, qit_ref, wit_ref, k_ref, kk_ref, vt_ref, xr_ref, xg_ref):
    x = x_ref[0]
    shift = mod_ref[0, 0:1, :]
    scale = mod_ref[0, 1:2, :]
    h = (_rms(x, g_ref[...]) * (1.0 + scale) + shift).astype(BF16)
    cos, sin = cos_ref[...], sin_ref[...]
    cos_t, sin_t = cost_ref[...], sint_ref[...]

    def proj(a, b):
        return jnp.dot(h, wn_ref[:, a:b], preferred_element_type=F32)

    def proj_t(a, b):
        return _nt_dot(wt_ref[a:b, :], h)

    qt_ref[0] = (_rope_t(proj_t(_R_Q, _R_QI), cos_t, sin_t)
                 * (HEAD_DIM ** -0.5 * math.log2(math.e))).astype(BF16)
    qit_ref[0] = (_rope_t(proj_t(_R_QI, _R_V), cos_t, sin_t) * (IDX_DIM ** -0.5)).astype(BF16)
    vt = proj_t(_R_V, _R_WI).astype(BF16)
    ones = jnp.ones((V_ROWS - HEAD_DIM, vt.shape[1]), BF16)
    for hd in range(N_HEADS):
        vt_ref[0, 0, hd * V_ROWS:hd * V_ROWS + HEAD_DIM, :] = vt[hd * HEAD_DIM:(hd + 1) * HEAD_DIM, :]
        vt_ref[0, 0, hd * V_ROWS + HEAD_DIM:(hd + 1) * V_ROWS, :] = ones
    wit_ref[0] = proj_t(_R_WI, _R_END)[0:N_IDX_HEADS, :] * (N_IDX_HEADS ** -0.5)
    k_ref[0] = _rope(proj(_C_K, _C_KK), cos, sin).astype(BF16)
    kk_ref[0] = _rope(proj(_C_KK, _C_XR), cos, sin).astype(BF16)
    xr_ref[0] = proj(_C_XR, _C_XG)
    xg_ref[0] = proj(_C_XG, _C_END)


def _in_proj_call(x, mod, g, w_nat, w_tr, cos2, sin2, cos_t, sin_t):
    b, s, d = x.shape
    tm = min(K_BLOCK, s)
    nt = s // tm
    tile = lambda n: pl.BlockSpec((1, tm, n), lambda bi, ti: (bi, ti, 0))
    tile_t = lambda n: pl.BlockSpec((1, n, tm), lambda bi, ti: (bi, 0, ti))
    const = lambda shape: pl.BlockSpec(shape, lambda bi, ti: (0, 0))
    shp = lambda n, dt: jax.ShapeDtypeStruct((b, s, n), dt)
    shp_t = lambda n, dt: jax.ShapeDtypeStruct((b, n, s), dt)
    return pl.pallas_call(
        _in_proj_kernel,
        grid=(b, nt),
        in_specs=[tile(d),
                  pl.BlockSpec((1, N_MOD, d), lambda bi, ti: (bi, 0, 0)),
                  const((1, d)), const((d, _C_END)), const((_R_END, d)),
                  pl.BlockSpec((tm, LANES), lambda bi, ti: (ti, 0)),
                  pl.BlockSpec((tm, LANES), lambda bi, ti: (ti, 0)),
                  pl.BlockSpec((HEAD_DIM, tm), lambda bi, ti: (0, ti)),
                  pl.BlockSpec((HEAD_DIM, tm), lambda bi, ti: (0, ti))],
        out_specs=[tile_t(D_ATTN), tile_t(D_ATTN), tile_t(N_IDX_HEADS), tile(D_ATTN), tile(LANES),
                   pl.BlockSpec((1, 1, N_HEADS * V_ROWS, tm), lambda bi, ti: (bi, ti, 0, 0)),
                   tile(D_RNN), tile(D_RNN)],
        out_shape=[shp_t(D_ATTN, BF16), shp_t(D_ATTN, BF16), shp_t(N_IDX_HEADS, F32),
                   shp(D_ATTN, BF16), shp(LANES, BF16),
                   jax.ShapeDtypeStruct((b, nt, N_HEADS * V_ROWS, tm), BF16),
                   shp(D_RNN, F32), shp(D_RNN, F32)],
        compiler_params=_cparams(("parallel", "parallel")),
        name="in_proj",
    )(x, mod, g, w_nat, w_tr, cos2, sin2, cos_t, sin_t)


def _sortable(score):
    bits = pltpu.bitcast(score, I32)
    return jnp.where(bits < 0, jnp.int32(INT_MIN) - bits, bits)


def _tile_fold(x, op, rows):
    parts = [x[t * rows:(t + 1) * rows, :] for t in range(x.shape[0] // rows)]
    while len(parts) > 1:
        nxt = [op(parts[t], parts[t + 1]) for t in range(0, len(parts) - 1, 2)]
        if len(parts) % 2:
            nxt.append(parts[-1])
        parts = nxt
    return parts[0]


def _row_fold(x, op):
    acc = x[0:SUBLANES, :]
    for t in range(1, x.shape[0] // SUBLANES):
        acc = op(acc, x[t * SUBLANES:(t + 1) * SUBLANES, :])
    return acc


def _fold16(x):
    return _tile_fold(x, jnp.add, 2 * SUBLANES)


def _attn_kernel(qt_ref, qit_ref, wit_ref, qitn_ref, witn_ref, kk_ref, k_ref, vt_ref, o_ref,
                 keys_ref, hi_ref, lo_ref, bias_ref, s_ref, p_ref, sacc_ref, qm_ref, qim_ref, m_ref,
                 l_ref, acc_ref, *, n_sel, qb, kb, sub):
    i = pl.program_id(1)
    n_blocks = pl.num_programs(1)
    cur = i % 2
    nxt = 1 - cur
    n_ch = ((i + 1) * qb + kb - 1) // kb
    lane_q = lax.broadcasted_iota(I32, (1, qb), 1)
    limit = ((i * qb + lane_q) // CHUNK + 1) * CHUNK
    limit_next = (((i + 1) * qb + lane_q) // CHUNK + 1) * CHUNK
    krow = lax.broadcasted_iota(I32, (kb, 1), 0)
    row128 = lax.broadcasted_iota(I32, (LANES, 1), 0)

    def head_rows(h):
        return slice((h // 2) * LANES, (h // 2 + 1) * LANES)

    def keep_rows(h):
        return (row128 < HEAD_DIM) if h % 2 == 0 else (row128 >= HEAD_DIM)

    def load_index_queries(src_ref):
        for h in range(N_IDX_HEADS):
            qis = src_ref[0, head_rows(h), :]
            qim_ref[h] = jnp.where(keep_rows(h), qis, jnp.zeros_like(qis))

    for h in range(N_HEADS):
        qs = qt_ref[0, head_rows(h), :]
        qm_ref[h] = jnp.where(keep_rows(h), qs, jnp.zeros_like(qs))
    m_ref[...] = jnp.full(m_ref.shape, NEG_INF, F32)
    l_ref[...] = jnp.zeros(l_ref.shape, F32)
    acc_ref[...] = jnp.zeros(acc_ref.shape, F32)

    def score_block(j, buf, wi, lim):
        kk = kk_ref[0, pl.ds(pl.multiple_of(j * kb, kb), kb), :]
        acc = jnp.zeros((kb, qb), F32)
        for h in range(N_IDX_HEADS):
            d = jnp.dot(kk, qim_ref[h], preferred_element_type=F32)
            acc = acc + wi[h:h + 1, :] * jnp.maximum(d, 0.0)
        key = jnp.where(j * kb + krow < lim, _sortable(acc), jnp.int32(INT_MIN))
        keys_ref[buf, j] = key
        hi_ref[buf, j] = (key >> 16).astype(I16)

    @pl.when(i == 0)
    def _first_block():
        load_index_queries(qit_ref)
        wi = wit_ref[0]

        def body(j, carry):
            score_block(j, cur, wi, limit)
            return carry

        lax.fori_loop(0, n_ch, body, 0)

    load_index_queries(qitn_ref)
    wi_next = witn_ref[0]

    def count16(get, pred):
        def one(j):
            return _fold16(jnp.where(pred(get(j)), jnp.int16(1), jnp.int16(0)))

        def body(jj, acc):
            return acc + (one(2 * jj) + one(2 * jj + 1))

        acc = lax.fori_loop(0, n_ch // 2, body, jnp.zeros((2 * SUBLANES, qb), I16))
        acc = lax.cond(n_ch % 2 == 1, lambda a: a + one(n_ch - 1), lambda a: a, acc)
        return jnp.sum(acc.astype(I32), axis=0, keepdims=True)

    def search16(get, target):
        ref = get
        c0 = count16(ref, lambda v: v >= jnp.int16(0))
        ok = c0 >= target
        init = (jnp.where(ok, jnp.int32(0), jnp.int32(INT16_MIN)),
                jnp.where(ok, c0, n_ch * kb), jnp.where(ok, 0, c0))

        def bit_body(bi, carry):
            t, n_ge, n_gt = carry
            cand = t | (jnp.int32(1) << (14 - bi))
            c = count16(ref, lambda v: v >= cand.astype(I16))
            ok = c >= target
            return jnp.where(ok, cand, t), jnp.where(ok, c, n_ge), jnp.where(ok, n_gt, c)

        return lax.fori_loop(0, 15, bit_body, init)

    tau_hi, _, n_gt_hi = search16(lambda j: hi_ref[cur, j], n_sel)
    tau_hi16 = tau_hi.astype(I16)

    def lo_body(j, carry):
        lo = (keys_ref[cur, j] ^ jnp.int32(0x8000)).astype(I16)
        lo_ref[j] = jnp.where(hi_ref[cur, j] == tau_hi16, lo, jnp.int16(INT16_MIN))
        return carry

    lax.fori_loop(0, n_ch, lo_body, 0)
    tau_lo, n_ge_lo, n_gt_lo = search16(lambda j: lo_ref[j], n_sel - n_gt_hi)
    tau = (tau_hi << 16) | ((tau_lo + 32768) & jnp.int32(0xFFFF))
    all_sel = tau == jnp.int32(INT_MIN)
    need_i = jnp.where(all_sel, 0, n_sel - n_gt_hi - n_gt_lo)
    need = need_i.astype(F32)
    ties = jnp.max(jnp.where(all_sel, 0, (n_ge_lo - n_gt_lo) - need_i)) > 0
    thr = jnp.where(all_sel, jnp.int32(INT_MIN + 1), tau)

    n_sub = kb // sub
    no_max = jnp.full((SUBLANES, qb), NEG_INF, F32)

    def bias_ties(j, slot, run):
        r_i = lax.broadcasted_iota(I32, (kb, kb), 0)
        c_i = lax.broadcasted_iota(I32, (kb, kb), 1)
        tri = (c_i <= r_i).astype(BF16)
        kc = keys_ref[cur, j]
        eq = kc == tau
        rank = jnp.dot(tri, eq.astype(BF16), preferred_element_type=F32) + run
        sel = (kc > tau) | (eq & (rank <= need))
        bias_ref[slot] = jnp.where(sel, 0.0, NEG_INF).astype(F32)
        return rank[kb - 1:kb, :]

    def bias_plain(j, slot, run):
        bias_ref[slot] = jnp.where(keys_ref[cur, j] >= thr, 0.0, NEG_INF).astype(F32)
        return run

    gs = HEADS_PER_STAGE
    n_stage = N_HEADS // gs

    def logits(jb, bslot, h, r, mx):
        cols = slice((h // 2) * LANES, (h // 2 + 1) * LANES)
        rows = pl.ds(pl.multiple_of(jb * kb + r * sub, sub), sub)
        x = (jnp.dot(k_ref[0, rows, cols], qm_ref[h], preferred_element_type=F32)
             + bias_ref[bslot, r * sub:(r + 1) * sub, :])
        s_ref[(h // gs % 2) * gs + h % gs, r * sub:(r + 1) * sub, :] = x
        return jnp.maximum(mx, _row_fold(x, jnp.maximum))

    run0 = lax.cond(ties, bias_ties, bias_plain, 0, 0, jnp.zeros((1, qb), F32))
    mx0 = [no_max] * gs
    for r in range(n_sub):
        mx0 = [logits(0, 0, g, r, mx0[g]) for g in range(gs)]

    def attn_body(j, carry):
        run, mx = carry
        mx = list(mx)
        jn = jnp.minimum(j + 1, n_ch - 1)
        run = lax.cond(ties, bias_ties, bias_plain, jn, (j + 1) % 2, run)
        for st in range(n_stage):
            heads = [st * gs + g for g in range(gs)]
            m_new, m_safe, alpha = [], [], []
            for g, h in enumerate(heads):
                m_old = m_ref[h]
                m_new.append(jnp.maximum(m_old, jnp.max(mx[g], axis=0, keepdims=True)))
                m_safe.append(jnp.where(m_new[g] == NEG_INF, 0.0, m_new[g]))
                alpha.append(jnp.exp2(m_old - m_safe[g]))
            mx = [no_max] * gs
            for r in range(n_sub):
                tile = slice(r * sub, (r + 1) * sub)
                for g in range(gs):
                    if st + 1 < n_stage:
                        mx[g] = logits(j, j % 2, heads[g] + gs, r, mx[g])
                    else:
                        mx[g] = logits(jn, (j + 1) % 2, g, r, mx[g])
                kk = kk_ref[0, pl.ds(pl.multiple_of(j * kb + r * sub, sub), sub), :]
                part = None
                for h in heads:
                    d = jnp.dot(kk, qim_ref[h], preferred_element_type=F32)
                    term = wi_next[h:h + 1, :] * jnp.maximum(d, 0.0)
                    part = term if part is None else part + term
                sacc_ref[tile, :] = part if st == 0 else sacc_ref[tile, :] + part
                for g in range(gs):
                    p_ref[g, tile, :] = jnp.exp2(
                        s_ref[(st % 2) * gs + g, tile, :] - m_safe[g]).astype(BF16)
            for g, h in enumerate(heads):
                pv = jnp.dot(vt_ref[0, j, h * V_ROWS:(h + 1) * V_ROWS, :], p_ref[g],
                             preferred_element_type=F32)
                l_ref[h] = alpha[g] * l_ref[h] + pv[HEAD_DIM:HEAD_DIM + 1, :]
                acc_ref[h] = alpha[g] * acc_ref[h] + pv[0:HEAD_DIM, :]
                m_ref[h] = m_new[g]
        key = jnp.where(j * kb + krow < limit_next, _sortable(sacc_ref[...]), jnp.int32(INT_MIN))
        keys_ref[nxt, j] = key
        hi_ref[nxt, j] = (key >> 16).astype(I16)
        return run, tuple(mx)

    lax.fori_loop(0, n_ch, attn_body, (run0, tuple(mx0)))

    n_ch_next = jnp.where(i + 1 < n_blocks, ((i + 2) * qb + kb - 1) // kb, n_ch)

    def extra_body(j, carry):
        score_block(j, nxt, wi_next, limit_next)
        return carry

    lax.fori_loop(n_ch, n_ch_next, extra_body, 0)

    outs = [acc_ref[h] / l_ref[h] for h in range(N_HEADS)]
    o_ref[0] = jnp.concatenate(outs, axis=0).T.astype(o_ref.dtype)


def _attn_call(qt, qit, wit, kk, k, vt):
    b, s, _ = k.shape
    qb = min(Q_BLOCK, s)
    kb = min(K_BLOCK, s)
    n_sel = min(TOPK_MAX, s // 4)
    nq = s // qb
    blk_t = lambda n: pl.BlockSpec((1, n, qb), lambda bi, qi_: (bi, 0, qi_))
    nxt_t = lambda n: pl.BlockSpec((1, n, qb), lambda bi, qi_: (bi, 0, jnp.minimum(qi_ + 1, nq - 1)))
    once = pl.Buffered(1)
    kern = functools.partial(_attn_kernel, n_sel=n_sel, qb=qb, kb=kb, sub=min(K_SUB, kb))
    return pl.pallas_call(
        kern,
        grid=(b, nq),
        in_specs=[blk_t(D_ATTN), blk_t(D_ATTN), blk_t(N_IDX_HEADS),
                  nxt_t(D_ATTN), nxt_t(N_IDX_HEADS),
                  pl.BlockSpec((1, s, LANES), lambda bi, qi_: (bi, 0, 0), pipeline_mode=once),
                  pl.BlockSpec((1, s, D_ATTN), lambda bi, qi_: (bi, 0, 0), pipeline_mode=once),
                  pl.BlockSpec((1, s // kb, N_HEADS * V_ROWS, kb), lambda bi, qi_: (bi, 0, 0, 0),
                               pipeline_mode=once)],
        out_specs=pl.BlockSpec((1, qb, D_ATTN), lambda bi, qi_: (bi, qi_, 0)),
        out_shape=jax.ShapeDtypeStruct((b, s, D_ATTN), BF16),
        scratch_shapes=[pltpu.VMEM((2, s // kb, kb, qb), I32),
                        pltpu.VMEM((2, s // kb, kb, qb), I16),
                        pltpu.VMEM((s // kb, kb, qb), I16),
                        pltpu.VMEM((2, kb, qb), F32),
                        pltpu.VMEM((2 * HEADS_PER_STAGE, kb, qb), F32),
                        pltpu.VMEM((HEADS_PER_STAGE, kb, qb), BF16),
                        pltpu.VMEM((kb, qb), F32),
                        pltpu.VMEM((N_HEADS, LANES, qb), BF16),
                        pltpu.VMEM((N_IDX_HEADS, LANES, qb), BF16),
                        pltpu.VMEM((N_HEADS, 1, qb), F32),
                        pltpu.VMEM((N_HEADS, 1, qb), F32),
                        pltpu.VMEM((N_HEADS, HEAD_DIM, qb), F32)],
        compiler_params=_cparams(("parallel", "arbitrary")),
        name="attn",
    )(qt, qit, wit, qit, wit, kk, k, vt)


def _gelu_tanh(x):
    return 0.5 * x * (1.0 + jnp.tanh(math.sqrt(2.0 / math.pi) * (x + 0.044715 * (x * x * x))))


def _rglru_kernel(xr_ref, xg_ref, cw_ref, cb_ref, wa_ref, ba_ref, wx_ref, bx_ref, lam_ref,
                  o_ref, xbuf_ref, h_ref, *, ts):
    t = pl.program_id(1)
    pad = 8

    @pl.when(t == 0)
    def _():
        xbuf_ref[0:pad, :] = jnp.zeros((pad, D_RNN), F32)
        h_ref[...] = jnp.zeros_like(h_ref)

    xr = xr_ref[0]
    xbuf_ref[pad:pad + ts, :] = xr
    conv = cb_ref[...] + cw_ref[CONV_WIDTH - 1:CONV_WIDTH, :] * xr
    for w in range(CONV_WIDTH - 1):
        off = pad - (CONV_WIDTH - 1) + w
        conv = conv + cw_ref[w:w + 1, :] * xbuf_ref[off:off + ts, :]
    xbuf_ref[0:pad, :] = xr[ts - pad:ts, :]

    cb16 = conv.astype(BF16)
    r = jax.nn.sigmoid(jnp.dot(cb16, wa_ref[...], preferred_element_type=F32) + ba_ref[...])
    ig = jax.nn.sigmoid(jnp.dot(cb16, wx_ref[...], preferred_element_type=F32) + bx_ref[...])
    z = -lam_ref[...]
    softplus = jnp.maximum(z, 0.0) + jnp.log1p(jnp.exp(-jnp.abs(z)))
    log_a = (-RG_C) * r * softplus
    a = jnp.exp(log_a)
    bb = jnp.sqrt(1.0 - a * a) * (ig * conv)

    row = lax.broadcasted_iota(I32, (ts, 1), 0)
    d = 1
    while d < ts:
        if d < SUBLANES:
            live = row >= d
            a_s = jnp.where(live, pltpu.roll(a, d, axis=0), 1.0)
            b_s = jnp.where(live, pltpu.roll(bb, d, axis=0), 0.0)
            bb = bb + a * b_s
            a = a * a_s
        else:
            lo_a, hi_a = a[:d, :], a[d:, :]
            bb = jnp.concatenate([bb[:d, :], bb[d:, :] + hi_a * bb[:ts - d, :]], axis=0)
            a = jnp.concatenate([lo_a, hi_a * a[:ts - d, :]], axis=0)
        d *= 2
    h = bb + a * h_ref[0:1, :]
    h_ref[0:1, :] = h[ts - 1:ts, :]
    o_ref[0] = (h * _gelu_tanh(xg_ref[0])).astype(o_ref.dtype)


def _rglru_call(xr, xg, conv_w, conv_b, wa, ba, wx, bx, lam):
    b, s, r = xr.shape
    ts = min(TOKEN_TILE, s)
    tile = pl.BlockSpec((1, ts, r), lambda bi, ti: (bi, ti, 0))
    const = lambda shape: pl.BlockSpec(shape, lambda bi, ti: (0, 0))
    return pl.pallas_call(
        functools.partial(_rglru_kernel, ts=ts),
        grid=(b, s // ts),
        in_specs=[tile, tile, const((CONV_WIDTH, r)), const((1, r)), const((r, r)), const((1, r)),
                  const((r, r)), const((1, r)), const((1, r))],
        out_specs=tile,
        out_shape=jax.ShapeDtypeStruct((b, s, r), BF16),
        scratch_shapes=[pltpu.VMEM((ts + 8, r), F32), pltpu.VMEM((8, r), F32)],
        compiler_params=_cparams(("parallel", "arbitrary")),
        name="rglru",
    )(xr, xg, conv_w, conv_b, wa, ba, wx, bx, lam)


def _route(logits):
    lane = lax.broadcasted_iota(I32, logits.shape, 1)
    is_g = (lane >= N_EXPERTS) & (lane < N_EXPERTS + N_GROUPS)
    big = jnp.int32(LANES)

    def first_lane(mask):
        return jnp.min(jnp.where(mask, lane, big), axis=1, keepdims=True)

    gl = jnp.where(is_g, logits, NEG_INF)
    ge = jnp.exp(gl - jnp.max(gl, axis=1, keepdims=True))
    p_groups = ge / jnp.sum(ge, axis=1, keepdims=True)
    p_g = jnp.max(p_groups, axis=1, keepdims=True)
    g_idx = first_lane(is_g & (p_groups == p_g)) - N_EXPERTS

    in_grp = (lane < N_EXPERTS) & ((lane // EXP_PER_GROUP) == g_idx)
    el = jnp.where(in_grp, logits, NEG_INF)
    ee = jnp.exp(el - jnp.max(el, axis=1, keepdims=True))
    es = ee / jnp.sum(ee, axis=1, keepdims=True)
    es = jnp.where(in_grp, es, NEG_INF)
    v1 = jnp.max(es, axis=1, keepdims=True)
    i1 = first_lane(in_grp & (es == v1))
    es2 = jnp.where(lane == i1, NEG_INF, es)
    v2 = jnp.max(es2, axis=1, keepdims=True)
    i2 = first_lane(in_grp & (es2 == v2))
    tot = v1 + v2
    e_w = jnp.where(lane == i1, v1 / tot, 0.0) + jnp.where(lane == i2, v2 / tot, 0.0)
    return p_g * e_w + jnp.where(lane == ROUTE_GROUP_LANE + g_idx, 1.0, 0.0)


def _out_proj_kernel(ya_ref, yr_ref, x_ref, mod_ref, wo_ref, gpm_ref, gpf_ref, wr_ref, br_ref,
                     x1_ref, h2_ref, comb_ref):
    mix = (jnp.dot(ya_ref[0], wo_ref[0:D_ATTN, :], preferred_element_type=F32)
           + jnp.dot(yr_ref[0], wo_ref[D_ATTN:D_ATTN + D_RNN, :], preferred_element_type=F32))
    gate1 = mod_ref[0, 2:3, :]
    shift2 = mod_ref[0, 3:4, :]
    scale2 = mod_ref[0, 4:5, :]
    x1 = x_ref[0] + gate1 * _rms(mix, gpm_ref[...])
    x1_ref[0] = x1
    h2 = _rms(x1, gpf_ref[...]) * (1.0 + scale2) + shift2
    h_hi = h2.astype(BF16)
    h2_ref[0] = h_hi
    h_lo = (h2 - h_hi.astype(F32)).astype(BF16)
    part = jnp.dot(h_hi, wr_ref[...], preferred_element_type=F32)
    logits = (part[:, 0:LANES] + (part[:, LANES:2 * LANES]
              + jnp.dot(h_lo, wr_ref[:, 0:LANES], preferred_element_type=F32))) + br_ref[...]
    comb_ref[0] = _route(logits)


def _out_proj_call(ya, yr, x, mod, wo, gpm, gpf, wr, br):
    b, s, d = x.shape
    tm = min(TOKEN_TILE, s)
    tile = lambda n: pl.BlockSpec((1, tm, n), lambda bi, ti: (bi, ti, 0))
    const = lambda shape: pl.BlockSpec(shape, lambda bi, ti: (0,) * len(shape))
    return pl.pallas_call(
        _out_proj_kernel,
        grid=(b, s // tm),
        in_specs=[tile(D_ATTN), tile(D_RNN), tile(d),
                  pl.BlockSpec((1, N_MOD, d), lambda bi, ti: (bi, 0, 0)),
                  const((D_ATTN + D_RNN, d)), const((1, d)), const((1, d)),
                  const((d, 2 * LANES)), const((1, LANES))],
        out_specs=[tile(d), tile(d), tile(LANES)],
        out_shape=[jax.ShapeDtypeStruct((b, s, d), F32), jax.ShapeDtypeStruct((b, s, d), BF16),
                   jax.ShapeDtypeStruct((b, s, LANES), F32)],
        compiler_params=_cparams(("parallel", "parallel")),
        name="out_proj",
    )(ya, yr, x, mod, wo, gpm, gpf, wr, br)


def _moe_kernel(h2_ref, comb_ref, x1_ref, mod_ref, wg_ref, wu_ref, wd_ref, gpo_ref,
                o_ref, hs_ref, cs_ref, ys_ref, act_ref, slot_ref, seg_ref, *, tm, ns, rb):
    g = pl.program_id(2)
    lane = lax.broadcasted_iota(I32, (1, LANES), 1)
    n_blk = tm // LANES

    @pl.when(g == 0)
    def _dispatch():
        comb = comb_ref[0]
        oh = jnp.where((lane >= ROUTE_GROUP_LANE) & (lane < ROUTE_GROUP_LANE + N_GROUPS), comb, 0.0)
        oh16 = oh.astype(BF16)
        sub = lax.broadcasted_iota(I32, (LANES, 1), 0)
        base = jnp.int32(0)
        base_lane = jnp.zeros((1, LANES), F32)
        base_sub = jnp.zeros((LANES, 1), F32)
        for gg in range(N_GROUPS):
            here = ROUTE_GROUP_LANE + gg
            n = jnp.sum(jnp.where(lane == here, oh, 0.0)).astype(I32)
            nb = (n + rb - 1) // rb
            seg_ref[gg] = base
            seg_ref[N_GROUPS + gg] = nb
            base_f = base.astype(F32)
            base_lane = base_lane + jnp.where(lane == here, base_f, 0.0)
            base_sub = base_sub + jnp.where(sub == here, base_f, 0.0)
            base = base + nb * rb

        tok_c = lax.broadcasted_iota(I32, (1, tm), 1)
        tok_r = lax.broadcasted_iota(I32, (tm, 1), 0)
        blk_r = lax.broadcasted_iota(I32, (LANES, 1), 0)
        blk_c = lax.broadcasted_iota(I32, (1, LANES), 1)
        eye = (blk_r == blk_c).astype(BF16)
        oht16 = _nt_dot(eye, oh16).astype(BF16)
        slot_row = []
        for t in range(n_blk):
            tril = (tok_c <= blk_r + t * LANES).astype(BF16)
            rank = jnp.dot(tril, oh16, preferred_element_type=F32)
            oh_blk = oh[t * LANES:(t + 1) * LANES, :]
            slot_ref[t * LANES:(t + 1) * LANES, :] = jnp.sum(
                oh_blk * (base_lane + rank - 1.0), axis=1, keepdims=True).astype(I32)
            triu = (tok_r <= blk_c + t * LANES).astype(BF16)
            rank_t = jnp.dot(oht16, triu, preferred_element_type=F32)
            oht_blk = oht16[:, t * LANES:(t + 1) * LANES].astype(F32)
            slot_row.append(jnp.sum(oht_blk * (base_sub + rank_t - 1.0), axis=0,
                                    keepdims=True).astype(I32))
        slot_row = jnp.concatenate(slot_row, axis=1)

        hi = comb.astype(BF16)
        lo = (comb - hi.astype(F32)).astype(BF16)
        cw = jnp.concatenate([hi, lo], axis=1)
        h = h2_ref[0]
        for r in range(ns // LANES):
            rows = slice(r * LANES, (r + 1) * LANES)
            p = (blk_r + r * LANES == slot_row).astype(BF16)
            hs_ref[rows, :] = jnp.dot(p, h, preferred_element_type=F32).astype(BF16)
            cs_ref[rows, :] = jnp.dot(p, cw, preferred_element_type=F32).astype(BF16)
        ys_ref[...] = jnp.zeros_like(ys_ref)

    n_act = EXP_PER_GROUP * D_EXPERT
    r_i = lax.broadcasted_iota(I32, (2 * LANES, n_act), 0) % LANES
    c_i = lax.broadcasted_iota(I32, (2 * LANES, n_act), 1) // D_EXPERT
    expand = (r_i == g * EXP_PER_GROUP + c_i).astype(BF16)
    base = seg_ref[g]

    def block(r, carry):
        rows = pl.ds(pl.multiple_of(base + r * rb, rb), rb)
        hb = hs_ref[rows, :]
        cexp = jnp.dot(cs_ref[rows, :], expand, preferred_element_type=F32)
        for e in range(EXP_PER_GROUP):
            gt = jnp.dot(hb, wg_ref[0, e], preferred_element_type=F32)
            up = jnp.dot(hb, wu_ref[0, e], preferred_element_type=F32)
            cols = slice(e * D_EXPERT, (e + 1) * D_EXPERT)
            act_ref[:, cols] = (gt * jax.nn.sigmoid(gt) * up * cexp[:, cols]).astype(BF16)
        ys_ref[rows, :] = jnp.dot(act_ref[...], wd_ref[0], preferred_element_type=F32).astype(BF16)
        return carry

    lax.fori_loop(0, seg_ref[N_GROUPS + g], block, 0)

    @pl.when(g == N_GROUPS - 1)
    def _combine():
        gate2 = mod_ref[0, 5:6, :]
        slot_c = lax.broadcasted_iota(I32, (1, ns), 1)
        half = tm // 2
        for hh in range(2):
            rows = slice(hh * half, (hh + 1) * half)
            pt = (slot_c == slot_ref[rows, :]).astype(BF16)
            y = jnp.dot(pt, ys_ref[...], preferred_element_type=F32)
            o_ref[0, rows, :] = x1_ref[0, rows, :] + gate2 * _rms(y, gpo_ref[...])


def _moe_call(h2, comb, x1, mod, wg, wu, wd, gpo):
    b, s, d = x1.shape
    tm = min(MOE_TILE, s)
    rb = MOE_ROWS
    ns = -(-(tm + N_GROUPS * rb) // LANES) * LANES
    tile = lambda n, **kw: pl.BlockSpec((1, tm, n), lambda bi, ti, gi: (bi, ti, 0), **kw)
    kern = functools.partial(_moe_kernel, tm=tm, ns=ns, rb=rb)
    return pl.pallas_call(
        kern,
        grid=(b, s // tm, N_GROUPS),
        in_specs=[tile(d), tile(LANES), tile(d, pipeline_mode=pl.Buffered(1)),
                  pl.BlockSpec((1, N_MOD, d), lambda bi, ti, gi: (bi, 0, 0)),
                  pl.BlockSpec((1, EXP_PER_GROUP, d, D_EXPERT), lambda bi, ti, gi: (gi, 0, 0, 0)),
                  pl.BlockSpec((1, EXP_PER_GROUP, d, D_EXPERT), lambda bi, ti, gi: (gi, 0, 0, 0)),
                  pl.BlockSpec((1, EXP_PER_GROUP * D_EXPERT, d), lambda bi, ti, gi: (gi, 0, 0)),
                  pl.BlockSpec((1, d), lambda bi, ti, gi: (0, 0))],
        out_specs=tile(d),
        out_shape=jax.ShapeDtypeStruct((b, s, d), F32),
        scratch_shapes=[pltpu.VMEM((ns, d), BF16),
                        pltpu.VMEM((ns, 2 * LANES), BF16),
                        pltpu.VMEM((ns, d), BF16),
                        pltpu.VMEM((rb, EXP_PER_GROUP * D_EXPERT), BF16),
                        pltpu.VMEM((tm, 1), I32),
                        pltpu.SMEM((2 * N_GROUPS,), I32)],
        compiler_params=_cparams(("parallel", "parallel", "arbitrary")),
        name="moe",
    )(h2, comb, x1, mod, wg, wu, wd, gpo)


def _rope_tables(seq_len):
    pos = jnp.arange(seq_len, dtype=F32)
    inv = ROPE_THETA ** (-jnp.arange(0, HEAD_DIM, 2, dtype=F32) / HEAD_DIM)
    ang = pos[:, None] * inv[None, :]
    cos, sin = jnp.cos(ang), jnp.sin(ang)
    cos2 = jnp.concatenate([cos, cos, cos, cos], axis=1)
    sin2 = jnp.concatenate([-sin, sin, -sin, sin], axis=1)
    return cos2, sin2, cos2[:, :HEAD_DIM].T, sin2[:, :HEAD_DIM].T


def _block_diag(w):
    n, c, d = w.shape
    eye = jnp.eye(n, dtype=w.dtype)
    return (eye[:, None, :, None] * w[:, :, None, :]).reshape(n * c, n * d)


def _pad_cols(w, n):
    return jnp.pad(w, ((0, 0), (0, n - w.shape[1])))


def kernel(x, c, w_ada, b_ada, g_pre_mix, g_post_mix, g_pre_ffn, g_post_ffn, w_in, conv_w, conv_b, w_rg_a, b_rg_a, w_rg_x, b_rg_x, lru_lambda, w_out, w_router_group, b_router_group, w_router_expert, b_router_expert, w_gate, w_up, w_down):
    b, s, d = x.shape
    depth = w_ada.shape[0]
    cos2, sin2, cos_t, sin_t = _rope_tables(s)
    c_pad = jnp.pad(c, ((0, (-b) % SUBLANES), (0, 0)))
    o_q, o_k, o_v, o_qi, o_ki, o_wi, o_xr, o_xg = 0, 512, 1024, 1536, 2048, 2112, 2120, 2632
    for l in range(depth):
        wl = w_in[l]
        w_ki = wl[:, o_ki:o_wi]
        w_nat = jnp.concatenate([wl[:, o_k:o_v], w_ki, w_ki, wl[:, o_xr:o_xg], wl[:, o_xg:]],
                                axis=1).astype(BF16)
        w_tr = jnp.concatenate([wl[:, o_q:o_k], wl[:, o_qi:o_ki], wl[:, o_v:o_qi],
                                _pad_cols(wl[:, o_wi:o_xr], _R_END - _R_WI)], axis=1).T.astype(BF16)
        w_route = _pad_cols(jnp.concatenate([w_router_expert[l], w_router_group[l]], axis=1), LANES)
        w_route_hi = w_route.astype(BF16)
        w_route = jnp.concatenate(
            [w_route_hi, (w_route - w_route_hi.astype(F32)).astype(BF16)], axis=1)
        b_route = _pad_cols(jnp.concatenate([b_router_expert[l], b_router_group[l]])[None, :], LANES)

        mod = _mod_call(c_pad, w_ada[l], b_ada[l][None, :])[:b].reshape(b, N_MOD, d)
        qt, qit, wit, k, kk, vt, xr, xg = _in_proj_call(
            x, mod, g_pre_mix[l][None, :], w_nat, w_tr, cos2, sin2, cos_t, sin_t)
        y_attn = _attn_call(qt, qit, wit, kk, k, vt)
        y_rnn = _rglru_call(xr, xg, conv_w[l], conv_b[l][None, :],
                            _block_diag(w_rg_a[l]).astype(BF16), b_rg_a[l][None, :],
                            _block_diag(w_rg_x[l]).astype(BF16), b_rg_x[l][None, :],
                            lru_lambda[l][None, :])
        x1, h2, comb = _out_proj_call(y_attn, y_rnn, x, mod, w_out[l].astype(BF16),
                                      g_post_mix[l][None, :], g_pre_ffn[l][None, :], w_route, b_route)
        x = _moe_call(h2, comb, x1, mod, w_gate[l].astype(BF16), w_up[l].astype(BF16),
                      w_down[l].reshape(N_GROUPS, EXP_PER_GROUP * D_EXPERT, d).astype(BF16),
                      g_post_ffn[l][None, :])
    return x
```

```python
import functools
import math

import jax
import jax.numpy as jnp
from jax import lax
from jax.experimental import pallas as pl
from jax.experimental.pallas import tpu as pltpu

F32 = jnp.float32
BF16 = jnp.bfloat16
I32 = jnp.int32
I16 = jnp.int16

D_MODEL = 1024
CHUNK = 64
ROPE_THETA = 10000.0
EPS = 1e-6
N_HEADS = 8
HEAD_DIM = 64
D_ATTN = N_HEADS * HEAD_DIM
N_IDX_HEADS = 8
IDX_DIM = 64
TOPK_MAX = 256
D_RNN = 512
N_RNN_BLOCKS = 8
CONV_WIDTH = 4
RG_C = 8.0
N_GROUPS = 4
EXP_PER_GROUP = 8
N_EXPERTS = N_GROUPS * EXP_PER_GROUP
D_EXPERT = 256
N_MOD = 6

LANES = 128
SUBLANES = 8
INT_MIN = -2 ** 31
INT16_MIN = -2 ** 15
NEG_INF = float("-inf")

Q_BLOCK = 256
K_BLOCK = 512
K_SUB = 256
HEADS_PER_STAGE = 2
V_ROWS = HEAD_DIM + 16
TOKEN_TILE = 512
MOE_TILE = 1024
MOE_ROWS = 128
ROUTE_GROUP_LANE = 64
VMEM_LIMIT = 60 * 1024 * 1024


def _cparams(sem):
    return pltpu.CompilerParams(dimension_semantics=sem, vmem_limit_bytes=VMEM_LIMIT)


def _nt_dot(a, b):
    return lax.dot_general(a, b, (((1,), (1,)), ((), ())), preferred_element_type=F32)


def _rms(x, g):
    return x * lax.rsqrt(jnp.mean(x * x, axis=-1, keepdims=True) + EPS) * g


def _mod_kernel(c_ref, w_ref, b_ref, o_ref):
    c = c_ref[...]
    sc = c * jax.nn.sigmoid(c)
    o_ref[...] = jnp.dot(sc, w_ref[...], preferred_element_type=F32,
                         precision=lax.Precision.HIGHEST) + b_ref[...]


def _mod_call(c_pad, w_ada, b_ada):
    rows, d = c_pad.shape
    n = w_ada.shape[1]
    bn = 1024
    return pl.pallas_call(
        _mod_kernel,
        grid=(n // bn,),
        in_specs=[pl.BlockSpec((rows, d), lambda j: (0, 0)),
                  pl.BlockSpec((d, bn), lambda j: (0, j)),
                  pl.BlockSpec((1, bn), lambda j: (0, j))],
        out_specs=pl.BlockSpec((rows, bn), lambda j: (0, j)),
        out_shape=jax.ShapeDtypeStruct((rows, n), F32),
        compiler_params=_cparams(("arbitrary",)),
        name="mod",
    )(c_pad, w_ada, b_ada)


_R_Q, _R_QI, _R_V, _R_WI, _R_END = 0, 512, 1024, 1536, 1552
_C_K, _C_KK, _C_XR, _C_XG, _C_END = 0, 512, 640, 1152, 1664


def _rope(y, cos, sin):
    lane = lax.broadcasted_iota(I32, (1, LANES), 1)
    low = (lane % HEAD_DIM) < (HEAD_DIM // 2)
    outs = []
    for p in range(y.shape[1] // LANES):
        s = y[:, p * LANES:(p + 1) * LANES]
        swapped = jnp.where(low, pltpu.roll(s, LANES - HEAD_DIM // 2, axis=1),
                            pltpu.roll(s, HEAD_DIM // 2, axis=1))
        outs.append(s * cos + swapped * sin)
    return outs[0] if len(outs) == 1 else jnp.concatenate(outs, axis=1)


def _rope_t(y, cos, sin):
    half = HEAD_DIM // 2
    outs = []
    for hd in range(y.shape[0] // HEAD_DIM):
        blk = y[hd * HEAD_DIM:(hd + 1) * HEAD_DIM, :]
        swapped = jnp.concatenate([blk[half:, :], blk[:half, :]], axis=0)
        outs.append(blk * cos + swapped * sin)
    return jnp.concatenate(outs, axis=0)


def _in_proj_kernel(x_ref, mod_ref, g_ref, wn_ref, wt_ref, cos_ref, sin_ref, cost_ref, sint_ref,
                    qt_ref, qit_ref, wit_ref, k_ref, kk_ref, vt_ref, xr_ref, xg_ref):
    x = x_ref[0]
    shift = mod_ref[0, 0:1, :]
    scale = mod_ref[0, 1:2, :]
    h = (_rms(x, g_ref[...]) * (1.0 + scale) + shift).astype(BF16)
    cos, sin = cos_ref[...], sin_ref[...]
    cos_t, sin_t = cost_ref[...], sint_ref[...]

    def proj(a, b):
        return jnp.dot(h, wn_ref[:, a:b], preferred_element_type=F32)

    def proj_t(a, b):
        return _nt_dot(wt_ref[a:b, :], h)

    qt_ref[0] = (_rope_t(proj_t(_R_Q, _R_QI), cos_t, sin_t)
                 * (HEAD_DIM ** -0.5 * math.log2(math.e))).astype(BF16)
    qit_ref[0] = (_rope_t(proj_t(_R_QI, _R_V), cos_t, sin_t) * (IDX_DIM ** -0.5)).astype(BF16)
    vt = proj_t(_R_V, _R_WI).astype(BF16)
    ones = jnp.ones((V_ROWS - HEAD_DIM, vt.shape[1]), BF16)
    for hd in range(N_HEADS):
        vt_ref[0, 0, hd * V_ROWS:hd * V_ROWS + HEAD_DIM, :] = vt[hd * HEAD_DIM:(hd + 1) * HEAD_DIM, :]
        vt_ref[0, 0, hd * V_ROWS + HEAD_DIM:(hd + 1) * V_ROWS, :] = ones
    wit_ref[0] = proj_t(_R_WI, _R_END)[0:N_IDX_HEADS, :] * (N_IDX_HEADS ** -0.5)
    k_ref[0] = _rope(proj(_C_K, _C_KK), cos, sin).astype(BF16)
    kk_ref[0] = _rope(proj(_C_KK, _C_XR), cos, sin).astype(BF16)
    xr_ref[0] = proj(_C_XR, _C_XG)
    xg_ref[0] = proj(_C_XG, _C_END)


def _in_proj_call(x, mod, g, w_nat, w_tr, cos2, sin2, cos_t, sin_t):
    b, s, d = x.shape
    tm = min(K_BLOCK, s)
    nt = s // tm
    tile = lambda n: pl.BlockSpec((1, tm, n), lambda bi, ti: (bi, ti, 0))
    tile_t = lambda n: pl.BlockSpec((1, n, tm), lambda bi, ti: (bi, 0, ti))
    const = lambda shape: pl.BlockSpec(shape, lambda bi, ti: (0, 0))
    shp = lambda n, dt: jax.ShapeDtypeStruct((b, s, n), dt)
    shp_t = lambda n, dt: jax.ShapeDtypeStruct((b, n, s), dt)
    return pl.pallas_call(
        _in_proj_kernel,
        grid=(b, nt),
        in_specs=[tile(d),
                  pl.BlockSpec((1, N_MOD, d), lambda bi, ti: (bi, 0, 0)),
                  const((1, d)), const((d, _C_END)), const((_R_END, d)),
                  pl.BlockSpec((tm, LANES), lambda bi, ti: (ti, 0)),
                  pl.BlockSpec((tm, LANES), lambda bi, ti: (ti, 0)),
                  pl.BlockSpec((HEAD_DIM, tm), lambda bi, ti: (0, ti)),
                  pl.BlockSpec((HEAD_DIM, tm), lambda bi, ti: (0, ti))],
        out_specs=[tile_t(D_ATTN), tile_t(D_ATTN), tile_t(N_IDX_HEADS), tile(D_ATTN), tile(LANES),
                   pl.BlockSpec((1, 1, N_HEADS * V_ROWS, tm), lambda bi, ti: (bi, ti, 0, 0)),
                   tile(D_RNN), tile(D_RNN)],
        out_shape=[shp_t(D_ATTN, BF16), shp_t(D_ATTN, BF16), shp_t(N_IDX_HEADS, F32),
                   shp(D_ATTN, BF16), shp(LANES, BF16),
                   jax.ShapeDtypeStruct((b, nt, N_HEADS * V_ROWS, tm), BF16),
                   shp(D_RNN, F32), shp(D_RNN, F32)],
        compiler_params=_cparams(("parallel", "parallel")),
        name="in_proj",
    )(x, mod, g, w_nat, w_tr, cos2, sin2, cos_t, sin_t)


def _sortable(score):
    bits = pltpu.bitcast(score, I32)
    return jnp.where(bits < 0, jnp.int32(INT_MIN) - bits, bits)


def _tile_fold(x, op, rows):
    parts = [x[t * rows:(t + 1) * rows, :] for t in range(x.shape[0] // rows)]
    while len(parts) > 1:
        nxt = [op(parts[t], parts[t + 1]) for t in range(0, len(parts) - 1, 2)]
        if len(parts) % 2:
            nxt.append(parts[-1])
        parts = nxt
    return parts[0]


def _row_fold(x, op):
    acc = x[0:SUBLANES, :]
    for t in range(1, x.shape[0] // SUBLANES):
        acc = op(acc, x[t * SUBLANES:(t + 1) * SUBLANES, :])
    return acc


def _fold16(x):
    return _tile_fold(x, jnp.add, 2 * SUBLANES)


def _attn_kernel(qt_ref, qit_ref, wit_ref, kk_ref, k_ref, vt_ref, o_ref,
                 keys_ref, hi_ref, lo_ref, bias_ref, s_ref, p_ref, qm_ref, qim_ref, m_ref, l_ref,
                 acc_ref, *, n_sel, qb, kb, sub):
    i = pl.program_id(1)
    n_ch = ((i + 1) * qb + kb - 1) // kb
    q_pos = i * qb + lax.broadcasted_iota(I32, (1, qb), 1)
    limit = (q_pos // CHUNK + 1) * CHUNK
    krow = lax.broadcasted_iota(I32, (kb, 1), 0)
    row128 = lax.broadcasted_iota(I32, (LANES, 1), 0)

    for h in range(N_HEADS):
        rows = slice((h // 2) * LANES, (h // 2 + 1) * LANES)
        keep = (row128 < HEAD_DIM) if h % 2 == 0 else (row128 >= HEAD_DIM)
        qs, qis = qt_ref[0, rows, :], qit_ref[0, rows, :]
        qm_ref[h] = jnp.where(keep, qs, jnp.zeros_like(qs))
        qim_ref[h] = jnp.where(keep, qis, jnp.zeros_like(qis))
    m_ref[...] = jnp.full(m_ref.shape, NEG_INF, F32)
    l_ref[...] = jnp.zeros(l_ref.shape, F32)
    acc_ref[...] = jnp.zeros(acc_ref.shape, F32)

    wi = wit_ref[0]

    def score_body(j, carry, masked):
        kk = kk_ref[0, pl.ds(pl.multiple_of(j * kb, kb), kb), :]
        acc = jnp.zeros((kb, qb), F32)
        for h in range(N_IDX_HEADS):
            d = jnp.dot(kk, qim_ref[h], preferred_element_type=F32)
            acc = acc + wi[h:h + 1, :] * jnp.maximum(d, 0.0)
        key = _sortable(acc)
        if masked:
            key = jnp.where(j * kb + krow < limit, key, jnp.int32(INT_MIN))
        keys_ref[j] = key
        hi_ref[j] = (key >> 16).astype(I16)
        return carry

    n_open = (i * qb + CHUNK) // kb
    lax.fori_loop(0, n_open, functools.partial(score_body, masked=False), 0)
    lax.fori_loop(n_open, n_ch, functools.partial(score_body, masked=True), 0)

    def count16(ref, pred):
        def one(j):
            return _fold16(jnp.where(pred(ref[j]), jnp.int16(1), jnp.int16(0)))

        def body(jj, acc):
            return acc + (one(2 * jj) + one(2 * jj + 1))

        acc = lax.fori_loop(0, n_ch // 2, body, jnp.zeros((2 * SUBLANES, qb), I16))
        acc = lax.cond(n_ch % 2 == 1, lambda a: a + one(n_ch - 1), lambda a: a, acc)
        return jnp.sum(acc.astype(I32), axis=0, keepdims=True)

    def search16(ref, target):
        c0 = count16(ref, lambda v: v >= jnp.int16(0))
        ok = c0 >= target
        init = (jnp.where(ok, jnp.int32(0), jnp.int32(INT16_MIN)),
                jnp.where(ok, c0, n_ch * kb), jnp.where(ok, 0, c0))

        def bit_body(bi, carry):
            t, n_ge, n_gt = carry
            cand = t | (jnp.int32(1) << (14 - bi))
            c = count16(ref, lambda v: v >= cand.astype(I16))
            ok = c >= target
            return jnp.where(ok, cand, t), jnp.where(ok, c, n_ge), jnp.where(ok, n_gt, c)

        return lax.fori_loop(0, 15, bit_body, init)

    tau_hi, _, n_gt_hi = search16(hi_ref, n_sel)
    tau_hi16 = tau_hi.astype(I16)

    def lo_body(j, carry):
        lo = (keys_ref[j] ^ jnp.int32(0x8000)).astype(I16)
        lo_ref[j] = jnp.where(hi_ref[j] == tau_hi16, lo, jnp.int16(INT16_MIN))
        return carry

    lax.fori_loop(0, n_ch, lo_body, 0)
    tau_lo, n_ge_lo, n_gt_lo = search16(lo_ref, n_sel - n_gt_hi)
    tau = (tau_hi << 16) | ((tau_lo + 32768) & jnp.int32(0xFFFF))
    all_sel = tau == jnp.int32(INT_MIN)
    need_i = jnp.where(all_sel, 0, n_sel - n_gt_hi - n_gt_lo)
    need = need_i.astype(F32)
    ties = jnp.max(jnp.where(all_sel, 0, (n_ge_lo - n_gt_lo) - need_i)) > 0
    thr = jnp.where(all_sel, jnp.int32(INT_MIN + 1), tau)

    n_sub = kb // sub
    no_max = jnp.full((SUBLANES, qb), NEG_INF, F32)

    def bias_ties(j, slot, run):
        r_i = lax.broadcasted_iota(I32, (kb, kb), 0)
        c_i = lax.broadcasted_iota(I32, (kb, kb), 1)
        tri = (c_i <= r_i).astype(BF16)
        kc = keys_ref[j]
        eq = kc == tau
        rank = jnp.dot(tri, eq.astype(BF16), preferred_element_type=F32) + run
        sel = (kc > tau) | (eq & (rank <= need))
        bias_ref[slot] = jnp.where(sel, 0.0, NEG_INF).astype(F32)
        return rank[kb - 1:kb, :]

    def bias_plain(j, slot, run):
        bias_ref[slot] = jnp.where(keys_ref[j] >= thr, 0.0, NEG_INF).astype(F32)
        return run

    gs = HEADS_PER_STAGE
    n_stage = N_HEADS // gs

    def logits(jb, bslot, h, r, mx):
        cols = slice((h // 2) * LANES, (h // 2 + 1) * LANES)
        rows = pl.ds(pl.multiple_of(jb * kb + r * sub, sub), sub)
        x = (jnp.dot(k_ref[0, rows, cols], qm_ref[h], preferred_element_type=F32)
             + bias_ref[bslot, r * sub:(r + 1) * sub, :])
        s_ref[(h // gs % 2) * gs + h % gs, r * sub:(r + 1) * sub, :] = x
        return jnp.maximum(mx, _row_fold(x, jnp.maximum))

    run0 = lax.cond(ties, bias_ties, bias_plain, 0, 0, jnp.zeros((1, qb), F32))
    mx0 = [no_max] * gs
    for r in range(n_sub):
        mx0 = [logits(0, 0, g, r, mx0[g]) for g in range(gs)]

    def attn_body(j, carry):
        run, mx = carry
        mx = list(mx)
        jn = jnp.minimum(j + 1, n_ch - 1)
        run = lax.cond(ties, bias_ties, bias_plain, jn, (j + 1) % 2, run)
        for st in range(n_stage):
            heads = [st * gs + g for g in range(gs)]
            m_new, m_safe, alpha = [], [], []
            for g, h in enumerate(heads):
                m_old = m_ref[h]
                m_new.append(jnp.maximum(m_old, jnp.max(mx[g], axis=0, keepdims=True)))
                m_safe.append(jnp.where(m_new[g] == NEG_INF, 0.0, m_new[g]))
                alpha.append(jnp.exp2(m_old - m_safe[g]))
            mx = [no_max] * gs
            for r in range(n_sub):
                tile = slice(r * sub, (r + 1) * sub)
                for g in range(gs):
                    if st + 1 < n_stage:
                        mx[g] = logits(j, j % 2, heads[g] + gs, r, mx[g])
                    else:
                        mx[g] = logits(jn, (j + 1) % 2, g, r, mx[g])
                for g in range(gs):
                    p_ref[g, tile, :] = jnp.exp2(
                        s_ref[(st % 2) * gs + g, tile, :] - m_safe[g]).astype(BF16)
            for g, h in enumerate(heads):
                pv = jnp.dot(vt_ref[0, j, h * V_ROWS:(h + 1) * V_ROWS, :], p_ref[g],
                             preferred_element_type=F32)
                l_ref[h] = alpha[g] * l_ref[h] + pv[HEAD_DIM:HEAD_DIM + 1, :]
                acc_ref[h] = alpha[g] * acc_ref[h] + pv[0:HEAD_DIM, :]
                m_ref[h] = m_new[g]
        return run, tuple(mx)

    lax.fori_loop(0, n_ch, attn_body, (run0, tuple(mx0)))

    outs = [acc_ref[h] / l_ref[h] for h in range(N_HEADS)]
    o_ref[0] = jnp.concatenate(outs, axis=0).T.astype(o_ref.dtype)


def _attn_call(qt, qit, wit, kk, k, vt):
    b, s, _ = k.shape
    qb = min(Q_BLOCK, s)
    kb = min(K_BLOCK, s)
    n_sel = min(TOPK_MAX, s // 4)
    blk_t = lambda n: pl.BlockSpec((1, n, qb), lambda bi, qi_: (bi, 0, qi_))
    once = pl.Buffered(1)
    kern = functools.partial(_attn_kernel, n_sel=n_sel, qb=qb, kb=kb, sub=min(K_SUB, kb))
    return pl.pallas_call(
        kern,
        grid=(b, s // qb),
        in_specs=[blk_t(D_ATTN), blk_t(D_ATTN), blk_t(N_IDX_HEADS),
                  pl.BlockSpec((1, s, LANES), lambda bi, qi_: (bi, 0, 0), pipeline_mode=once),
                  pl.BlockSpec((1, s, D_ATTN), lambda bi, qi_: (bi, 0, 0), pipeline_mode=once),
                  pl.BlockSpec((1, s // kb, N_HEADS * V_ROWS, kb), lambda bi, qi_: (bi, 0, 0, 0),
                               pipeline_mode=once)],
        out_specs=pl.BlockSpec((1, qb, D_ATTN), lambda bi, qi_: (bi, qi_, 0)),
        out_shape=jax.ShapeDtypeStruct((b, s, D_ATTN), BF16),
        scratch_shapes=[pltpu.VMEM((s // kb, kb, qb), I32),
                        pltpu.VMEM((s // kb, kb, qb), I16),
                        pltpu.VMEM((s // kb, kb, qb), I16),
                        pltpu.VMEM((2, kb, qb), F32),
                        pltpu.VMEM((2 * HEADS_PER_STAGE, kb, qb), F32),
                        pltpu.VMEM((HEADS_PER_STAGE, kb, qb), BF16),
                        pltpu.VMEM((N_HEADS, LANES, qb), BF16),
                        pltpu.VMEM((N_IDX_HEADS, LANES, qb), BF16),
                        pltpu.VMEM((N_HEADS, 1, qb), F32),
                        pltpu.VMEM((N_HEADS, 1, qb), F32),
                        pltpu.VMEM((N_HEADS, HEAD_DIM, qb), F32)],
        compiler_params=_cparams(("parallel", "arbitrary")),
        name="attn",
    )(qt, qit, wit, kk, k, vt)


def _gelu_tanh(x):
    return 0.5 * x * (1.0 + jnp.tanh(math.sqrt(2.0 / math.pi) * (x + 0.044715 * (x * x * x))))


def _rglru_kernel(xr_ref, xg_ref, cw_ref, cb_ref, wa_ref, ba_ref, wx_ref, bx_ref, lam_ref,
                  o_ref, xbuf_ref, h_ref, *, ts):
    t = pl.program_id(1)
    pad = 8

    @pl.when(t == 0)
    def _():
        xbuf_ref[0:pad, :] = jnp.zeros((pad, D_RNN), F32)
        h_ref[...] = jnp.zeros_like(h_ref)

    xr = xr_ref[0]
    xbuf_ref[pad:pad + ts, :] = xr
    conv = cb_ref[...] + cw_ref[CONV_WIDTH - 1:CONV_WIDTH, :] * xr
    for w in range(CONV_WIDTH - 1):
        off = pad - (CONV_WIDTH - 1) + w
        conv = conv + cw_ref[w:w + 1, :] * xbuf_ref[off:off + ts, :]
    xbuf_ref[0:pad, :] = xr[ts - pad:ts, :]

    cb16 = conv.astype(BF16)
    r = jax.nn.sigmoid(jnp.dot(cb16, wa_ref[...], preferred_element_type=F32) + ba_ref[...])
    ig = jax.nn.sigmoid(jnp.dot(cb16, wx_ref[...], preferred_element_type=F32) + bx_ref[...])
    z = -lam_ref[...]
    softplus = jnp.maximum(z, 0.0) + jnp.log1p(jnp.exp(-jnp.abs(z)))
    log_a = (-RG_C) * r * softplus
    a = jnp.exp(log_a)
    bb = jnp.sqrt(1.0 - a * a) * (ig * conv)

    row = lax.broadcasted_iota(I32, (ts, 1), 0)
    d = 1
    while d < ts:
        if d < SUBLANES:
            live = row >= d
            a_s = jnp.where(live, pltpu.roll(a, d, axis=0), 1.0)
            b_s = jnp.where(live, pltpu.roll(bb, d, axis=0), 0.0)
            bb = bb + a * b_s
            a = a * a_s
        else:
            lo_a, hi_a = a[:d, :], a[d:, :]
            bb = jnp.concatenate([bb[:d, :], bb[d:, :] + hi_a * bb[:ts - d, :]], axis=0)
            a = jnp.concatenate([lo_a, hi_a * a[:ts - d, :]], axis=0)
        d *= 2
    h = bb + a * h_ref[0:1, :]
    h_ref[0:1, :] = h[ts - 1:ts, :]
    o_ref[0] = (h * _gelu_tanh(xg_ref[0])).astype(o_ref.dtype)


def _rglru_call(xr, xg, conv_w, conv_b, wa, ba, wx, bx, lam):
    b, s, r = xr.shape
    ts = min(TOKEN_TILE, s)
    tile = pl.BlockSpec((1, ts, r), lambda bi, ti: (bi, ti, 0))
    const = lambda shape: pl.BlockSpec(shape, lambda bi, ti: (0, 0))
    return pl.pallas_call(
        functools.partial(_rglru_kernel, ts=ts),
        grid=(b, s // ts),
        in_specs=[tile, tile, const((CONV_WIDTH, r)), const((1, r)), const((r, r)), const((1, r)),
                  const((r, r)), const((1, r)), const((1, r))],
        out_specs=tile,
        out_shape=jax.ShapeDtypeStruct((b, s, r), BF16),
        scratch_shapes=[pltpu.VMEM((ts + 8, r), F32), pltpu.VMEM((8, r), F32)],
        compiler_params=_cparams(("parallel", "arbitrary")),
        name="rglru",
    )(xr, xg, conv_w, conv_b, wa, ba, wx, bx, lam)


def _route(logits):
    lane = lax.broadcasted_iota(I32, logits.shape, 1)
    is_g = (lane >= N_EXPERTS) & (lane < N_EXPERTS + N_GROUPS)
    big = jnp.int32(LANES)

    def first_lane(mask):
        return jnp.min(jnp.where(mask, lane, big), axis=1, keepdims=True)

    gl = jnp.where(is_g, logits, NEG_INF)
    ge = jnp.exp(gl - jnp.max(gl, axis=1, keepdims=True))
    p_groups = ge / jnp.sum(ge, axis=1, keepdims=True)
    p_g = jnp.max(p_groups, axis=1, keepdims=True)
    g_idx = first_lane(is_g & (p_groups == p_g)) - N_EXPERTS

    in_grp = (lane < N_EXPERTS) & ((lane // EXP_PER_GROUP) == g_idx)
    el = jnp.where(in_grp, logits, NEG_INF)
    ee = jnp.exp(el - jnp.max(el, axis=1, keepdims=True))
    es = ee / jnp.sum(ee, axis=1, keepdims=True)
    es = jnp.where(in_grp, es, NEG_INF)
    v1 = jnp.max(es, axis=1, keepdims=True)
    i1 = first_lane(in_grp & (es == v1))
    es2 = jnp.where(lane == i1, NEG_INF, es)
    v2 = jnp.max(es2, axis=1, keepdims=True)
    i2 = first_lane(in_grp & (es2 == v2))
    tot = v1 + v2
    e_w = jnp.where(lane == i1, v1 / tot, 0.0) + jnp.where(lane == i2, v2 / tot, 0.0)
    return p_g * e_w + jnp.where(lane == ROUTE_GROUP_LANE + g_idx, 1.0, 0.0)


def _out_proj_kernel(ya_ref, yr_ref, x_ref, mod_ref, wo_ref, gpm_ref, gpf_ref, wr_ref, br_ref,
                     x1_ref, h2_ref, comb_ref):
    mix = (jnp.dot(ya_ref[0], wo_ref[0:D_ATTN, :], preferred_element_type=F32)
           + jnp.dot(yr_ref[0], wo_ref[D_ATTN:D_ATTN + D_RNN, :], preferred_element_type=F32))
    gate1 = mod_ref[0, 2:3, :]
    shift2 = mod_ref[0, 3:4, :]
    scale2 = mod_ref[0, 4:5, :]
    x1 = x_ref[0] + gate1 * _rms(mix, gpm_ref[...])
    x1_ref[0] = x1
    h2 = _rms(x1, gpf_ref[...]) * (1.0 + scale2) + shift2
    h_hi = h2.astype(BF16)
    h2_ref[0] = h_hi
    h_lo = (h2 - h_hi.astype(F32)).astype(BF16)
    part = jnp.dot(h_hi, wr_ref[...], preferred_element_type=F32)
    logits = (part[:, 0:LANES] + (part[:, LANES:2 * LANES]
              + jnp.dot(h_lo, wr_ref[:, 0:LANES], preferred_element_type=F32))) + br_ref[...]
    comb_ref[0] = _route(logits)


def _out_proj_call(ya, yr, x, mod, wo, gpm, gpf, wr, br):
    b, s, d = x.shape
    tm = min(TOKEN_TILE, s)
    tile = lambda n: pl.BlockSpec((1, tm, n), lambda bi, ti: (bi, ti, 0))
    const = lambda shape: pl.BlockSpec(shape, lambda bi, ti: (0,) * len(shape))
    return pl.pallas_call(
        _out_proj_kernel,
        grid=(b, s // tm),
        in_specs=[tile(D_ATTN), tile(D_RNN), tile(d),
                  pl.BlockSpec((1, N_MOD, d), lambda bi, ti: (bi, 0, 0)),
                  const((D_ATTN + D_RNN, d)), const((1, d)), const((1, d)),
                  const((d, 2 * LANES)), const((1, LANES))],
        out_specs=[tile(d), tile(d), tile(LANES)],
        out_shape=[jax.ShapeDtypeStruct((b, s, d), F32), jax.ShapeDtypeStruct((b, s, d), BF16),
                   jax.ShapeDtypeStruct((b, s, LANES), F32)],
        compiler_params=_cparams(("parallel", "parallel")),
        name="out_proj",
    )(ya, yr, x, mod, wo, gpm, gpf, wr, br)


def _moe_kernel(h2_ref, comb_ref, x1_ref, mod_ref, wg_ref, wu_ref, wd_ref, gpo_ref,
                o_ref, hs_ref, cs_ref, ys_ref, act_ref, slot_ref, seg_ref, *, tm, ns, rb):
    g = pl.program_id(2)
    lane = lax.broadcasted_iota(I32, (1, LANES), 1)
    n_blk = tm // LANES

    @pl.when(g == 0)
    def _dispatch():
        comb = comb_ref[0]
        oh = jnp.where((lane >= ROUTE_GROUP_LANE) & (lane < ROUTE_GROUP_LANE + N_GROUPS), comb, 0.0)
        oh16 = oh.astype(BF16)
        sub = lax.broadcasted_iota(I32, (LANES, 1), 0)
        base = jnp.int32(0)
        base_lane = jnp.zeros((1, LANES), F32)
        base_sub = jnp.zeros((LANES, 1), F32)
        for gg in range(N_GROUPS):
            here = ROUTE_GROUP_LANE + gg
            n = jnp.sum(jnp.where(lane == here, oh, 0.0)).astype(I32)
            nb = (n + rb - 1) // rb
            seg_ref[gg] = base
            seg_ref[N_GROUPS + gg] = nb
            base_f = base.astype(F32)
            base_lane = base_lane + jnp.where(lane == here, base_f, 0.0)
            base_sub = base_sub + jnp.where(sub == here, base_f, 0.0)
            base = base + nb * rb

        tok_c = lax.broadcasted_iota(I32, (1, tm), 1)
        tok_r = lax.broadcasted_iota(I32, (tm, 1), 0)
        blk_r = lax.broadcasted_iota(I32, (LANES, 1), 0)
        blk_c = lax.broadcasted_iota(I32, (1, LANES), 1)
        eye = (blk_r == blk_c).astype(BF16)
        oht16 = _nt_dot(eye, oh16).astype(BF16)
        slot_row = []
        for t in range(n_blk):
            tril = (tok_c <= blk_r + t * LANES).astype(BF16)
            rank = jnp.dot(tril, oh16, preferred_element_type=F32)
            oh_blk = oh[t * LANES:(t + 1) * LANES, :]
            slot_ref[t * LANES:(t + 1) * LANES, :] = jnp.sum(
                oh_blk * (base_lane + rank - 1.0), axis=1, keepdims=True).astype(I32)
            triu = (tok_r <= blk_c + t * LANES).astype(BF16)
            rank_t = jnp.dot(oht16, triu, preferred_element_type=F32)
            oht_blk = oht16[:, t * LANES:(t + 1) * LANES].astype(F32)
            slot_row.append(jnp.sum(oht_blk * (base_sub + rank_t - 1.0), axis=0,
                                    keepdims=True).astype(I32))
        slot_row = jnp.concatenate(slot_row, axis=1)

        hi = comb.astype(BF16)
        lo = (comb - hi.astype(F32)).astype(BF16)
        cw = jnp.concatenate([hi, lo], axis=1)
        h = h2_ref[0]
        for r in range(ns // LANES):
            rows = slice(r * LANES, (r + 1) * LANES)
            p = (blk_r + r * LANES == slot_row).astype(BF16)
            hs_ref[rows, :] = jnp.dot(p, h, preferred_element_type=F32).astype(BF16)
            cw_sorted = jnp.dot(p, cw, preferred_element_type=F32)
            cs_ref[rows, :] = cw_sorted[:, 0:LANES] + cw_sorted[:, LANES:2 * LANES]
        ys_ref[...] = jnp.zeros_like(ys_ref)

    base = seg_ref[g]

    def block(r, carry):
        rows = pl.ds(pl.multiple_of(base + r * rb, rb), rb)
        hb = hs_ref[rows, :]
        cb = cs_ref[rows, :]
        for e in range(EXP_PER_GROUP):
            gt = jnp.dot(hb, wg_ref[0, e], preferred_element_type=F32)
            up = jnp.dot(hb, wu_ref[0, e], preferred_element_type=F32)
            cols = slice(e * D_EXPERT, (e + 1) * D_EXPERT)
            scale = jnp.sum(jnp.where(lane == g * EXP_PER_GROUP + e, cb, 0.0), axis=1, keepdims=True)
            act_ref[:, cols] = (gt * jax.nn.sigmoid(gt) * up * scale).astype(BF16)
        ys_ref[rows, :] = jnp.dot(act_ref[...], wd_ref[0], preferred_element_type=F32).astype(BF16)
        return carry

    lax.fori_loop(0, seg_ref[N_GROUPS + g], block, 0)

    @pl.when(g == N_GROUPS - 1)
    def _combine():
        gate2 = mod_ref[0, 5:6, :]
        slot_c = lax.broadcasted_iota(I32, (1, ns), 1)
        half = tm // 2
        for hh in range(2):
            rows = slice(hh * half, (hh + 1) * half)
            pt = (slot_c == slot_ref[rows, :]).astype(BF16)
            y = jnp.dot(pt, ys_ref[...], preferred_element_type=F32)
            o_ref[0, rows, :] = x1_ref[0, rows, :] + gate2 * _rms(y, gpo_ref[...])


def _moe_call(h2, comb, x1, mod, wg, wu, wd, gpo):
    b, s, d = x1.shape
    tm = min(MOE_TILE, s)
    rb = MOE_ROWS
    ns = tm + N_GROUPS * rb
    tile = lambda n, **kw: pl.BlockSpec((1, tm, n), lambda bi, ti, gi: (bi, ti, 0), **kw)
    kern = functools.partial(_moe_kernel, tm=tm, ns=ns, rb=rb)
    return pl.pallas_call(
        kern,
        grid=(b, s // tm, N_GROUPS),
        in_specs=[tile(d), tile(LANES), tile(d, pipeline_mode=pl.Buffered(1)),
                  pl.BlockSpec((1, N_MOD, d), lambda bi, ti, gi: (bi, 0, 0)),
                  pl.BlockSpec((1, EXP_PER_GROUP, d, D_EXPERT), lambda bi, ti, gi: (gi, 0, 0, 0)),
                  pl.BlockSpec((1, EXP_PER_GROUP, d, D_EXPERT), lambda bi, ti, gi: (gi, 0, 0, 0)),
                  pl.BlockSpec((1, EXP_PER_GROUP * D_EXPERT, d), lambda bi, ti, gi: (gi, 0, 0)),
                  pl.BlockSpec((1, d), lambda bi, ti, gi: (0, 0))],
        out_specs=tile(d),
        out_shape=jax.ShapeDtypeStruct((b, s, d), F32),
        scratch_shapes=[pltpu.VMEM((ns, d), BF16),
                        pltpu.VMEM((ns, LANES), F32),
                        pltpu.VMEM((ns, d), BF16),
                        pltpu.VMEM((rb, EXP_PER_GROUP * D_EXPERT), BF16),
                        pltpu.VMEM((tm, 1), I32),
                        pltpu.SMEM((2 * N_GROUPS,), I32)],
        compiler_params=_cparams(("parallel", "parallel", "arbitrary")),
        name="moe",
    )(h2, comb, x1, mod, wg, wu, wd, gpo)


def _rope_tables(seq_len):
    pos = jnp.arange(seq_len, dtype=F32)
    inv = ROPE_THETA ** (-jnp.arange(0, HEAD_DIM, 2, dtype=F32) / HEAD_DIM)
    ang = pos[:, None] * inv[None, :]
    cos, sin = jnp.cos(ang), jnp.sin(ang)
    cos2 = jnp.concatenate([cos, cos, cos, cos], axis=1)
    sin2 = jnp.concatenate([-sin, sin, -sin, sin], axis=1)
    return cos2, sin2, cos2[:, :HEAD_DIM].T, sin2[:, :HEAD_DIM].T


def _block_diag(w):
    n, c, d = w.shape
    eye = jnp.eye(n, dtype=w.dtype)
    return (eye[:, None, :, None] * w[:, :, None, :]).reshape(n * c, n * d)


def _pad_cols(w, n):
    return jnp.pad(w, ((0, 0), (0, n - w.shape[1])))


def kernel(x, c, w_ada, b_ada, g_pre_mix, g_post_mix, g_pre_ffn, g_post_ffn, w_in, conv_w, conv_b, w_rg_a, b_rg_a, w_rg_x, b_rg_x, lru_lambda, w_out, w_router_group, b_router_group, w_router_expert, b_router_expert, w_gate, w_up, w_down):
    b, s, d = x.shape
    depth = w_ada.shape[0]
    cos2, sin2, cos_t, sin_t = _rope_tables(s)
    c_pad = jnp.pad(c, ((0, (-b) % SUBLANES), (0, 0)))
    o_q, o_k, o_v, o_qi, o_ki, o_wi, o_xr, o_xg = 0, 512, 1024, 1536, 2048, 2112, 2120, 2632
    for l in range(depth):
        wl = w_in[l]
        w_ki = wl[:, o_ki:o_wi]
        w_nat = jnp.concatenate([wl[:, o_k:o_v], w_ki, w_ki, wl[:, o_xr:o_xg], wl[:, o_xg:]],
                                axis=1).astype(BF16)
        w_tr = jnp.concatenate([wl[:, o_q:o_k], wl[:, o_qi:o_ki], wl[:, o_v:o_qi],
                                _pad_cols(wl[:, o_wi:o_xr], _R_END - _R_WI)], axis=1).T.astype(BF16)
        w_route = _pad_cols(jnp.concatenate([w_router_expert[l], w_router_group[l]], axis=1), LANES)
        w_route_hi = w_route.astype(BF16)
        w_route = jnp.concatenate(
            [w_route_hi, (w_route - w_route_hi.astype(F32)).astype(BF16)], axis=1)
        b_route = _pad_cols(jnp.concatenate([b_router_expert[l], b_router_group[l]])[None, :], LANES)

        mod = _mod_call(c_pad, w_ada[l], b_ada[l][None, :])[:b].reshape(b, N_MOD, d)
        qt, qit, wit, k, kk, vt, xr, xg = _in_proj_call(
            x, mod, g_pre_mix[l][None, :], w_nat, w_tr, cos2, sin2, cos_t, sin_t)
        y_attn = _attn_call(qt, qit, wit, kk, k, vt)
        y_rnn = _rglru_call(xr, xg, conv_w[l], conv_b[l][None, :],
                            _block_diag(w_rg_a[l]).astype(BF16), b_rg_a[l][None, :],
                            _block_diag(w_rg_x[l]).astype(BF16), b_rg_x[l][None, :],
                            lru_lambda[l][None, :])
        x1, h2, comb = _out_proj_call(y_attn, y_rnn, x, mod, w_out[l].astype(BF16),
                                      g_post_mix[l][None, :], g_pre_ffn[l][None, :], w_route, b_route)
        x = _moe_call(h2, comb, x1, mod, w_gate[l].astype(BF16), w_up[l].astype(BF16),
                      w_down[l].reshape(N_GROUPS, EXP_PER_GROUP * D_EXPERT, d).astype(BF16),
                      g_post_ffn[l][None, :])
    return x
```

```python
import functools
import math

import jax
import jax.numpy as jnp
from jax import lax
from jax.experimental import pallas as pl
from jax.experimental.pallas import tpu as pltpu

F32 = jnp.float32
BF16 = jnp.bfloat16
I32 = jnp.int32
I16 = jnp.int16

D_MODEL = 1024
CHUNK = 64
ROPE_THETA = 10000.0
EPS = 1e-6
N_HEADS = 8
HEAD_DIM = 64
D_ATTN = N_HEADS * HEAD_DIM
N_IDX_HEADS = 8
IDX_DIM = 64
TOPK_MAX = 256
D_RNN = 512
N_RNN_BLOCKS = 8
CONV_WIDTH = 4
RG_C = 8.0
N_GROUPS = 4
EXP_PER_GROUP = 8
N_EXPERTS = N_GROUPS * EXP_PER_GROUP
D_EXPERT = 256
N_MOD = 6

LANES = 128
SUBLANES = 8
INT_MIN = -2 ** 31
INT16_MIN = -2 ** 15
NEG_INF = float("-inf")

Q_BLOCK = 256
K_BLOCK = 512
HI_SPAN = 512
K_SUB = 256
HEADS_PER_STAGE = 2
V_ROWS = HEAD_DIM + 16
TOKEN_TILE = 512
MOE_TILE = 1024
MOE_ROWS = 128
ROUTE_GROUP_LANE = 64
VMEM_LIMIT = 60 * 1024 * 1024


def _cparams(sem):
    return pltpu.CompilerParams(dimension_semantics=sem, vmem_limit_bytes=VMEM_LIMIT)


def _nt_dot(a, b):
    return lax.dot_general(a, b, (((1,), (1,)), ((), ())), preferred_element_type=F32)


def _rms(x, g):
    return x * lax.rsqrt(jnp.mean(x * x, axis=-1, keepdims=True) + EPS) * g


def _mod_kernel(c_ref, w_ref, b_ref, o_ref):
    c = c_ref[...]
    sc = c * jax.nn.sigmoid(c)
    o_ref[...] = jnp.dot(sc, w_ref[...], preferred_element_type=F32,
                         precision=lax.Precision.HIGHEST) + b_ref[...]


def _mod_call(c_pad, w_ada, b_ada):
    rows, d = c_pad.shape
    n = w_ada.shape[1]
    bn = 1024
    return pl.pallas_call(
        _mod_kernel,
        grid=(n // bn,),
        in_specs=[pl.BlockSpec((rows, d), lambda j: (0, 0)),
                  pl.BlockSpec((d, bn), lambda j: (0, j)),
                  pl.BlockSpec((1, bn), lambda j: (0, j))],
        out_specs=pl.BlockSpec((rows, bn), lambda j: (0, j)),
        out_shape=jax.ShapeDtypeStruct((rows, n), F32),
        compiler_params=_cparams(("arbitrary",)),
        name="mod",
    )(c_pad, w_ada, b_ada)


_R_Q, _R_QI, _R_V, _R_WI, _R_END = 0, 512, 1024, 1536, 1552
_C_K, _C_KK, _C_XR, _C_XG, _C_END = 0, 512, 640, 1152, 1664


def _rope(y, cos, sin):
    lane = lax.broadcasted_iota(I32, (1, LANES), 1)
    low = (lane % HEAD_DIM) < (HEAD_DIM // 2)
    outs = []
    for p in range(y.shape[1] // LANES):
        s = y[:, p * LANES:(p + 1) * LANES]
        swapped = jnp.where(low, pltpu.roll(s, LANES - HEAD_DIM // 2, axis=1),
                            pltpu.roll(s, HEAD_DIM // 2, axis=1))
        outs.append(s * cos + swapped * sin)
    return outs[0] if len(outs) == 1 else jnp.concatenate(outs, axis=1)


def _rope_t(y, cos, sin):
    half = HEAD_DIM // 2
    outs = []
    for hd in range(y.shape[0] // HEAD_DIM):
        blk = y[hd * HEAD_DIM:(hd + 1) * HEAD_DIM, :]
        swapped = jnp.concatenate([blk[half:, :], blk[:half, :]], axis=0)
        outs.append(blk * cos + swapped * sin)
    return jnp.concatenate(outs, axis=0)


def _in_proj_kernel(x_ref, mod_ref, g_ref, wn_ref, wt_ref, cos_ref, sin_ref, cost_ref, sint_ref,
                    qt_ref, qit_ref, wit_ref, k_ref, kk_ref, vt_ref, xr_ref, xg_ref):
    x = x_ref[0]
    shift = mod_ref[0, 0:1, :]
    scale = mod_ref[0, 1:2, :]
    h = (_rms(x, g_ref[...]) * (1.0 + scale) + shift).astype(BF16)
    cos, sin = cos_ref[...], sin_ref[...]
    cos_t, sin_t = cost_ref[...], sint_ref[...]

    def proj(a, b):
        return jnp.dot(h, wn_ref[:, a:b], preferred_element_type=F32)

    def proj_t(a, b):
        return _nt_dot(wt_ref[a:b, :], h)

    qt_ref[0] = (_rope_t(proj_t(_R_Q, _R_QI), cos_t, sin_t)
                 * (HEAD_DIM ** -0.5 * math.log2(math.e))).astype(BF16)
    qit_ref[0] = (_rope_t(proj_t(_R_QI, _R_V), cos_t, sin_t) * (IDX_DIM ** -0.5)).astype(BF16)
    vt = proj_t(_R_V, _R_WI).astype(BF16)
    ones = jnp.ones((V_ROWS - HEAD_DIM, vt.shape[1]), BF16)
    for hd in range(N_HEADS):
        vt_ref[0, 0, hd * V_ROWS:hd * V_ROWS + HEAD_DIM, :] = vt[hd * HEAD_DIM:(hd + 1) * HEAD_DIM, :]
        vt_ref[0, 0, hd * V_ROWS + HEAD_DIM:(hd + 1) * V_ROWS, :] = ones
    wit_ref[0] = proj_t(_R_WI, _R_END)[0:N_IDX_HEADS, :] * (N_IDX_HEADS ** -0.5)
    k_ref[0] = _rope(proj(_C_K, _C_KK), cos, sin).astype(BF16)
    kk_ref[0] = _rope(proj(_C_KK, _C_XR), cos, sin).astype(BF16)
    xr_ref[0] = proj(_C_XR, _C_XG)
    xg_ref[0] = proj(_C_XG, _C_END)


def _in_proj_call(x, mod, g, w_nat, w_tr, cos2, sin2, cos_t, sin_t):
    b, s, d = x.shape
    tm = min(K_BLOCK, s)
    nt = s // tm
    tile = lambda n: pl.BlockSpec((1, tm, n), lambda bi, ti: (bi, ti, 0))
    tile_t = lambda n: pl.BlockSpec((1, n, tm), lambda bi, ti: (bi, 0, ti))
    const = lambda shape: pl.BlockSpec(shape, lambda bi, ti: (0, 0))
    shp = lambda n, dt: jax.ShapeDtypeStruct((b, s, n), dt)
    shp_t = lambda n, dt: jax.ShapeDtypeStruct((b, n, s), dt)
    return pl.pallas_call(
        _in_proj_kernel,
        grid=(b, nt),
        in_specs=[tile(d),
                  pl.BlockSpec((1, N_MOD, d), lambda bi, ti: (bi, 0, 0)),
                  const((1, d)), const((d, _C_END)), const((_R_END, d)),
                  pl.BlockSpec((tm, LANES), lambda bi, ti: (ti, 0)),
                  pl.BlockSpec((tm, LANES), lambda bi, ti: (ti, 0)),
                  pl.BlockSpec((HEAD_DIM, tm), lambda bi, ti: (0, ti)),
                  pl.BlockSpec((HEAD_DIM, tm), lambda bi, ti: (0, ti))],
        out_specs=[tile_t(D_ATTN), tile_t(D_ATTN), tile_t(N_IDX_HEADS), tile(D_ATTN), tile(LANES),
                   pl.BlockSpec((1, 1, N_HEADS * V_ROWS, tm), lambda bi, ti: (bi, ti, 0, 0)),
                   tile(D_RNN), tile(D_RNN)],
        out_shape=[shp_t(D_ATTN, BF16), shp_t(D_ATTN, BF16), shp_t(N_IDX_HEADS, F32),
                   shp(D_ATTN, BF16), shp(LANES, BF16),
                   jax.ShapeDtypeStruct((b, nt, N_HEADS * V_ROWS, tm), BF16),
                   shp(D_RNN, F32), shp(D_RNN, F32)],
        compiler_params=_cparams(("parallel", "parallel")),
        name="in_proj",
    )(x, mod, g, w_nat, w_tr, cos2, sin2, cos_t, sin_t)


def _sortable(score):
    bits = pltpu.bitcast(score, I32)
    return jnp.where(bits < 0, jnp.int32(INT_MIN) - bits, bits)


def _tile_fold(x, op, rows):
    parts = [x[t * rows:(t + 1) * rows, :] for t in range(x.shape[0] // rows)]
    while len(parts) > 1:
        nxt = [op(parts[t], parts[t + 1]) for t in range(0, len(parts) - 1, 2)]
        if len(parts) % 2:
            nxt.append(parts[-1])
        parts = nxt
    return parts[0]


def _row_fold(x, op):
    acc = x[0:SUBLANES, :]
    for t in range(1, x.shape[0] // SUBLANES):
        acc = op(acc, x[t * SUBLANES:(t + 1) * SUBLANES, :])
    return acc


def _fold16(x):
    return _tile_fold(x, jnp.add, 2 * SUBLANES)


def _attn_kernel(qt_ref, qit_ref, wit_ref, kk_ref, k_ref, vt_ref, o_ref,
                 keys_ref, hi_ref, lo_ref, bias_ref, s_ref, p_ref, qm_ref, qim_ref, m_ref, l_ref,
                 acc_ref, *, n_sel, qb, kb, sub):
    i = pl.program_id(1)
    n_ch = ((i + 1) * qb + kb - 1) // kb
    q_pos = i * qb + lax.broadcasted_iota(I32, (1, qb), 1)
    limit = (q_pos // CHUNK + 1) * CHUNK
    krow = lax.broadcasted_iota(I32, (kb, 1), 0)
    row128 = lax.broadcasted_iota(I32, (LANES, 1), 0)

    for h in range(N_HEADS):
        rows = slice((h // 2) * LANES, (h // 2 + 1) * LANES)
        keep = (row128 < HEAD_DIM) if h % 2 == 0 else (row128 >= HEAD_DIM)
        qs, qis = qt_ref[0, rows, :], qit_ref[0, rows, :]
        qm_ref[h] = jnp.where(keep, qs, jnp.zeros_like(qs))
        qim_ref[h] = jnp.where(keep, qis, jnp.zeros_like(qis))
    m_ref[...] = jnp.full(m_ref.shape, NEG_INF, F32)
    l_ref[...] = jnp.zeros(l_ref.shape, F32)
    acc_ref[...] = jnp.zeros(acc_ref.shape, F32)

    wi = wit_ref[0]

    def score_body(j, carry, masked):
        kk = kk_ref[0, pl.ds(pl.multiple_of(j * kb, kb), kb), :]
        acc = jnp.zeros((kb, qb), F32)
        for h in range(N_IDX_HEADS):
            d = jnp.dot(kk, qim_ref[h], preferred_element_type=F32)
            acc = acc + wi[h:h + 1, :] * jnp.maximum(d, 0.0)
        key = _sortable(acc)
        if masked:
            key = jnp.where(j * kb + krow < limit, key, jnp.int32(INT_MIN))
        keys_ref[j] = key
        hi_ref[j] = (key >> 16).astype(I16)
        return carry

    n_open = (i * qb + CHUNK) // kb
    lax.fori_loop(0, n_open, functools.partial(score_body, masked=False), 0)
    lax.fori_loop(n_open, n_ch, functools.partial(score_body, masked=True), 0)

    def sweep16(ref, tile_fn, op, init, finish):
        def body(jj, acc):
            return op(acc, op(tile_fn(ref[2 * jj]), tile_fn(ref[2 * jj + 1])))

        acc = lax.fori_loop(0, n_ch // 2, body, jnp.full((2 * SUBLANES, qb), init, I16))
        acc = lax.cond(n_ch % 2 == 1, lambda a: op(a, tile_fn(ref[n_ch - 1])), lambda a: a, acc)
        return finish(acc.astype(I32), axis=0, keepdims=True)

    def count16(ref, pred):
        ones = lambda v: _fold16(jnp.where(pred(v), jnp.int16(1), jnp.int16(0)))
        return sweep16(ref, ones, jnp.add, 0, jnp.sum)

    def max16(ref):
        pick = lambda a, b: jnp.where(a > b, a, b)
        tile_max = lambda v: _tile_fold(v, pick, 2 * SUBLANES)
        return sweep16(ref, tile_max, pick, INT16_MIN, jnp.max)

    def search16(ref, target):
        c0 = count16(ref, lambda v: v >= jnp.int16(0))
        ok = c0 >= target
        init = (jnp.where(ok, jnp.int32(0), jnp.int32(INT16_MIN)),
                jnp.where(ok, c0, n_ch * kb), jnp.where(ok, 0, c0))

        def bit_body(bi, carry):
            t, n_ge, n_gt = carry
            cand = t | (jnp.int32(1) << (14 - bi))
            c = count16(ref, lambda v: v >= cand.astype(I16))
            ok = c >= target
            return jnp.where(ok, cand, t), jnp.where(ok, c, n_ge), jnp.where(ok, n_gt, c)

        return lax.fori_loop(0, 15, bit_body, init)

    def search_span(ref, target, lo, c_lo, hi):
        def body(_, carry):
            lo, c_lo, hi, c_hi1 = carry
            mid = (lo + hi + 1) >> 1
            c = count16(ref, lambda v: v >= mid.astype(I16))
            ok = c >= target
            return (jnp.where(ok, mid, lo), jnp.where(ok, c, c_lo),
                    jnp.where(ok, hi, mid - 1), jnp.where(ok, c_hi1, c))

        n_steps = (HI_SPAN - 1).bit_length()
        t, n_ge, _, n_gt = lax.fori_loop(0, n_steps, body, (lo, c_lo, hi, jnp.zeros_like(c_lo)))
        return t, n_ge, n_gt

    top = max16(hi_ref)
    floor_ = jnp.maximum(top - (HI_SPAN - 1), jnp.int32(INT16_MIN))
    c_floor = count16(hi_ref, lambda v: v >= floor_.astype(I16))
    tau_hi, _, n_gt_hi = lax.cond(
        jnp.min(c_floor) >= n_sel,
        lambda: search_span(hi_ref, n_sel, floor_, c_floor, top),
        lambda: search16(hi_ref, n_sel))
    tau_hi16 = tau_hi.astype(I16)

    def lo_body(j, carry):
        lo = (keys_ref[j] ^ jnp.int32(0x8000)).astype(I16)
        lo_ref[j] = jnp.where(hi_ref[j] == tau_hi16, lo, jnp.int16(INT16_MIN))
        return carry

    lax.fori_loop(0, n_ch, lo_body, 0)
    tau_lo, n_ge_lo, n_gt_lo = search16(lo_ref, n_sel - n_gt_hi)
    tau = (tau_hi << 16) | ((tau_lo + 32768) & jnp.int32(0xFFFF))
    all_sel = tau == jnp.int32(INT_MIN)
    need_i = jnp.where(all_sel, 0, n_sel - n_gt_hi - n_gt_lo)
    need = need_i.astype(F32)
    ties = jnp.max(jnp.where(all_sel, 0, (n_ge_lo - n_gt_lo) - need_i)) > 0
    thr = jnp.where(all_sel, jnp.int32(INT_MIN + 1), tau)

    n_sub = kb // sub
    no_max = jnp.full((SUBLANES, qb), NEG_INF, F32)

    def bias_ties(j, slot, run):
        r_i = lax.broadcasted_iota(I32, (kb, kb), 0)
        c_i = lax.broadcasted_iota(I32, (kb, kb), 1)
        tri = (c_i <= r_i).astype(BF16)
        kc = keys_ref[j]
        eq = kc == tau
        rank = jnp.dot(tri, eq.astype(BF16), preferred_element_type=F32) + run
        sel = (kc > tau) | (eq & (rank <= need))
        bias_ref[slot] = jnp.where(sel, 0.0, NEG_INF).astype(F32)
        return rank[kb - 1:kb, :]

    def bias_plain(j, slot, run):
        bias_ref[slot] = jnp.where(keys_ref[j] >= thr, 0.0, NEG_INF).astype(F32)
        return run

    gs = HEADS_PER_STAGE
    n_stage = N_HEADS // gs

    def logits(jb, bslot, h, r, mx):
        cols = slice((h // 2) * LANES, (h // 2 + 1) * LANES)
        rows = pl.ds(pl.multiple_of(jb * kb + r * sub, sub), sub)
        x = (jnp.dot(k_ref[0, rows, cols], qm_ref[h], preferred_element_type=F32)
             + bias_ref[bslot, r * sub:(r + 1) * sub, :])
        s_ref[(h // gs % 2) * gs + h % gs, r * sub:(r + 1) * sub, :] = x
        return jnp.maximum(mx, _row_fold(x, jnp.maximum))

    run0 = lax.cond(ties, bias_ties, bias_plain, 0, 0, jnp.zeros((1, qb), F32))
    mx0 = [no_max] * gs
    for r in range(n_sub):
        mx0 = [logits(0, 0, g, r, mx0[g]) for g in range(gs)]

    def attn_body(j, carry):
        run, mx = carry
        mx = list(mx)
        jn = jnp.minimum(j + 1, n_ch - 1)
        run = lax.cond(ties, bias_ties, bias_plain, jn, (j + 1) % 2, run)
        for st in range(n_stage):
            heads = [st * gs + g for g in range(gs)]
            m_new, m_safe, alpha = [], [], []
            for g, h in enumerate(heads):
                m_old = m_ref[h]
                m_new.append(jnp.maximum(m_old, jnp.max(mx[g], axis=0, keepdims=True)))
                m_safe.append(jnp.where(m_new[g] == NEG_INF, 0.0, m_new[g]))
                alpha.append(jnp.exp2(m_old - m_safe[g]))
            mx = [no_max] * gs
            for r in range(n_sub):
                tile = slice(r * sub, (r + 1) * sub)
                for g in range(gs):
                    if st + 1 < n_stage:
                        mx[g] = logits(j, j % 2, heads[g] + gs, r, mx[g])
                    else:
                        mx[g] = logits(jn, (j + 1) % 2, g, r, mx[g])
                for g in range(gs):
                    p_ref[g, tile, :] = jnp.exp2(
                        s_ref[(st % 2) * gs + g, tile, :] - m_safe[g]).astype(BF16)
            for g, h in enumerate(heads):
                pv = jnp.dot(vt_ref[0, j, h * V_ROWS:(h + 1) * V_ROWS, :], p_ref[g],
                             preferred_element_type=F32)
                l_ref[h] = alpha[g] * l_ref[h] + pv[HEAD_DIM:HEAD_DIM + 1, :]
                acc_ref[h] = alpha[g] * acc_ref[h] + pv[0:HEAD_DIM, :]
                m_ref[h] = m_new[g]
        return run, tuple(mx)

    lax.fori_loop(0, n_ch, attn_body, (run0, tuple(mx0)))

    outs = [acc_ref[h] / l_ref[h] for h in range(N_HEADS)]
    o_ref[0] = jnp.concatenate(outs, axis=0).T.astype(o_ref.dtype)


def _attn_call(qt, qit, wit, kk, k, vt):
    b, s, _ = k.shape
    qb = min(Q_BLOCK, s)
    kb = min(K_BLOCK, s)
    n_sel = min(TOPK_MAX, s // 4)
    blk_t = lambda n: pl.BlockSpec((1, n, qb), lambda bi, qi_: (bi, 0, qi_))
    once = pl.Buffered(1)
    kern = functools.partial(_attn_kernel, n_sel=n_sel, qb=qb, kb=kb, sub=min(K_SUB, kb))
    return pl.pallas_call(
        kern,
        grid=(b, s // qb),
        in_specs=[blk_t(D_ATTN), blk_t(D_ATTN), blk_t(N_IDX_HEADS),
                  pl.BlockSpec((1, s, LANES), lambda bi, qi_: (bi, 0, 0), pipeline_mode=once),
                  pl.BlockSpec((1, s, D_ATTN), lambda bi, qi_: (bi, 0, 0), pipeline_mode=once),
                  pl.BlockSpec((1, s // kb, N_HEADS * V_ROWS, kb), lambda bi, qi_: (bi, 0, 0, 0),
                               pipeline_mode=once)],
        out_specs=pl.BlockSpec((1, qb, D_ATTN), lambda bi, qi_: (bi, qi_, 0)),
        out_shape=jax.ShapeDtypeStruct((b, s, D_ATTN), BF16),
        scratch_shapes=[pltpu.VMEM((s // kb, kb, qb), I32),
                        pltpu.VMEM((s // kb, kb, qb), I16),
                        pltpu.VMEM((s // kb, kb, qb), I16),
                        pltpu.VMEM((2, kb, qb), F32),
                        pltpu.VMEM((2 * HEADS_PER_STAGE, kb, qb), F32),
                        pltpu.VMEM((HEADS_PER_STAGE, kb, qb), BF16),
                        pltpu.VMEM((N_HEADS, LANES, qb), BF16),
                        pltpu.VMEM((N_IDX_HEADS, LANES, qb), BF16),
                        pltpu.VMEM((N_HEADS, 1, qb), F32),
                        pltpu.VMEM((N_HEADS, 1, qb), F32),
                        pltpu.VMEM((N_HEADS, HEAD_DIM, qb), F32)],
        compiler_params=_cparams(("parallel", "arbitrary")),
        name="attn",
    )(qt, qit, wit, kk, k, vt)


def _gelu_tanh(x):
    return 0.5 * x * (1.0 + jnp.tanh(math.sqrt(2.0 / math.pi) * (x + 0.044715 * (x * x * x))))


def _rglru_kernel(xr_ref, xg_ref, cw_ref, cb_ref, wa_ref, ba_ref, wx_ref, bx_ref, lam_ref,
                  o_ref, xbuf_ref, h_ref, *, ts):
    t = pl.program_id(1)
    pad = 8

    @pl.when(t == 0)
    def _():
        xbuf_ref[0:pad, :] = jnp.zeros((pad, D_RNN), F32)
        h_ref[...] = jnp.zeros_like(h_ref)

    xr = xr_ref[0]
    xbuf_ref[pad:pad + ts, :] = xr
    conv = cb_ref[...] + cw_ref[CONV_WIDTH - 1:CONV_WIDTH, :] * xr
    for w in range(CONV_WIDTH - 1):
        off = pad - (CONV_WIDTH - 1) + w
        conv = conv + cw_ref[w:w + 1, :] * xbuf_ref[off:off + ts, :]
    xbuf_ref[0:pad, :] = xr[ts - pad:ts, :]

    cb16 = conv.astype(BF16)
    r = jax.nn.sigmoid(jnp.dot(cb16, wa_ref[...], preferred_element_type=F32) + ba_ref[...])
    ig = jax.nn.sigmoid(jnp.dot(cb16, wx_ref[...], preferred_element_type=F32) + bx_ref[...])
    z = -lam_ref[...]
    softplus = jnp.maximum(z, 0.0) + jnp.log1p(jnp.exp(-jnp.abs(z)))
    log_a = (-RG_C) * r * softplus
    a = jnp.exp(log_a)
    bb = jnp.sqrt(1.0 - a * a) * (ig * conv)

    row = lax.broadcasted_iota(I32, (ts, 1), 0)
    d = 1
    while d < ts:
        if d < SUBLANES:
            live = row >= d
            a_s = jnp.where(live, pltpu.roll(a, d, axis=0), 1.0)
            b_s = jnp.where(live, pltpu.roll(bb, d, axis=0), 0.0)
            bb = bb + a * b_s
            a = a * a_s
        else:
            lo_a, hi_a = a[:d, :], a[d:, :]
            bb = jnp.concatenate([bb[:d, :], bb[d:, :] + hi_a * bb[:ts - d, :]], axis=0)
            a = jnp.concatenate([lo_a, hi_a * a[:ts - d, :]], axis=0)
        d *= 2
    h = bb + a * h_ref[0:1, :]
    h_ref[0:1, :] = h[ts - 1:ts, :]
    o_ref[0] = (h * _gelu_tanh(xg_ref[0])).astype(o_ref.dtype)


def _rglru_call(xr, xg, conv_w, conv_b, wa, ba, wx, bx, lam):
    b, s, r = xr.shape
    ts = min(TOKEN_TILE, s)
    tile = pl.BlockSpec((1, ts, r), lambda bi, ti: (bi, ti, 0))
    const = lambda shape: pl.BlockSpec(shape, lambda bi, ti: (0, 0))
    return pl.pallas_call(
        functools.partial(_rglru_kernel, ts=ts),
        grid=(b, s // ts),
        in_specs=[tile, tile, const((CONV_WIDTH, r)), const((1, r)), const((r, r)), const((1, r)),
                  const((r, r)), const((1, r)), const((1, r))],
        out_specs=tile,
        out_shape=jax.ShapeDtypeStruct((b, s, r), BF16),
        scratch_shapes=[pltpu.VMEM((ts + 8, r), F32), pltpu.VMEM((8, r), F32)],
        compiler_params=_cparams(("parallel", "arbitrary")),
        name="rglru",
    )(xr, xg, conv_w, conv_b, wa, ba, wx, bx, lam)


def _route(logits):
    lane = lax.broadcasted_iota(I32, logits.shape, 1)
    is_g = (lane >= N_EXPERTS) & (lane < N_EXPERTS + N_GROUPS)
    big = jnp.int32(LANES)

    def first_lane(mask):
        return jnp.min(jnp.where(mask, lane, big), axis=1, keepdims=True)

    gl = jnp.where(is_g, logits, NEG_INF)
    ge = jnp.exp(gl - jnp.max(gl, axis=1, keepdims=True))
    p_groups = ge / jnp.sum(ge, axis=1, keepdims=True)
    p_g = jnp.max(p_groups, axis=1, keepdims=True)
    g_idx = first_lane(is_g & (p_groups == p_g)) - N_EXPERTS

    in_grp = (lane < N_EXPERTS) & ((lane // EXP_PER_GROUP) == g_idx)
    el = jnp.where(in_grp, logits, NEG_INF)
    ee = jnp.exp(el - jnp.max(el, axis=1, keepdims=True))
    es = ee / jnp.sum(ee, axis=1, keepdims=True)
    es = jnp.where(in_grp, es, NEG_INF)
    v1 = jnp.max(es, axis=1, keepdims=True)
    i1 = first_lane(in_grp & (es == v1))
    es2 = jnp.where(lane == i1, NEG_INF, es)
    v2 = jnp.max(es2, axis=1, keepdims=True)
    i2 = first_lane(in_grp & (es2 == v2))
    tot = v1 + v2
    e_w = jnp.where(lane == i1, v1 / tot, 0.0) + jnp.where(lane == i2, v2 / tot, 0.0)
    return p_g * e_w + jnp.where(lane == ROUTE_GROUP_LANE + g_idx, 1.0, 0.0)


def _out_proj_kernel(ya_ref, yr_ref, x_ref, mod_ref, wo_ref, gpm_ref, gpf_ref, wr_ref, br_ref,
                     x1_ref, h2_ref, comb_ref):
    mix = (jnp.dot(ya_ref[0], wo_ref[0:D_ATTN, :], preferred_element_type=F32)
           + jnp.dot(yr_ref[0], wo_ref[D_ATTN:D_ATTN + D_RNN, :], preferred_element_type=F32))
    gate1 = mod_ref[0, 2:3, :]
    shift2 = mod_ref[0, 3:4, :]
    scale2 = mod_ref[0, 4:5, :]
    x1 = x_ref[0] + gate1 * _rms(mix, gpm_ref[...])
    x1_ref[0] = x1
    h2 = _rms(x1, gpf_ref[...]) * (1.0 + scale2) + shift2
    h_hi = h2.astype(BF16)
    h2_ref[0] = h_hi
    h_lo = (h2 - h_hi.astype(F32)).astype(BF16)
    part = jnp.dot(h_hi, wr_ref[...], preferred_element_type=F32)
    logits = (part[:, 0:LANES] + (part[:, LANES:2 * LANES]
              + jnp.dot(h_lo, wr_ref[:, 0:LANES], preferred_element_type=F32))) + br_ref[...]
    comb_ref[0] = _route(logits)


def _out_proj_call(ya, yr, x, mod, wo, gpm, gpf, wr, br):
    b, s, d = x.shape
    tm = min(TOKEN_TILE, s)
    tile = lambda n: pl.BlockSpec((1, tm, n), lambda bi, ti: (bi, ti, 0))
    const = lambda shape: pl.BlockSpec(shape, lambda bi, ti: (0,) * len(shape))
    return pl.pallas_call(
        _out_proj_kernel,
        grid=(b, s // tm),
        in_specs=[tile(D_ATTN), tile(D_RNN), tile(d),
                  pl.BlockSpec((1, N_MOD, d), lambda bi, ti: (bi, 0, 0)),
                  const((D_ATTN + D_RNN, d)), const((1, d)), const((1, d)),
                  const((d, 2 * LANES)), const((1, LANES))],
        out_specs=[tile(d), tile(d), tile(LANES)],
        out_shape=[jax.ShapeDtypeStruct((b, s, d), F32), jax.ShapeDtypeStruct((b, s, d), BF16),
                   jax.ShapeDtypeStruct((b, s, LANES), F32)],
        compiler_params=_cparams(("parallel", "parallel")),
        name="out_proj",
    )(ya, yr, x, mod, wo, gpm, gpf, wr, br)


def _moe_kernel(h2_ref, comb_ref, x1_ref, mod_ref, wg_ref, wu_ref, wd_ref, gpo_ref,
                o_ref, hs_ref, cs_ref, ys_ref, act_ref, slot_ref, seg_ref, *, tm, ns, rb):
    g = pl.program_id(2)
    lane = lax.broadcasted_iota(I32, (1, LANES), 1)
    n_blk = tm // LANES

    @pl.when(g == 0)
    def _dispatch():
        comb = comb_ref[0]
        oh = jnp.where((lane >= ROUTE_GROUP_LANE) & (lane < ROUTE_GROUP_LANE + N_GROUPS), comb, 0.0)
        oh16 = oh.astype(BF16)
        sub = lax.broadcasted_iota(I32, (LANES, 1), 0)
        base = jnp.int32(0)
        base_lane = jnp.zeros((1, LANES), F32)
        base_sub = jnp.zeros((LANES, 1), F32)
        for gg in range(N_GROUPS):
            here = ROUTE_GROUP_LANE + gg
            n = jnp.sum(jnp.where(lane == here, oh, 0.0)).astype(I32)
            nb = (n + rb - 1) // rb
            seg_ref[gg] = base
            seg_ref[N_GROUPS + gg] = nb
            base_f = base.astype(F32)
            base_lane = base_lane + jnp.where(lane == here, base_f, 0.0)
            base_sub = base_sub + jnp.where(sub == here, base_f, 0.0)
            base = base + nb * rb

        tok_c = lax.broadcasted_iota(I32, (1, tm), 1)
        tok_r = lax.broadcasted_iota(I32, (tm, 1), 0)
        blk_r = lax.broadcasted_iota(I32, (LANES, 1), 0)
        blk_c = lax.broadcasted_iota(I32, (1, LANES), 1)
        eye = (blk_r == blk_c).astype(BF16)
        oht16 = _nt_dot(eye, oh16).astype(BF16)
        slot_row = []
        for t in range(n_blk):
            tril = (tok_c <= blk_r + t * LANES).astype(BF16)
            rank = jnp.dot(tril, oh16, preferred_element_type=F32)
            oh_blk = oh[t * LANES:(t + 1) * LANES, :]
            slot_ref[t * LANES:(t + 1) * LANES, :] = jnp.sum(
                oh_blk * (base_lane + rank - 1.0), axis=1, keepdims=True).astype(I32)
            triu = (tok_r <= blk_c + t * LANES).astype(BF16)
            rank_t = jnp.dot(oht16, triu, preferred_element_type=F32)
            oht_blk = oht16[:, t * LANES:(t + 1) * LANES].astype(F32)
            slot_row.append(jnp.sum(oht_blk * (base_sub + rank_t - 1.0), axis=0,
                                    keepdims=True).astype(I32))
        slot_row = jnp.concatenate(slot_row, axis=1)

        hi = comb.astype(BF16)
        lo = (comb - hi.astype(F32)).astype(BF16)
        cw = jnp.concatenate([hi, lo], axis=1)
        h = h2_ref[0]
        for r in range(ns // LANES):
            rows = slice(r * LANES, (r + 1) * LANES)
            p = (blk_r + r * LANES == slot_row).astype(BF16)
            hs_ref[rows, :] = jnp.dot(p, h, preferred_element_type=F32).astype(BF16)
            cw_sorted = jnp.dot(p, cw, preferred_element_type=F32)
            cs_ref[rows, :] = cw_sorted[:, 0:LANES] + cw_sorted[:, LANES:2 * LANES]
        ys_ref[...] = jnp.zeros_like(ys_ref)

    base = seg_ref[g]

    def block(r, carry):
        rows = pl.ds(pl.multiple_of(base + r * rb, rb), rb)
        hb = hs_ref[rows, :]
        cb = cs_ref[rows, :]
        for e in range(EXP_PER_GROUP):
            gt = jnp.dot(hb, wg_ref[0, e], preferred_element_type=F32)
            up = jnp.dot(hb, wu_ref[0, e], preferred_element_type=F32)
            cols = slice(e * D_EXPERT, (e + 1) * D_EXPERT)
            scale = jnp.sum(jnp.where(lane == g * EXP_PER_GROUP + e, cb, 0.0), axis=1, keepdims=True)
            act_ref[:, cols] = (gt * jax.nn.sigmoid(gt) * up * scale).astype(BF16)
        ys_ref[rows, :] = jnp.dot(act_ref[...], wd_ref[0], preferred_element_type=F32).astype(BF16)
        return carry

    lax.fori_loop(0, seg_ref[N_GROUPS + g], block, 0)

    @pl.when(g == N_GROUPS - 1)
    def _combine():
        gate2 = mod_ref[0, 5:6, :]
        slot_c = lax.broadcasted_iota(I32, (1, ns), 1)
        half = tm // 2
        for hh in range(2):
            rows = slice(hh * half, (hh + 1) * half)
            pt = (slot_c == slot_ref[rows, :]).astype(BF16)
            y = jnp.dot(pt, ys_ref[...], preferred_element_type=F32)
            o_ref[0, rows, :] = x1_ref[0, rows, :] + gate2 * _rms(y, gpo_ref[...])


def _moe_call(h2, comb, x1, mod, wg, wu, wd, gpo):
    b, s, d = x1.shape
    tm = min(MOE_TILE, s)
    rb = MOE_ROWS
    ns = tm + N_GROUPS * rb
    tile = lambda n, **kw: pl.BlockSpec((1, tm, n), lambda bi, ti, gi: (bi, ti, 0), **kw)
    kern = functools.partial(_moe_kernel, tm=tm, ns=ns, rb=rb)
    return pl.pallas_call(
        kern,
        grid=(b, s // tm, N_GROUPS),
        in_specs=[tile(d), tile(LANES), tile(d, pipeline_mode=pl.Buffered(1)),
                  pl.BlockSpec((1, N_MOD, d), lambda bi, ti, gi: (bi, 0, 0)),
                  pl.BlockSpec((1, EXP_PER_GROUP, d, D_EXPERT), lambda bi, ti, gi: (gi, 0, 0, 0)),
                  pl.BlockSpec((1, EXP_PER_GROUP, d, D_EXPERT), lambda bi, ti, gi: (gi, 0, 0, 0)),
                  pl.BlockSpec((1, EXP_PER_GROUP * D_EXPERT, d), lambda bi, ti, gi: (gi, 0, 0)),
                  pl.BlockSpec((1, d), lambda bi, ti, gi: (0, 0))],
        out_specs=tile(d),
        out_shape=jax.ShapeDtypeStruct((b, s, d), F32),
        scratch_shapes=[pltpu.VMEM((ns, d), BF16),
                        pltpu.VMEM((ns, LANES), F32),
                        pltpu.VMEM((ns, d), BF16),
                        pltpu.VMEM((rb, EXP_PER_GROUP * D_EXPERT), BF16),
                        pltpu.VMEM((tm, 1), I32),
                        pltpu.SMEM((2 * N_GROUPS,), I32)],
        compiler_params=_cparams(("parallel", "parallel", "arbitrary")),
        name="moe",
    )(h2, comb, x1, mod, wg, wu, wd, gpo)


def _rope_tables(seq_len):
    pos = jnp.arange(seq_len, dtype=F32)
    inv = ROPE_THETA ** (-jnp.arange(0, HEAD_DIM, 2, dtype=F32) / HEAD_DIM)
    ang = pos[:, None] * inv[None, :]
    cos, sin = jnp.cos(ang), jnp.sin(ang)
    cos2 = jnp.concatenate([cos, cos, cos, cos], axis=1)
    sin2 = jnp.concatenate([-sin, sin, -sin, sin], axis=1)
    return cos2, sin2, cos2[:, :HEAD_DIM].T, sin2[:, :HEAD_DIM].T


def _block_diag(w):
    n, c, d = w.shape
    eye = jnp.eye(n, dtype=w.dtype)
    return (eye[:, None, :, None] * w[:, :, None, :]).reshape(n * c, n * d)


def _pad_cols(w, n):
    return jnp.pad(w, ((0, 0), (0, n - w.shape[1])))


def kernel(x, c, w_ada, b_ada, g_pre_mix, g_post_mix, g_pre_ffn, g_post_ffn, w_in, conv_w, conv_b, w_rg_a, b_rg_a, w_rg_x, b_rg_x, lru_lambda, w_out, w_router_group, b_router_group, w_router_expert, b_router_expert, w_gate, w_up, w_down):
    b, s, d = x.shape
    depth = w_ada.shape[0]
    cos2, sin2, cos_t, sin_t = _rope_tables(s)
    c_pad = jnp.pad(c, ((0, (-b) % SUBLANES), (0, 0)))
    o_q, o_k, o_v, o_qi, o_ki, o_wi, o_xr, o_xg = 0, 512, 1024, 1536, 2048, 2112, 2120, 2632
    for l in range(depth):
        wl = w_in[l]
        w_ki = wl[:, o_ki:o_wi]
        w_nat = jnp.concatenate([wl[:, o_k:o_v], w_ki, w_ki, wl[:, o_xr:o_xg], wl[:, o_xg:]],
                                axis=1).astype(BF16)
        w_tr = jnp.concatenate([wl[:, o_q:o_k], wl[:, o_qi:o_ki], wl[:, o_v:o_qi],
                                _pad_cols(wl[:, o_wi:o_xr], _R_END - _R_WI)], axis=1).T.astype(BF16)
        w_route = _pad_cols(jnp.concatenate([w_router_expert[l], w_router_group[l]], axis=1), LANES)
        w_route_hi = w_route.astype(BF16)
        w_route = jnp.concatenate(
            [w_route_hi, (w_route - w_route_hi.astype(F32)).astype(BF16)], axis=1)
        b_route = _pad_cols(jnp.concatenate([b_router_expert[l], b_router_group[l]])[None, :], LANES)

        mod = _mod_call(c_pad, w_ada[l], b_ada[l][None, :])[:b].reshape(b, N_MOD, d)
        qt, qit, wit, k, kk, vt, xr, xg = _in_proj_call(
            x, mod, g_pre_mix[l][None, :], w_nat, w_tr, cos2, sin2, cos_t, sin_t)
        y_attn = _attn_call(qt, qit, wit, kk, k, vt)
        y_rnn = _rglru_call(xr, xg, conv_w[l], conv_b[l][None, :],
                            _block_diag(w_rg_a[l]).astype(BF16), b_rg_a[l][None, :],
                            _block_diag(w_rg_x[l]).astype(BF16), b_rg_x[l][None, :],
                            lru_lambda[l][None, :])
        x1, h2, comb = _out_proj_call(y_attn, y_rnn, x, mod, w_out[l].astype(BF16),
                                      g_post_mix[l][None, :], g_pre_ffn[l][None, :], w_route, b_route)
        x = _moe_call(h2, comb, x1, mod, w_gate[l].astype(BF16), w_up[l].astype(BF16),
                      w_down[l].reshape(N_GROUPS, EXP_PER_GROUP * D_EXPERT, d).astype(BF16),
                      g_post_ffn[l][None, :])
    return x
```

```python
import functools
import math

import jax
import jax.numpy as jnp
from jax import lax
from jax.experimental import pallas as pl
from jax.experimental.pallas import tpu as pltpu

F32 = jnp.float32
BF16 = jnp.bfloat16
I32 = jnp.int32
I16 = jnp.int16

D_MODEL = 1024
CHUNK = 64
ROPE_THETA = 10000.0
EPS = 1e-6
N_HEADS = 8
HEAD_DIM = 64
D_ATTN = N_HEADS * HEAD_DIM
N_IDX_HEADS = 8
IDX_DIM = 64
TOPK_MAX = 256
D_RNN = 512
N_RNN_BLOCKS = 8
CONV_WIDTH = 4
RG_C = 8.0
N_GROUPS = 4
EXP_PER_GROUP = 8
N_EXPERTS = N_GROUPS * EXP_PER_GROUP
D_EXPERT = 256
N_MOD = 6

LANES = 128
SUBLANES = 8
INT_MIN = -2 ** 31
INT16_MIN = -2 ** 15
NEG_INF = float("-inf")

Q_BLOCK = 256
K_BLOCK = 512
K_SUB = 256
HEADS_PER_STAGE = 2
V_ROWS = HEAD_DIM + 16
TOKEN_TILE = 512
MOE_TILE = 1024
MOE_ROWS = 128
ROUTE_GROUP_LANE = 64
VMEM_LIMIT = 60 * 1024 * 1024


def _cparams(sem):
    return pltpu.CompilerParams(dimension_semantics=sem, vmem_limit_bytes=VMEM_LIMIT)


def _nt_dot(a, b):
    return lax.dot_general(a, b, (((1,), (1,)), ((), ())), preferred_element_type=F32)


def _rms(x, g):
    return x * lax.rsqrt(jnp.mean(x * x, axis=-1, keepdims=True) + EPS) * g


def _mod_kernel(c_ref, w_ref, b_ref, o_ref):
    c = c_ref[...]
    sc = c * jax.nn.sigmoid(c)
    o_ref[...] = jnp.dot(sc, w_ref[...], preferred_element_type=F32,
                         precision=lax.Precision.HIGHEST) + b_ref[...]


def _mod_call(c_pad, w_ada, b_ada):
    rows, d = c_pad.shape
    n = w_ada.shape[1]
    bn = 1024
    return pl.pallas_call(
        _mod_kernel,
        grid=(n // bn,),
        in_specs=[pl.BlockSpec((rows, d), lambda j: (0, 0)),
                  pl.BlockSpec((d, bn), lambda j: (0, j)),
                  pl.BlockSpec((1, bn), lambda j: (0, j))],
        out_specs=pl.BlockSpec((rows, bn), lambda j: (0, j)),
        out_shape=jax.ShapeDtypeStruct((rows, n), F32),
        compiler_params=_cparams(("arbitrary",)),
        name="mod",
    )(c_pad, w_ada, b_ada)


_R_Q, _R_QI, _R_V, _R_WI, _R_END = 0, 512, 1024, 1536, 1552
_C_K, _C_KK, _C_XR, _C_XG, _C_END = 0, 512, 640, 1152, 1664


def _rope(y, cos, sin):
    lane = lax.broadcasted_iota(I32, (1, LANES), 1)
    low = (lane % HEAD_DIM) < (HEAD_DIM // 2)
    outs = []
    for p in range(y.shape[1] // LANES):
        s = y[:, p * LANES:(p + 1) * LANES]
        swapped = jnp.where(low, pltpu.roll(s, LANES - HEAD_DIM // 2, axis=1),
                            pltpu.roll(s, HEAD_DIM // 2, axis=1))
        outs.append(s * cos + swapped * sin)
    return outs[0] if len(outs) == 1 else jnp.concatenate(outs, axis=1)


def _rope_t(y, cos, sin):
    half = HEAD_DIM // 2
    outs = []
    for hd in range(y.shape[0] // HEAD_DIM):
        blk = y[hd * HEAD_DIM:(hd + 1) * HEAD_DIM, :]
        swapped = jnp.concatenate([blk[half:, :], blk[:half, :]], axis=0)
        outs.append(blk * cos + swapped * sin)
    return jnp.concatenate(outs, axis=0)


def _in_proj_kernel(x_ref, mod_ref, g_ref, wn_ref, wt_ref, cos_ref, sin_ref, cost_ref, sint_ref,
                    qt_ref, qit_ref, wit_ref, k_ref, kk_ref, vt_ref, xr_ref, xg_ref):
    x = x_ref[0]
    shift = mod_ref[0, 0:1, :]
    scale = mod_ref[0, 1:2, :]
    h = (_rms(x, g_ref[...]) * (1.0 + scale) + shift).astype(BF16)
    cos, sin = cos_ref[...], sin_ref[...]
    cos_t, sin_t = cost_ref[...], sint_ref[...]

    def proj(a, b):
        return jnp.dot(h, wn_ref[:, a:b], preferred_element_type=F32)

    def proj_t(a, b):
        return _nt_dot(wt_ref[a:b, :], h)

    qt_ref[0] = (_rope_t(proj_t(_R_Q, _R_QI), cos_t, sin_t)
                 * (HEAD_DIM ** -0.5 * math.log2(math.e))).astype(BF16)
    qit_ref[0] = (_rope_t(proj_t(_R_QI, _R_V), cos_t, sin_t) * (IDX_DIM ** -0.5)).astype(BF16)
    vt = proj_t(_R_V, _R_WI).astype(BF16)
    ones = jnp.ones((V_ROWS - HEAD_DIM, vt.shape[1]), BF16)
    for hd in range(N_HEADS):
        vt_ref[0, 0, hd * V_ROWS:hd * V_ROWS + HEAD_DIM, :] = vt[hd * HEAD_DIM:(hd + 1) * HEAD_DIM, :]
        vt_ref[0, 0, hd * V_ROWS + HEAD_DIM:(hd + 1) * V_ROWS, :] = ones
    wit_ref[0] = proj_t(_R_WI, _R_END)[0:N_IDX_HEADS, :] * (N_IDX_HEADS ** -0.5)
    k_ref[0] = _rope(proj(_C_K, _C_KK), cos, sin).astype(BF16)
    kk_ref[0] = _rope(proj(_C_KK, _C_XR), cos, sin).astype(BF16)
    xr_ref[0] = proj(_C_XR, _C_XG)
    xg_ref[0] = proj(_C_XG, _C_END)


def _in_proj_call(x, mod, g, w_nat, w_tr, cos2, sin2, cos_t, sin_t):
    b, s, d = x.shape
    tm = min(K_BLOCK, s)
    nt = s // tm
    tile = lambda n: pl.BlockSpec((1, tm, n), lambda bi, ti: (bi, ti, 0))
    tile_t = lambda n: pl.BlockSpec((1, n, tm), lambda bi, ti: (bi, 0, ti))
    const = lambda shape: pl.BlockSpec(shape, lambda bi, ti: (0, 0))
    shp = lambda n, dt: jax.ShapeDtypeStruct((b, s, n), dt)
    shp_t = lambda n, dt: jax.ShapeDtypeStruct((b, n, s), dt)
    return pl.pallas_call(
        _in_proj_kernel,
        grid=(b, nt),
        in_specs=[tile(d),
                  pl.BlockSpec((1, N_MOD, d), lambda bi, ti: (bi, 0, 0)),
                  const((1, d)), const((d, _C_END)), const((_R_END, d)),
                  pl.BlockSpec((tm, LANES), lambda bi, ti: (ti, 0)),
                  pl.BlockSpec((tm, LANES), lambda bi, ti: (ti, 0)),
                  pl.BlockSpec((HEAD_DIM, tm), lambda bi, ti: (0, ti)),
                  pl.BlockSpec((HEAD_DIM, tm), lambda bi, ti: (0, ti))],
        out_specs=[tile_t(D_ATTN), tile_t(D_ATTN), tile_t(N_IDX_HEADS), tile(D_ATTN), tile(LANES),
                   pl.BlockSpec((1, 1, N_HEADS * V_ROWS, tm), lambda bi, ti: (bi, ti, 0, 0)),
                   tile(D_RNN), tile(D_RNN)],
        out_shape=[shp_t(D_ATTN, BF16), shp_t(D_ATTN, BF16), shp_t(N_IDX_HEADS, F32),
                   shp(D_ATTN, BF16), shp(LANES, BF16),
                   jax.ShapeDtypeStruct((b, nt, N_HEADS * V_ROWS, tm), BF16),
                   shp(D_RNN, F32), shp(D_RNN, F32)],
        compiler_params=_cparams(("parallel", "parallel")),
        name="in_proj",
    )(x, mod, g, w_nat, w_tr, cos2, sin2, cos_t, sin_t)


def _sortable(score):
    bits = pltpu.bitcast(score, I32)
    return jnp.where(bits < 0, jnp.int32(INT_MIN) - bits, bits)


def _tile_fold(x, op, rows):
    parts = [x[t * rows:(t + 1) * rows, :] for t in range(x.shape[0] // rows)]
    while len(parts) > 1:
        nxt = [op(parts[t], parts[t + 1]) for t in range(0, len(parts) - 1, 2)]
        if len(parts) % 2:
            nxt.append(parts[-1])
        parts = nxt
    return parts[0]


def _row_fold(x, op):
    acc = x[0:SUBLANES, :]
    for t in range(1, x.shape[0] // SUBLANES):
        acc = op(acc, x[t * SUBLANES:(t + 1) * SUBLANES, :])
    return acc


def _fold16(x):
    return _tile_fold(x, jnp.add, 2 * SUBLANES)


def _attn_kernel(qt_ref, qit_ref, wit_ref, kk_ref, k_ref, vt_ref, o_ref,
                 keys_ref, hi_ref, lo_ref, bias_ref, s_ref, p_ref, qm_ref, qim_ref, m_ref, l_ref,
                 acc_ref, *, n_sel, qb, kb, sub):
    i = pl.program_id(1)
    n_ch = ((i + 1) * qb + kb - 1) // kb
    q_pos = i * qb + lax.broadcasted_iota(I32, (1, qb), 1)
    limit = (q_pos // CHUNK + 1) * CHUNK
    krow = lax.broadcasted_iota(I32, (kb, 1), 0)
    row128 = lax.broadcasted_iota(I32, (LANES, 1), 0)

    for h in range(N_HEADS):
        rows = slice((h // 2) * LANES, (h // 2 + 1) * LANES)
        keep = (row128 < HEAD_DIM) if h % 2 == 0 else (row128 >= HEAD_DIM)
        qs, qis = qt_ref[0, rows, :], qit_ref[0, rows, :]
        qm_ref[h] = jnp.where(keep, qs, jnp.zeros_like(qs))
        qim_ref[h] = jnp.where(keep, qis, jnp.zeros_like(qis))
    m_ref[...] = jnp.full(m_ref.shape, NEG_INF, F32)
    l_ref[...] = jnp.zeros(l_ref.shape, F32)
    acc_ref[...] = jnp.zeros(acc_ref.shape, F32)

    wi = wit_ref[0]

    def score_body(j, carry, masked):
        kk = kk_ref[0, pl.ds(pl.multiple_of(j * kb, kb), kb), :]
        acc = jnp.zeros((kb, qb), F32)
        for h in range(N_IDX_HEADS):
            d = jnp.dot(kk, qim_ref[h], preferred_element_type=F32)
            acc = acc + wi[h:h + 1, :] * jnp.maximum(d, 0.0)
        key = _sortable(acc)
        if masked:
            key = jnp.where(j * kb + krow < limit, key, jnp.int32(INT_MIN))
        keys_ref[j] = key
        hi_ref[j] = (key >> 16).astype(I16)
        return carry

    n_open = (i * qb + CHUNK) // kb

    def open_pair(jj, carry):
        score_body(2 * jj, carry, masked=False)
        return score_body(2 * jj + 1, carry, masked=False)

    lax.fori_loop(0, n_open // 2, open_pair, 0)
    lax.fori_loop(2 * (n_open // 2), n_open, functools.partial(score_body, masked=False), 0)
    lax.fori_loop(n_open, n_ch, functools.partial(score_body, masked=True), 0)

    def count16(ref, pred):
        def one(j):
            return _fold16(jnp.where(pred(ref[j]), jnp.int16(1), jnp.int16(0)))

        def body(jj, acc):
            return acc + (one(2 * jj) + one(2 * jj + 1))

        acc = lax.fori_loop(0, n_ch // 2, body, jnp.zeros((2 * SUBLANES, qb), I16))
        acc = lax.cond(n_ch % 2 == 1, lambda a: a + one(n_ch - 1), lambda a: a, acc)
        return jnp.sum(acc.astype(I32), axis=0, keepdims=True)

    def search16(ref, target):
        c0 = count16(ref, lambda v: v >= jnp.int16(0))
        ok = c0 >= target
        init = (jnp.where(ok, jnp.int32(0), jnp.int32(INT16_MIN)),
                jnp.where(ok, c0, n_ch * kb), jnp.where(ok, 0, c0))

        def bit_body(bi, carry):
            t, n_ge, n_gt = carry
            cand = t | (jnp.int32(1) << (14 - bi))
            c = count16(ref, lambda v: v >= cand.astype(I16))
            ok = c >= target
            return jnp.where(ok, cand, t), jnp.where(ok, c, n_ge), jnp.where(ok, n_gt, c)

        return lax.fori_loop(0, 15, bit_body, init)

    tau_hi, _, n_gt_hi = search16(hi_ref, n_sel)
    tau_hi16 = tau_hi.astype(I16)

    def lo_body(j, carry):
        lo = (keys_ref[j] ^ jnp.int32(0x8000)).astype(I16)
        lo_ref[j] = jnp.where(hi_ref[j] == tau_hi16, lo, jnp.int16(INT16_MIN))
        return carry

    lax.fori_loop(0, n_ch, lo_body, 0)
    tau_lo, n_ge_lo, n_gt_lo = search16(lo_ref, n_sel - n_gt_hi)
    tau = (tau_hi << 16) | ((tau_lo + 32768) & jnp.int32(0xFFFF))
    all_sel = tau == jnp.int32(INT_MIN)
    need_i = jnp.where(all_sel, 0, n_sel - n_gt_hi - n_gt_lo)
    need = need_i.astype(F32)
    ties = jnp.max(jnp.where(all_sel, 0, (n_ge_lo - n_gt_lo) - need_i)) > 0
    thr = jnp.where(all_sel, jnp.int32(INT_MIN + 1), tau)

    n_sub = kb // sub
    no_max = jnp.full((SUBLANES, qb), NEG_INF, F32)

    def bias_ties(j, slot, run):
        r_i = lax.broadcasted_iota(I32, (kb, kb), 0)
        c_i = lax.broadcasted_iota(I32, (kb, kb), 1)
        tri = (c_i <= r_i).astype(BF16)
        kc = keys_ref[j]
        eq = kc == tau
        rank = jnp.dot(tri, eq.astype(BF16), preferred_element_type=F32) + run
        sel = (kc > tau) | (eq & (rank <= need))
        bias_ref[slot] = jnp.where(sel, 0.0, NEG_INF).astype(F32)
        return rank[kb - 1:kb, :]

    def bias_plain(j, slot, run):
        bias_ref[slot] = jnp.where(keys_ref[j] >= thr, 0.0, NEG_INF).astype(F32)
        return run

    gs = HEADS_PER_STAGE
    n_stage = N_HEADS // gs

    def logits(jb, bslot, h, r, mx):
        cols = slice((h // 2) * LANES, (h // 2 + 1) * LANES)
        rows = pl.ds(pl.multiple_of(jb * kb + r * sub, sub), sub)
        x = (jnp.dot(k_ref[0, rows, cols], qm_ref[h], preferred_element_type=F32)
             + bias_ref[bslot, r * sub:(r + 1) * sub, :])
        s_ref[(h // gs % 2) * gs + h % gs, r * sub:(r + 1) * sub, :] = x
        return jnp.maximum(mx, _row_fold(x, jnp.maximum))

    run0 = lax.cond(ties, bias_ties, bias_plain, 0, 0, jnp.zeros((1, qb), F32))
    mx0 = [no_max] * gs
    for r in range(n_sub):
        mx0 = [logits(0, 0, g, r, mx0[g]) for g in range(gs)]

    def attn_body(j, carry):
        run, mx = carry
        mx = list(mx)
        jn = jnp.minimum(j + 1, n_ch - 1)
        run = lax.cond(ties, bias_ties, bias_plain, jn, (j + 1) % 2, run)
        for st in range(n_stage):
            heads = [st * gs + g for g in range(gs)]
            m_new, m_safe, alpha = [], [], []
            for g, h in enumerate(heads):
                m_old = m_ref[h]
                m_new.append(jnp.maximum(m_old, jnp.max(mx[g], axis=0, keepdims=True)))
                m_safe.append(jnp.where(m_new[g] == NEG_INF, 0.0, m_new[g]))
                alpha.append(jnp.exp2(m_old - m_safe[g]))
            mx = [no_max] * gs
            for r in range(n_sub):
                tile = slice(r * sub, (r + 1) * sub)
                for g in range(gs):
                    if st + 1 < n_stage:
                        mx[g] = logits(j, j % 2, heads[g] + gs, r, mx[g])
                    else:
                        mx[g] = logits(jn, (j + 1) % 2, g, r, mx[g])
                for g in range(gs):
                    p_ref[g, tile, :] = jnp.exp2(
                        s_ref[(st % 2) * gs + g, tile, :] - m_safe[g]).astype(BF16)
            for g, h in enumerate(heads):
                pv = jnp.dot(vt_ref[0, j, h * V_ROWS:(h + 1) * V_ROWS, :], p_ref[g],
                             preferred_element_type=F32)
                l_ref[h] = alpha[g] * l_ref[h] + pv[HEAD_DIM:HEAD_DIM + 1, :]
                acc_ref[h] = alpha[g] * acc_ref[h] + pv[0:HEAD_DIM, :]
                m_ref[h] = m_new[g]
        return run, tuple(mx)

    lax.fori_loop(0, n_ch, attn_body, (run0, tuple(mx0)))

    outs = [acc_ref[h] / l_ref[h] for h in range(N_HEADS)]
    o_ref[0] = jnp.concatenate(outs, axis=0).T.astype(o_ref.dtype)


def _attn_call(qt, qit, wit, kk, k, vt):
    b, s, _ = k.shape
    qb = min(Q_BLOCK, s)
    kb = min(K_BLOCK, s)
    n_sel = min(TOPK_MAX, s // 4)
    blk_t = lambda n: pl.BlockSpec((1, n, qb), lambda bi, qi_: (bi, 0, qi_))
    once = pl.Buffered(1)
    kern = functools.partial(_attn_kernel, n_sel=n_sel, qb=qb, kb=kb, sub=min(K_SUB, kb))
    return pl.pallas_call(
        kern,
        grid=(b, s // qb),
        in_specs=[blk_t(D_ATTN), blk_t(D_ATTN), blk_t(N_IDX_HEADS),
                  pl.BlockSpec((1, s, LANES), lambda bi, qi_: (bi, 0, 0), pipeline_mode=once),
                  pl.BlockSpec((1, s, D_ATTN), lambda bi, qi_: (bi, 0, 0), pipeline_mode=once),
                  pl.BlockSpec((1, s // kb, N_HEADS * V_ROWS, kb), lambda bi, qi_: (bi, 0, 0, 0),
                               pipeline_mode=once)],
        out_specs=pl.BlockSpec((1, qb, D_ATTN), lambda bi, qi_: (bi, qi_, 0)),
        out_shape=jax.ShapeDtypeStruct((b, s, D_ATTN), BF16),
        scratch_shapes=[pltpu.VMEM((s // kb, kb, qb), I32),
                        pltpu.VMEM((s // kb, kb, qb), I16),
                        pltpu.VMEM((s // kb, kb, qb), I16),
                        pltpu.VMEM((2, kb, qb), F32),
                        pltpu.VMEM((2 * HEADS_PER_STAGE, kb, qb), F32),
                        pltpu.VMEM((HEADS_PER_STAGE, kb, qb), BF16),
                        pltpu.VMEM((N_HEADS, LANES, qb), BF16),
                        pltpu.VMEM((N_IDX_HEADS, LANES, qb), BF16),
                        pltpu.VMEM((N_HEADS, 1, qb), F32),
                        pltpu.VMEM((N_HEADS, 1, qb), F32),
                        pltpu.VMEM((N_HEADS, HEAD_DIM, qb), F32)],
        compiler_params=_cparams(("parallel", "arbitrary")),
        name="attn",
    )(qt, qit, wit, kk, k, vt)


def _gelu_tanh(x):
    return 0.5 * x * (1.0 + jnp.tanh(math.sqrt(2.0 / math.pi) * (x + 0.044715 * (x * x * x))))


def _rglru_kernel(xr_ref, xg_ref, cw_ref, cb_ref, wa_ref, ba_ref, wx_ref, bx_ref, lam_ref,
                  o_ref, xbuf_ref, h_ref, *, ts):
    t = pl.program_id(1)
    pad = 8

    @pl.when(t == 0)
    def _():
        xbuf_ref[0:pad, :] = jnp.zeros((pad, D_RNN), F32)
        h_ref[...] = jnp.zeros_like(h_ref)

    xr = xr_ref[0]
    xbuf_ref[pad:pad + ts, :] = xr
    conv = cb_ref[...] + cw_ref[CONV_WIDTH - 1:CONV_WIDTH, :] * xr
    for w in range(CONV_WIDTH - 1):
        off = pad - (CONV_WIDTH - 1) + w
        conv = conv + cw_ref[w:w + 1, :] * xbuf_ref[off:off + ts, :]
    xbuf_ref[0:pad, :] = xr[ts - pad:ts, :]

    cb16 = conv.astype(BF16)
    r = jax.nn.sigmoid(jnp.dot(cb16, wa_ref[...], preferred_element_type=F32) + ba_ref[...])
    ig = jax.nn.sigmoid(jnp.dot(cb16, wx_ref[...], preferred_element_type=F32) + bx_ref[...])
    z = -lam_ref[...]
    softplus = jnp.maximum(z, 0.0) + jnp.log1p(jnp.exp(-jnp.abs(z)))
    log_a = (-RG_C) * r * softplus
    a = jnp.exp(log_a)
    bb = jnp.sqrt(1.0 - a * a) * (ig * conv)

    row = lax.broadcasted_iota(I32, (ts, 1), 0)
    d = 1
    while d < ts:
        if d < SUBLANES:
            live = row >= d
            a_s = jnp.where(live, pltpu.roll(a, d, axis=0), 1.0)
            b_s = jnp.where(live, pltpu.roll(bb, d, axis=0), 0.0)
            bb = bb + a * b_s
            a = a * a_s
        else:
            lo_a, hi_a = a[:d, :], a[d:, :]
            bb = jnp.concatenate([bb[:d, :], bb[d:, :] + hi_a * bb[:ts - d, :]], axis=0)
            a = jnp.concatenate([lo_a, hi_a * a[:ts - d, :]], axis=0)
        d *= 2
    h = bb + a * h_ref[0:1, :]
    h_ref[0:1, :] = h[ts - 1:ts, :]
    o_ref[0] = (h * _gelu_tanh(xg_ref[0])).astype(o_ref.dtype)


def _rglru_call(xr, xg, conv_w, conv_b, wa, ba, wx, bx, lam):
    b, s, r = xr.shape
    ts = min(TOKEN_TILE, s)
    tile = pl.BlockSpec((1, ts, r), lambda bi, ti: (bi, ti, 0))
    const = lambda shape: pl.BlockSpec(shape, lambda bi, ti: (0, 0))
    return pl.pallas_call(
        functools.partial(_rglru_kernel, ts=ts),
        grid=(b, s // ts),
        in_specs=[tile, tile, const((CONV_WIDTH, r)), const((1, r)), const((r, r)), const((1, r)),
                  const((r, r)), const((1, r)), const((1, r))],
        out_specs=tile,
        out_shape=jax.ShapeDtypeStruct((b, s, r), BF16),
        scratch_shapes=[pltpu.VMEM((ts + 8, r), F32), pltpu.VMEM((8, r), F32)],
        compiler_params=_cparams(("parallel", "arbitrary")),
        name="rglru",
    )(xr, xg, conv_w, conv_b, wa, ba, wx, bx, lam)


def _route(logits):
    lane = lax.broadcasted_iota(I32, logits.shape, 1)
    is_g = (lane >= N_EXPERTS) & (lane < N_EXPERTS + N_GROUPS)
    big = jnp.int32(LANES)

    def first_lane(mask):
        return jnp.min(jnp.where(mask, lane, big), axis=1, keepdims=True)

    gl = jnp.where(is_g, logits, NEG_INF)
    ge = jnp.exp(gl - jnp.max(gl, axis=1, keepdims=True))
    p_groups = ge / jnp.sum(ge, axis=1, keepdims=True)
    p_g = jnp.max(p_groups, axis=1, keepdims=True)
    g_idx = first_lane(is_g & (p_groups == p_g)) - N_EXPERTS

    in_grp = (lane < N_EXPERTS) & ((lane // EXP_PER_GROUP) == g_idx)
    el = jnp.where(in_grp, logits, NEG_INF)
    ee = jnp.exp(el - jnp.max(el, axis=1, keepdims=True))
    es = ee / jnp.sum(ee, axis=1, keepdims=True)
    es = jnp.where(in_grp, es, NEG_INF)
    v1 = jnp.max(es, axis=1, keepdims=True)
    i1 = first_lane(in_grp & (es == v1))
    es2 = jnp.where(lane == i1, NEG_INF, es)
    v2 = jnp.max(es2, axis=1, keepdims=True)
    i2 = first_lane(in_grp & (es2 == v2))
    tot = v1 + v2
    e_w = jnp.where(lane == i1, v1 / tot, 0.0) + jnp.where(lane == i2, v2 / tot, 0.0)
    return p_g * e_w + jnp.where(lane == ROUTE_GROUP_LANE + g_idx, 1.0, 0.0)


def _out_proj_kernel(ya_ref, yr_ref, x_ref, mod_ref, wo_ref, gpm_ref, gpf_ref, wr_ref, br_ref,
                     x1_ref, h2_ref, comb_ref):
    mix = (jnp.dot(ya_ref[0], wo_ref[0:D_ATTN, :], preferred_element_type=F32)
           + jnp.dot(yr_ref[0], wo_ref[D_ATTN:D_ATTN + D_RNN, :], preferred_element_type=F32))
    gate1 = mod_ref[0, 2:3, :]
    shift2 = mod_ref[0, 3:4, :]
    scale2 = mod_ref[0, 4:5, :]
    x1 = x_ref[0] + gate1 * _rms(mix, gpm_ref[...])
    x1_ref[0] = x1
    h2 = _rms(x1, gpf_ref[...]) * (1.0 + scale2) + shift2
    h_hi = h2.astype(BF16)
    h2_ref[0] = h_hi
    h_lo = (h2 - h_hi.astype(F32)).astype(BF16)
    part = jnp.dot(h_hi, wr_ref[...], preferred_element_type=F32)
    logits = (part[:, 0:LANES] + (part[:, LANES:2 * LANES]
              + jnp.dot(h_lo, wr_ref[:, 0:LANES], preferred_element_type=F32))) + br_ref[...]
    comb_ref[0] = _route(logits)


def _out_proj_call(ya, yr, x, mod, wo, gpm, gpf, wr, br):
    b, s, d = x.shape
    tm = min(TOKEN_TILE, s)
    tile = lambda n: pl.BlockSpec((1, tm, n), lambda bi, ti: (bi, ti, 0))
    const = lambda shape: pl.BlockSpec(shape, lambda bi, ti: (0,) * len(shape))
    return pl.pallas_call(
        _out_proj_kernel,
        grid=(b, s // tm),
        in_specs=[tile(D_ATTN), tile(D_RNN), tile(d),
                  pl.BlockSpec((1, N_MOD, d), lambda bi, ti: (bi, 0, 0)),
                  const((D_ATTN + D_RNN, d)), const((1, d)), const((1, d)),
                  const((d, 2 * LANES)), const((1, LANES))],
        out_specs=[tile(d), tile(d), tile(LANES)],
        out_shape=[jax.ShapeDtypeStruct((b, s, d), F32), jax.ShapeDtypeStruct((b, s, d), BF16),
                   jax.ShapeDtypeStruct((b, s, LANES), F32)],
        compiler_params=_cparams(("parallel", "parallel")),
        name="out_proj",
    )(ya, yr, x, mod, wo, gpm, gpf, wr, br)


def _moe_kernel(h2_ref, comb_ref, x1_ref, mod_ref, wg_ref, wu_ref, wd_ref, gpo_ref,
                o_ref, hs_ref, cs_ref, ys_ref, act_ref, slot_ref, seg_ref, *, tm, ns, rb):
    g = pl.program_id(2)
    lane = lax.broadcasted_iota(I32, (1, LANES), 1)
    n_blk = tm // LANES

    @pl.when(g == 0)
    def _dispatch():
        comb = comb_ref[0]
        oh = jnp.where((lane >= ROUTE_GROUP_LANE) & (lane < ROUTE_GROUP_LANE + N_GROUPS), comb, 0.0)
        oh16 = oh.astype(BF16)
        sub = lax.broadcasted_iota(I32, (LANES, 1), 0)
        base = jnp.int32(0)
        base_lane = jnp.zeros((1, LANES), F32)
        base_sub = jnp.zeros((LANES, 1), F32)
        for gg in range(N_GROUPS):
            here = ROUTE_GROUP_LANE + gg
            n = jnp.sum(jnp.where(lane == here, oh, 0.0)).astype(I32)
            nb = (n + rb - 1) // rb
            seg_ref[gg] = base
            seg_ref[N_GROUPS + gg] = nb
            base_f = base.astype(F32)
            base_lane = base_lane + jnp.where(lane == here, base_f, 0.0)
            base_sub = base_sub + jnp.where(sub == here, base_f, 0.0)
            base = base + nb * rb

        tok_c = lax.broadcasted_iota(I32, (1, tm), 1)
        tok_r = lax.broadcasted_iota(I32, (tm, 1), 0)
        blk_r = lax.broadcasted_iota(I32, (LANES, 1), 0)
        blk_c = lax.broadcasted_iota(I32, (1, LANES), 1)
        eye = (blk_r == blk_c).astype(BF16)
        oht16 = _nt_dot(eye, oh16).astype(BF16)
        slot_row = []
        for t in range(n_blk):
            tril = (tok_c <= blk_r + t * LANES).astype(BF16)
            rank = jnp.dot(tril, oh16, preferred_element_type=F32)
            oh_blk = oh[t * LANES:(t + 1) * LANES, :]
            slot_ref[t * LANES:(t + 1) * LANES, :] = jnp.sum(
                oh_blk * (base_lane + rank - 1.0), axis=1, keepdims=True).astype(I32)
            triu = (tok_r <= blk_c + t * LANES).astype(BF16)
            rank_t = jnp.dot(oht16, triu, preferred_element_type=F32)
            oht_blk = oht16[:, t * LANES:(t + 1) * LANES].astype(F32)
            slot_row.append(jnp.sum(oht_blk * (base_sub + rank_t - 1.0), axis=0,
                                    keepdims=True).astype(I32))
        slot_row = jnp.concatenate(slot_row, axis=1)

        hi = comb.astype(BF16)
        lo = (comb - hi.astype(F32)).astype(BF16)
        cw = jnp.concatenate([hi, lo], axis=1)
        h = h2_ref[0]
        for r in range(ns // LANES):
            rows = slice(r * LANES, (r + 1) * LANES)
            p = (blk_r + r * LANES == slot_row).astype(BF16)
            hs_ref[rows, :] = jnp.dot(p, h, preferred_element_type=F32).astype(BF16)
            cw_sorted = jnp.dot(p, cw, preferred_element_type=F32)
            cs_ref[rows, :] = cw_sorted[:, 0:LANES] + cw_sorted[:, LANES:2 * LANES]
        ys_ref[...] = jnp.zeros_like(ys_ref)

    base = seg_ref[g]

    def block(r, carry):
        rows = pl.ds(pl.multiple_of(base + r * rb, rb), rb)
        hb = hs_ref[rows, :]
        cb = cs_ref[rows, :]
        for e in range(EXP_PER_GROUP):
            gt = jnp.dot(hb, wg_ref[0, e], preferred_element_type=F32)
            up = jnp.dot(hb, wu_ref[0, e], preferred_element_type=F32)
            cols = slice(e * D_EXPERT, (e + 1) * D_EXPERT)
            scale = jnp.sum(jnp.where(lane == g * EXP_PER_GROUP + e, cb, 0.0), axis=1, keepdims=True)
            act_ref[:, cols] = (gt * jax.nn.sigmoid(gt) * up * scale).astype(BF16)
        ys_ref[rows, :] = jnp.dot(act_ref[...], wd_ref[0], preferred_element_type=F32).astype(BF16)
        return carry

    lax.fori_loop(0, seg_ref[N_GROUPS + g], block, 0)

    @pl.when(g == N_GROUPS - 1)
    def _combine():
        gate2 = mod_ref[0, 5:6, :]
        slot_c = lax.broadcasted_iota(I32, (1, ns), 1)
        half = tm // 2
        for hh in range(2):
            rows = slice(hh * half, (hh + 1) * half)
            pt = (slot_c == slot_ref[rows, :]).astype(BF16)
            y = jnp.dot(pt, ys_ref[...], preferred_element_type=F32)
            o_ref[0, rows, :] = x1_ref[0, rows, :] + gate2 * _rms(y, gpo_ref[...])


def _moe_call(h2, comb, x1, mod, wg, wu, wd, gpo):
    b, s, d = x1.shape
    tm = min(MOE_TILE, s)
    rb = MOE_ROWS
    ns = tm + N_GROUPS * rb
    tile = lambda n, **kw: pl.BlockSpec((1, tm, n), lambda bi, ti, gi: (bi, ti, 0), **kw)
    kern = functools.partial(_moe_kernel, tm=tm, ns=ns, rb=rb)
    return pl.pallas_call(
        kern,
        grid=(b, s // tm, N_GROUPS),
        in_specs=[tile(d), tile(LANES), tile(d, pipeline_mode=pl.Buffered(1)),
                  pl.BlockSpec((1, N_MOD, d), lambda bi, ti, gi: (bi, 0, 0)),
                  pl.BlockSpec((1, EXP_PER_GROUP, d, D_EXPERT), lambda bi, ti, gi: (gi, 0, 0, 0)),
                  pl.BlockSpec((1, EXP_PER_GROUP, d, D_EXPERT), lambda bi, ti, gi: (gi, 0, 0, 0)),
                  pl.BlockSpec((1, EXP_PER_GROUP * D_EXPERT, d), lambda bi, ti, gi: (gi, 0, 0)),
                  pl.BlockSpec((1, d), lambda bi, ti, gi: (0, 0))],
        out_specs=tile(d),
        out_shape=jax.ShapeDtypeStruct((b, s, d), F32),
        scratch_shapes=[pltpu.VMEM((ns, d), BF16),
                        pltpu.VMEM((ns, LANES), F32),
                        pltpu.VMEM((ns, d), BF16),
                        pltpu.VMEM((rb, EXP_PER_GROUP * D_EXPERT), BF16),
                        pltpu.VMEM((tm, 1), I32),
                        pltpu.SMEM((2 * N_GROUPS,), I32)],
        compiler_params=_cparams(("parallel", "parallel", "arbitrary")),
        name="moe",
    )(h2, comb, x1, mod, wg, wu, wd, gpo)


def _rope_tables(seq_len):
    pos = jnp.arange(seq_len, dtype=F32)
    inv = ROPE_THETA ** (-jnp.arange(0, HEAD_DIM, 2, dtype=F32) / HEAD_DIM)
    ang = pos[:, None] * inv[None, :]
    cos, sin = jnp.cos(ang), jnp.sin(ang)
    cos2 = jnp.concatenate([cos, cos, cos, cos], axis=1)
    sin2 = jnp.concatenate([-sin, sin, -sin, sin], axis=1)
    return cos2, sin2, cos2[:, :HEAD_DIM].T, sin2[:, :HEAD_DIM].T


def _block_diag(w):
    n, c, d = w.shape
    eye = jnp.eye(n, dtype=w.dtype)
    return (eye[:, None, :, None] * w[:, :, None, :]).reshape(n * c, n * d)


def _pad_cols(w, n):
    return jnp.pad(w, ((0, 0), (0, n - w.shape[1])))


def kernel(x, c, w_ada, b_ada, g_pre_mix, g_post_mix, g_pre_ffn, g_post_ffn, w_in, conv_w, conv_b, w_rg_a, b_rg_a, w_rg_x, b_rg_x, lru_lambda, w_out, w_router_group, b_router_group, w_router_expert, b_router_expert, w_gate, w_up, w_down):
    b, s, d = x.shape
    depth = w_ada.shape[0]
    cos2, sin2, cos_t, sin_t = _rope_tables(s)
    c_pad = jnp.pad(c, ((0, (-b) % SUBLANES), (0, 0)))
    o_q, o_k, o_v, o_qi, o_ki, o_wi, o_xr, o_xg = 0, 512, 1024, 1536, 2048, 2112, 2120, 2632
    for l in range(depth):
        wl = w_in[l]
        w_ki = wl[:, o_ki:o_wi]
        w_nat = jnp.concatenate([wl[:, o_k:o_v], w_ki, w_ki, wl[:, o_xr:o_xg], wl[:, o_xg:]],
                                axis=1).astype(BF16)
        w_tr = jnp.concatenate([wl[:, o_q:o_k], wl[:, o_qi:o_ki], wl[:, o_v:o_qi],
                                _pad_cols(wl[:, o_wi:o_xr], _R_END - _R_WI)], axis=1).T.astype(BF16)
        w_route = _pad_cols(jnp.concatenate([w_router_expert[l], w_router_group[l]], axis=1), LANES)
        w_route_hi = w_route.astype(BF16)
        w_route = jnp.concatenate(
            [w_route_hi, (w_route - w_route_hi.astype(F32)).astype(BF16)], axis=1)
        b_route = _pad_cols(jnp.concatenate([b_router_expert[l], b_router_group[l]])[None, :], LANES)

        mod = _mod_call(c_pad, w_ada[l], b_ada[l][None, :])[:b].reshape(b, N_MOD, d)
        qt, qit, wit, k, kk, vt, xr, xg = _in_proj_call(
            x, mod, g_pre_mix[l][None, :], w_nat, w_tr, cos2, sin2, cos_t, sin_t)
        y_attn = _attn_call(qt, qit, wit, kk, k, vt)
        y_rnn = _rglru_call(xr, xg, conv_w[l], conv_b[l][None, :],
                            _block_diag(w_rg_a[l]).astype(BF16), b_rg_a[l][None, :],
                            _block_diag(w_rg_x[l]).astype(BF16), b_rg_x[l][None, :],
                            lru_lambda[l][None, :])
        x1, h2, comb = _out_proj_call(y_attn, y_rnn, x, mod, w_out[l].astype(BF16),
                                      g_post_mix[l][None, :], g_pre_ffn[l][None, :], w_route, b_route)
        x = _moe_call(h2, comb, x1, mod, w_gate[l].astype(BF16), w_up[l].astype(BF16),
                      w_down[l].reshape(N_GROUPS, EXP_PER_GROUP * D_EXPERT, d).astype(BF16),
                      g_post_ffn[l][None, :])
    return x
```

```python
import functools
import math

import jax
import jax.numpy as jnp
from jax import lax
from jax.experimental import pallas as pl
from jax.experimental.pallas import tpu as pltpu

F32 = jnp.float32
BF16 = jnp.bfloat16
I32 = jnp.int32
I16 = jnp.int16

D_MODEL = 1024
CHUNK = 64
ROPE_THETA = 10000.0
EPS = 1e-6
N_HEADS = 8
HEAD_DIM = 64
D_ATTN = N_HEADS * HEAD_DIM
N_IDX_HEADS = 8
IDX_DIM = 64
TOPK_MAX = 256
D_RNN = 512
N_RNN_BLOCKS = 8
CONV_WIDTH = 4
RG_C = 8.0
N_GROUPS = 4
EXP_PER_GROUP = 8
N_EXPERTS = N_GROUPS * EXP_PER_GROUP
D_EXPERT = 256
N_MOD = 6

LANES = 128
SUBLANES = 8
INT_MIN = -2 ** 31
INT16_MIN = -2 ** 15
NEG_INF = float("-inf")

Q_BLOCK = 256
K_BLOCK = 512
K_SUB = 256
HEADS_PER_STAGE = 2
V_ROWS = HEAD_DIM + 16
IN_TILE = 1024
RNN_TILE = 512
OUT_TILE = 1024
MOE_TILE = 1024
MOE_ROWS = 128
ROUTE_GROUP_LANE = 64
VMEM_LIMIT = 60 * 1024 * 1024


def _cparams(sem):
    return pltpu.CompilerParams(dimension_semantics=sem, vmem_limit_bytes=VMEM_LIMIT)


def _nt_dot(a, b):
    return lax.dot_general(a, b, (((1,), (1,)), ((), ())), preferred_element_type=F32)


def _rms(x, g):
    return x * lax.rsqrt(jnp.mean(x * x, axis=-1, keepdims=True) + EPS) * g


def _mod_kernel(c_ref, w_ref, b_ref, o_ref):
    c = c_ref[...]
    sc = c * jax.nn.sigmoid(c)
    o_ref[...] = jnp.dot(sc, w_ref[...], preferred_element_type=F32,
                         precision=lax.Precision.HIGHEST) + b_ref[...]


def _mod_call(c_pad, w_ada, b_ada):
    rows, d = c_pad.shape
    n = w_ada.shape[1]
    bn = 1024
    return pl.pallas_call(
        _mod_kernel,
        grid=(n // bn,),
        in_specs=[pl.BlockSpec((rows, d), lambda j: (0, 0)),
                  pl.BlockSpec((d, bn), lambda j: (0, j)),
                  pl.BlockSpec((1, bn), lambda j: (0, j))],
        out_specs=pl.BlockSpec((rows, bn), lambda j: (0, j)),
        out_shape=jax.ShapeDtypeStruct((rows, n), F32),
        compiler_params=_cparams(("arbitrary",)),
        name="mod",
    )(c_pad, w_ada, b_ada)


_R_Q, _R_QI, _R_V, _R_WI, _R_END = 0, 512, 1024, 1536, 1552
_C_K, _C_KK, _C_XR, _C_XG, _C_END = 0, 512, 640, 1152, 1664


def _rope(y, cos, sin):
    lane = lax.broadcasted_iota(I32, (1, LANES), 1)
    low = (lane % HEAD_DIM) < (HEAD_DIM // 2)
    outs = []
    for p in range(y.shape[1] // LANES):
        s = y[:, p * LANES:(p + 1) * LANES]
        swapped = jnp.where(low, pltpu.roll(s, LANES - HEAD_DIM // 2, axis=1),
                            pltpu.roll(s, HEAD_DIM // 2, axis=1))
        outs.append(s * cos + swapped * sin)
    return outs[0] if len(outs) == 1 else jnp.concatenate(outs, axis=1)


def _rope_t(y, cos, sin):
    half = HEAD_DIM // 2
    outs = []
    for hd in range(y.shape[0] // HEAD_DIM):
        blk = y[hd * HEAD_DIM:(hd + 1) * HEAD_DIM, :]
        swapped = jnp.concatenate([blk[half:, :], blk[:half, :]], axis=0)
        outs.append(blk * cos + swapped * sin)
    return jnp.concatenate(outs, axis=0)


def _in_proj_kernel(x_ref, mod_ref, g_ref, wn_ref, wt_ref, cos_ref, sin_ref, cost_ref, sint_ref,
                    qt_ref, qit_ref, wit_ref, k_ref, kk_ref, vt_ref, xr_ref, xg_ref):
    x = x_ref[0]
    shift = mod_ref[0, 0:1, :]
    scale = mod_ref[0, 1:2, :]
    h = (_rms(x, g_ref[...]) * (1.0 + scale) + shift).astype(BF16)
    cos, sin = cos_ref[...], sin_ref[...]
    cos_t, sin_t = cost_ref[...], sint_ref[...]

    def proj(a, b):
        return jnp.dot(h, wn_ref[:, a:b], preferred_element_type=F32)

    def proj_t(a, b):
        return _nt_dot(wt_ref[a:b, :], h)

    qt_ref[0] = (_rope_t(proj_t(_R_Q, _R_QI), cos_t, sin_t)
                 * (HEAD_DIM ** -0.5 * math.log2(math.e))).astype(BF16)
    qit_ref[0] = (_rope_t(proj_t(_R_QI, _R_V), cos_t, sin_t) * (IDX_DIM ** -0.5)).astype(BF16)
    vt = proj_t(_R_V, _R_WI).astype(BF16)
    ones = jnp.ones((V_ROWS - HEAD_DIM, vt.shape[1]), BF16)
    kb = vt_ref.shape[3]
    for c in range(vt_ref.shape[1]):
        for hd in range(N_HEADS):
            vt_ref[0, c, hd * V_ROWS:hd * V_ROWS + HEAD_DIM, :] = (
                vt[hd * HEAD_DIM:(hd + 1) * HEAD_DIM, c * kb:(c + 1) * kb])
            vt_ref[0, c, hd * V_ROWS + HEAD_DIM:(hd + 1) * V_ROWS, :] = ones[:, 0:kb]
    wit_ref[0] = proj_t(_R_WI, _R_END)[0:N_IDX_HEADS, :] * (N_IDX_HEADS ** -0.5)
    k_ref[0] = _rope(proj(_C_K, _C_KK), cos, sin).astype(BF16)
    kk_ref[0] = _rope(proj(_C_KK, _C_XR), cos, sin).astype(BF16)
    xr_ref[0] = proj(_C_XR, _C_XG)
    xg_ref[0] = proj(_C_XG, _C_END)


def _in_proj_call(x, mod, g, w_nat, w_tr, cos2, sin2, cos_t, sin_t):
    b, s, d = x.shape
    kb = min(K_BLOCK, s)
    tm = min(IN_TILE, s)
    nkb = tm // kb
    nt = s // tm
    tile = lambda n: pl.BlockSpec((1, tm, n), lambda bi, ti: (bi, ti, 0))
    tile_t = lambda n: pl.BlockSpec((1, n, tm), lambda bi, ti: (bi, 0, ti))
    const = lambda shape: pl.BlockSpec(shape, lambda bi, ti: (0, 0))
    shp = lambda n, dt: jax.ShapeDtypeStruct((b, s, n), dt)
    shp_t = lambda n, dt: jax.ShapeDtypeStruct((b, n, s), dt)
    return pl.pallas_call(
        _in_proj_kernel,
        grid=(b, nt),
        in_specs=[tile(d),
                  pl.BlockSpec((1, N_MOD, d), lambda bi, ti: (bi, 0, 0)),
                  const((1, d)), const((d, _C_END)), const((_R_END, d)),
                  pl.BlockSpec((tm, LANES), lambda bi, ti: (ti, 0)),
                  pl.BlockSpec((tm, LANES), lambda bi, ti: (ti, 0)),
                  pl.BlockSpec((HEAD_DIM, tm), lambda bi, ti: (0, ti)),
                  pl.BlockSpec((HEAD_DIM, tm), lambda bi, ti: (0, ti))],
        out_specs=[tile_t(D_ATTN), tile_t(D_ATTN), tile_t(N_IDX_HEADS), tile(D_ATTN), tile(LANES),
                   pl.BlockSpec((1, nkb, N_HEADS * V_ROWS, kb), lambda bi, ti: (bi, ti, 0, 0)),
                   tile(D_RNN), tile(D_RNN)],
        out_shape=[shp_t(D_ATTN, BF16), shp_t(D_ATTN, BF16), shp_t(N_IDX_HEADS, F32),
                   shp(D_ATTN, BF16), shp(LANES, BF16),
                   jax.ShapeDtypeStruct((b, s // kb, N_HEADS * V_ROWS, kb), BF16),
                   shp(D_RNN, F32), shp(D_RNN, F32)],
        compiler_params=_cparams(("parallel", "parallel")),
        name="in_proj",
    )(x, mod, g, w_nat, w_tr, cos2, sin2, cos_t, sin_t)


def _sortable(score):
    bits = pltpu.bitcast(score, I32)
    return jnp.where(bits < 0, jnp.int32(INT_MIN) - bits, bits)


def _tile_fold(x, op, rows):
    parts = [x[t * rows:(t + 1) * rows, :] for t in range(x.shape[0] // rows)]
    while len(parts) > 1:
        nxt = [op(parts[t], parts[t + 1]) for t in range(0, len(parts) - 1, 2)]
        if len(parts) % 2:
            nxt.append(parts[-1])
        parts = nxt
    return parts[0]


def _row_fold(x, op):
    acc = x[0:SUBLANES, :]
    for t in range(1, x.shape[0] // SUBLANES):
        acc = op(acc, x[t * SUBLANES:(t + 1) * SUBLANES, :])
    return acc


def _fold16(x):
    return _tile_fold(x, jnp.add, 2 * SUBLANES)


def _attn_kernel(qt_ref, qit_ref, wit_ref, kk_ref, k_ref, vt_ref, o_ref,
                 keys_ref, hi_ref, lo_ref, bias_ref, s_ref, p_ref, qm_ref, qim_ref, m_ref, l_ref,
                 acc_ref, *, n_sel, qb, kb, sub):
    i = pl.program_id(1)
    n_ch = ((i + 1) * qb + kb - 1) // kb
    q_pos = i * qb + lax.broadcasted_iota(I32, (1, qb), 1)
    limit = (q_pos // CHUNK + 1) * CHUNK
    krow = lax.broadcasted_iota(I32, (kb, 1), 0)
    row128 = lax.broadcasted_iota(I32, (LANES, 1), 0)

    for h in range(N_HEADS):
        rows = slice((h // 2) * LANES, (h // 2 + 1) * LANES)
        keep = (row128 < HEAD_DIM) if h % 2 == 0 else (row128 >= HEAD_DIM)
        qs, qis = qt_ref[0, rows, :], qit_ref[0, rows, :]
        qm_ref[h] = jnp.where(keep, qs, jnp.zeros_like(qs))
        qim_ref[h] = jnp.where(keep, qis, jnp.zeros_like(qis))
    m_ref[...] = jnp.full(m_ref.shape, NEG_INF, F32)
    l_ref[...] = jnp.zeros(l_ref.shape, F32)
    acc_ref[...] = jnp.zeros(acc_ref.shape, F32)

    wi = wit_ref[0]

    def score_body(j, carry, masked):
        kk = kk_ref[0, pl.ds(pl.multiple_of(j * kb, kb), kb), :]
        acc = jnp.zeros((kb, qb), F32)
        for h in range(N_IDX_HEADS):
            d = jnp.dot(kk, qim_ref[h], preferred_element_type=F32)
            acc = acc + wi[h:h + 1, :] * jnp.maximum(d, 0.0)
        key = _sortable(acc)
        if masked:
            key = jnp.where(j * kb + krow < limit, key, jnp.int32(INT_MIN))
        keys_ref[j] = key
        hi_ref[j] = (key >> 16).astype(I16)
        return carry

    n_open = (i * qb + CHUNK) // kb

    def open_pair(jj, carry):
        score_body(2 * jj, carry, masked=False)
        return score_body(2 * jj + 1, carry, masked=False)

    lax.fori_loop(0, n_open // 2, open_pair, 0)
    lax.fori_loop(2 * (n_open // 2), n_open, functools.partial(score_body, masked=False), 0)
    lax.fori_loop(n_open, n_ch, functools.partial(score_body, masked=True), 0)

    def count16(ref, pred):
        def one(j):
            return _fold16(jnp.where(pred(ref[j]), jnp.int16(1), jnp.int16(0)))

        def body(jj, acc):
            return acc + (one(2 * jj) + one(2 * jj + 1))

        acc = lax.fori_loop(0, n_ch // 2, body, jnp.zeros((2 * SUBLANES, qb), I16))
        acc = lax.cond(n_ch % 2 == 1, lambda a: a + one(n_ch - 1), lambda a: a, acc)
        return jnp.sum(acc.astype(I32), axis=0, keepdims=True)

    def search16(ref, target):
        c0 = count16(ref, lambda v: v >= jnp.int16(0))
        ok = c0 >= target
        init = (jnp.where(ok, jnp.int32(0), jnp.int32(INT16_MIN)),
                jnp.where(ok, c0, n_ch * kb), jnp.where(ok, 0, c0))

        def bit_body(bi, carry):
            t, n_ge, n_gt = carry
            cand = t | (jnp.int32(1) << (14 - bi))
            c = count16(ref, lambda v: v >= cand.astype(I16))
            ok = c >= target
            return jnp.where(ok, cand, t), jnp.where(ok, c, n_ge), jnp.where(ok, n_gt, c)

        return lax.fori_loop(0, 15, bit_body, init)

    tau_hi, _, n_gt_hi = search16(hi_ref, n_sel)
    tau_hi16 = tau_hi.astype(I16)

    def lo_body(j, carry):
        lo = (keys_ref[j] ^ jnp.int32(0x8000)).astype(I16)
        lo_ref[j] = jnp.where(hi_ref[j] == tau_hi16, lo, jnp.int16(INT16_MIN))
        return carry

    lax.fori_loop(0, n_ch, lo_body, 0)
    tau_lo, n_ge_lo, n_gt_lo = search16(lo_ref, n_sel - n_gt_hi)
    tau = (tau_hi << 16) | ((tau_lo + 32768) & jnp.int32(0xFFFF))
    all_sel = tau == jnp.int32(INT_MIN)
    need_i = jnp.where(all_sel, 0, n_sel - n_gt_hi - n_gt_lo)
    need = need_i.astype(F32)
    ties = jnp.max(jnp.where(all_sel, 0, (n_ge_lo - n_gt_lo) - need_i)) > 0
    thr = jnp.where(all_sel, jnp.int32(INT_MIN + 1), tau)

    n_sub = kb // sub
    no_max = jnp.full((SUBLANES, qb), NEG_INF, F32)

    def bias_ties(j, slot, run):
        r_i = lax.broadcasted_iota(I32, (kb, kb), 0)
        c_i = lax.broadcasted_iota(I32, (kb, kb), 1)
        tri = (c_i <= r_i).astype(BF16)
        kc = keys_ref[j]
        eq = kc == tau
        rank = jnp.dot(tri, eq.astype(BF16), preferred_element_type=F32) + run
        sel = (kc > tau) | (eq & (rank <= need))
        bias_ref[slot] = jnp.where(sel, 0.0, NEG_INF).astype(F32)
        return rank[kb - 1:kb, :]

    def bias_plain(j, slot, run):
        bias_ref[slot] = jnp.where(keys_ref[j] >= thr, 0.0, NEG_INF).astype(F32)
        return run

    gs = HEADS_PER_STAGE
    n_stage = N_HEADS // gs

    def logits(jb, bslot, h, r, mx):
        cols = slice((h // 2) * LANES, (h // 2 + 1) * LANES)
        rows = pl.ds(pl.multiple_of(jb * kb + r * sub, sub), sub)
        x = (jnp.dot(k_ref[0, rows, cols], qm_ref[h], preferred_element_type=F32)
             + bias_ref[bslot, r * sub:(r + 1) * sub, :])
        s_ref[(h // gs % 2) * gs + h % gs, r * sub:(r + 1) * sub, :] = x
        return jnp.maximum(mx, _row_fold(x, jnp.maximum))

    run0 = lax.cond(ties, bias_ties, bias_plain, 0, 0, jnp.zeros((1, qb), F32))
    mx0 = [no_max] * gs
    for r in range(n_sub):
        mx0 = [logits(0, 0, g, r, mx0[g]) for g in range(gs)]

    def attn_body(j, carry):
        run, mx = carry
        mx = list(mx)
        jn = jnp.minimum(j + 1, n_ch - 1)
        run = lax.cond(ties, bias_ties, bias_plain, jn, (j + 1) % 2, run)
        for st in range(n_stage):
            heads = [st * gs + g for g in range(gs)]
            m_new, m_safe, alpha = [], [], []
            for g, h in enumerate(heads):
                m_old = m_ref[h]
                m_new.append(jnp.maximum(m_old, jnp.max(mx[g], axis=0, keepdims=True)))
                m_safe.append(jnp.where(m_new[g] == NEG_INF, 0.0, m_new[g]))
                alpha.append(jnp.exp2(m_old - m_safe[g]))
            mx = [no_max] * gs
            for r in range(n_sub):
                tile = slice(r * sub, (r + 1) * sub)
                for g in range(gs):
                    if st + 1 < n_stage:
                        mx[g] = logits(j, j % 2, heads[g] + gs, r, mx[g])
                    else:
                        mx[g] = logits(jn, (j + 1) % 2, g, r, mx[g])
                for g in range(gs):
                    p_ref[g, tile, :] = jnp.exp2(
                        s_ref[(st % 2) * gs + g, tile, :] - m_safe[g]).astype(BF16)
            for g, h in enumerate(heads):
                pv = jnp.dot(vt_ref[0, j, h * V_ROWS:(h + 1) * V_ROWS, :], p_ref[g],
                             preferred_element_type=F32)
                l_ref[h] = alpha[g] * l_ref[h] + pv[HEAD_DIM:HEAD_DIM + 1, :]
                acc_ref[h] = alpha[g] * acc_ref[h] + pv[0:HEAD_DIM, :]
                m_ref[h] = m_new[g]
        return run, tuple(mx)

    lax.fori_loop(0, n_ch, attn_body, (run0, tuple(mx0)))

    outs = [acc_ref[h] / l_ref[h] for h in range(N_HEADS)]
    o_ref[0] = jnp.concatenate(outs, axis=0).T.astype(o_ref.dtype)


def _attn_call(qt, qit, wit, kk, k, vt):
    b, s, _ = k.shape
    qb = min(Q_BLOCK, s)
    kb = min(K_BLOCK, s)
    n_sel = min(TOPK_MAX, s // 4)
    blk_t = lambda n: pl.BlockSpec((1, n, qb), lambda bi, qi_: (bi, 0, qi_))
    once = pl.Buffered(1)
    kern = functools.partial(_attn_kernel, n_sel=n_sel, qb=qb, kb=kb, sub=min(K_SUB, kb))
    return pl.pallas_call(
        kern,
        grid=(b, s // qb),
        in_specs=[blk_t(D_ATTN), blk_t(D_ATTN), blk_t(N_IDX_HEADS),
                  pl.BlockSpec((1, s, LANES), lambda bi, qi_: (bi, 0, 0), pipeline_mode=once),
                  pl.BlockSpec((1, s, D_ATTN), lambda bi, qi_: (bi, 0, 0), pipeline_mode=once),
                  pl.BlockSpec((1, s // kb, N_HEADS * V_ROWS, kb), lambda bi, qi_: (bi, 0, 0, 0),
                               pipeline_mode=once)],
        out_specs=pl.BlockSpec((1, qb, D_ATTN), lambda bi, qi_: (bi, qi_, 0)),
        out_shape=jax.ShapeDtypeStruct((b, s, D_ATTN), BF16),
        scratch_shapes=[pltpu.VMEM((s // kb, kb, qb), I32),
                        pltpu.VMEM((s // kb, kb, qb), I16),
                        pltpu.VMEM((s // kb, kb, qb), I16),
                        pltpu.VMEM((2, kb, qb), F32),
                        pltpu.VMEM((2 * HEADS_PER_STAGE, kb, qb), F32),
                        pltpu.VMEM((HEADS_PER_STAGE, kb, qb), BF16),
                        pltpu.VMEM((N_HEADS, LANES, qb), BF16),
                        pltpu.VMEM((N_IDX_HEADS, LANES, qb), BF16),
                        pltpu.VMEM((N_HEADS, 1, qb), F32),
                        pltpu.VMEM((N_HEADS, 1, qb), F32),
                        pltpu.VMEM((N_HEADS, HEAD_DIM, qb), F32)],
        compiler_params=_cparams(("parallel", "arbitrary")),
        name="attn",
    )(qt, qit, wit, kk, k, vt)


def _gelu_tanh(x):
    return 0.5 * x * (1.0 + jnp.tanh(math.sqrt(2.0 / math.pi) * (x + 0.044715 * (x * x * x))))


def _rglru_kernel(xr_ref, xg_ref, cw_ref, cb_ref, wa_ref, ba_ref, wx_ref, bx_ref, lam_ref,
                  o_ref, xbuf_ref, h_ref, *, ts):
    t = pl.program_id(1)
    pad = 8

    @pl.when(t == 0)
    def _():
        xbuf_ref[0:pad, :] = jnp.zeros((pad, D_RNN), F32)
        h_ref[...] = jnp.zeros_like(h_ref)

    xr = xr_ref[0]
    xbuf_ref[pad:pad + ts, :] = xr
    conv = cb_ref[...] + cw_ref[CONV_WIDTH - 1:CONV_WIDTH, :] * xr
    for w in range(CONV_WIDTH - 1):
        off = pad - (CONV_WIDTH - 1) + w
        conv = conv + cw_ref[w:w + 1, :] * xbuf_ref[off:off + ts, :]
    xbuf_ref[0:pad, :] = xr[ts - pad:ts, :]

    cb16 = conv.astype(BF16)
    r = jax.nn.sigmoid(jnp.dot(cb16, wa_ref[...], preferred_element_type=F32) + ba_ref[...])
    ig = jax.nn.sigmoid(jnp.dot(cb16, wx_ref[...], preferred_element_type=F32) + bx_ref[...])
    z = -lam_ref[...]
    softplus = jnp.maximum(z, 0.0) + jnp.log1p(jnp.exp(-jnp.abs(z)))
    log_a = (-RG_C) * r * softplus
    a = jnp.exp(log_a)
    bb = jnp.sqrt(1.0 - a * a) * (ig * conv)

    row = lax.broadcasted_iota(I32, (ts, 1), 0)
    d = 1
    while d < ts:
        if d < SUBLANES:
            live = row >= d
            a_s = jnp.where(live, pltpu.roll(a, d, axis=0), 1.0)
            b_s = jnp.where(live, pltpu.roll(bb, d, axis=0), 0.0)
            bb = bb + a * b_s
            a = a * a_s
        else:
            lo_a, hi_a = a[:d, :], a[d:, :]
            bb = jnp.concatenate([bb[:d, :], bb[d:, :] + hi_a * bb[:ts - d, :]], axis=0)
            a = jnp.concatenate([lo_a, hi_a * a[:ts - d, :]], axis=0)
        d *= 2
    h = bb + a * h_ref[0:1, :]
    h_ref[0:1, :] = h[ts - 1:ts, :]
    o_ref[0] = (h * _gelu_tanh(xg_ref[0])).astype(o_ref.dtype)


def _rglru_call(xr, xg, conv_w, conv_b, wa, ba, wx, bx, lam):
    b, s, r = xr.shape
    ts = min(RNN_TILE, s)
    tile = pl.BlockSpec((1, ts, r), lambda bi, ti: (bi, ti, 0))
    const = lambda shape: pl.BlockSpec(shape, lambda bi, ti: (0, 0))
    return pl.pallas_call(
        functools.partial(_rglru_kernel, ts=ts),
        grid=(b, s // ts),
        in_specs=[tile, tile, const((CONV_WIDTH, r)), const((1, r)), const((r, r)), const((1, r)),
                  const((r, r)), const((1, r)), const((1, r))],
        out_specs=tile,
        out_shape=jax.ShapeDtypeStruct((b, s, r), BF16),
        scratch_shapes=[pltpu.VMEM((ts + 8, r), F32), pltpu.VMEM((8, r), F32)],
        compiler_params=_cparams(("parallel", "arbitrary")),
        name="rglru",
    )(xr, xg, conv_w, conv_b, wa, ba, wx, bx, lam)


def _route(logits):
    lane = lax.broadcasted_iota(I32, logits.shape, 1)
    is_g = (lane >= N_EXPERTS) & (lane < N_EXPERTS + N_GROUPS)
    big = jnp.int32(LANES)

    def first_lane(mask):
        return jnp.min(jnp.where(mask, lane, big), axis=1, keepdims=True)

    gl = jnp.where(is_g, logits, NEG_INF)
    ge = jnp.exp(gl - jnp.max(gl, axis=1, keepdims=True))
    p_groups = ge / jnp.sum(ge, axis=1, keepdims=True)
    p_g = jnp.max(p_groups, axis=1, keepdims=True)
    g_idx = first_lane(is_g & (p_groups == p_g)) - N_EXPERTS

    in_grp = (lane < N_EXPERTS) & ((lane // EXP_PER_GROUP) == g_idx)
    el = jnp.where(in_grp, logits, NEG_INF)
    ee = jnp.exp(el - jnp.max(el, axis=1, keepdims=True))
    es = ee / jnp.sum(ee, axis=1, keepdims=True)
    es = jnp.where(in_grp, es, NEG_INF)
    v1 = jnp.max(es, axis=1, keepdims=True)
    i1 = first_lane(in_grp & (es == v1))
    es2 = jnp.where(lane == i1, NEG_INF, es)
    v2 = jnp.max(es2, axis=1, keepdims=True)
    i2 = first_lane(in_grp & (es2 == v2))
    tot = v1 + v2
    e_w = jnp.where(lane == i1, v1 / tot, 0.0) + jnp.where(lane == i2, v2 / tot, 0.0)
    return p_g * e_w + jnp.where(lane == ROUTE_GROUP_LANE + g_idx, 1.0, 0.0)


def _out_proj_kernel(ya_ref, yr_ref, x_ref, mod_ref, wo_ref, gpm_ref, gpf_ref, wr_ref, br_ref,
                     x1_ref, h2_ref, comb_ref):
    mix = (jnp.dot(ya_ref[0], wo_ref[0:D_ATTN, :], preferred_element_type=F32)
           + jnp.dot(yr_ref[0], wo_ref[D_ATTN:D_ATTN + D_RNN, :], preferred_element_type=F32))
    gate1 = mod_ref[0, 2:3, :]
    shift2 = mod_ref[0, 3:4, :]
    scale2 = mod_ref[0, 4:5, :]
    x1 = x_ref[0] + gate1 * _rms(mix, gpm_ref[...])
    x1_ref[0] = x1
    h2 = _rms(x1, gpf_ref[...]) * (1.0 + scale2) + shift2
    h_hi = h2.astype(BF16)
    h2_ref[0] = h_hi
    h_lo = (h2 - h_hi.astype(F32)).astype(BF16)
    part = jnp.dot(h_hi, wr_ref[...], preferred_element_type=F32)
    logits = (part[:, 0:LANES] + (part[:, LANES:2 * LANES]
              + jnp.dot(h_lo, wr_ref[:, 0:LANES], preferred_element_type=F32))) + br_ref[...]
    comb_ref[0] = _route(logits)


def _out_proj_call(ya, yr, x, mod, wo, gpm, gpf, wr, br):
    b, s, d = x.shape
    tm = min(OUT_TILE, s)
    tile = lambda n: pl.BlockSpec((1, tm, n), lambda bi, ti: (bi, ti, 0))
    const = lambda shape: pl.BlockSpec(shape, lambda bi, ti: (0,) * len(shape))
    return pl.pallas_call(
        _out_proj_kernel,
        grid=(b, s // tm),
        in_specs=[tile(D_ATTN), tile(D_RNN), tile(d),
                  pl.BlockSpec((1, N_MOD, d), lambda bi, ti: (bi, 0, 0)),
                  const((D_ATTN + D_RNN, d)), const((1, d)), const((1, d)),
                  const((d, 2 * LANES)), const((1, LANES))],
        out_specs=[tile(d), tile(d), tile(LANES)],
        out_shape=[jax.ShapeDtypeStruct((b, s, d), F32), jax.ShapeDtypeStruct((b, s, d), BF16),
                   jax.ShapeDtypeStruct((b, s, LANES), F32)],
        compiler_params=_cparams(("parallel", "parallel")),
        name="out_proj",
    )(ya, yr, x, mod, wo, gpm, gpf, wr, br)


def _moe_kernel(h2_ref, comb_ref, x1_ref, mod_ref, wg_ref, wu_ref, wd_ref, gpo_ref,
                o_ref, hs_ref, cs_ref, ys_ref, act_ref, slot_ref, seg_ref, *, tm, ns, rb):
    g = pl.program_id(2)
    lane = lax.broadcasted_iota(I32, (1, LANES), 1)
    n_blk = tm // LANES

    @pl.when(g == 0)
    def _dispatch():
        comb = comb_ref[0]
        oh = jnp.where((lane >= ROUTE_GROUP_LANE) & (lane < ROUTE_GROUP_LANE + N_GROUPS), comb, 0.0)
        oh16 = oh.astype(BF16)
        sub = lax.broadcasted_iota(I32, (LANES, 1), 0)
        base = jnp.int32(0)
        base_lane = jnp.zeros((1, LANES), F32)
        base_sub = jnp.zeros((LANES, 1), F32)
        for gg in range(N_GROUPS):
            here = ROUTE_GROUP_LANE + gg
            n = jnp.sum(jnp.where(lane == here, oh, 0.0)).astype(I32)
            nb = (n + rb - 1) // rb
            seg_ref[gg] = base
            seg_ref[N_GROUPS + gg] = nb
            base_f = base.astype(F32)
            base_lane = base_lane + jnp.where(lane == here, base_f, 0.0)
            base_sub = base_sub + jnp.where(sub == here, base_f, 0.0)
            base = base + nb * rb

        tok_c = lax.broadcasted_iota(I32, (1, tm), 1)
        tok_r = lax.broadcasted_iota(I32, (tm, 1), 0)
        blk_r = lax.broadcasted_iota(I32, (LANES, 1), 0)
        blk_c = lax.broadcasted_iota(I32, (1, LANES), 1)
        eye = (blk_r == blk_c).astype(BF16)
        oht16 = _nt_dot(eye, oh16).astype(BF16)
        slot_row = []
        for t in range(n_blk):
            tril = (tok_c <= blk_r + t * LANES).astype(BF16)
            rank = jnp.dot(tril, oh16, preferred_element_type=F32)
            oh_blk = oh[t * LANES:(t + 1) * LANES, :]
            slot_ref[t * LANES:(t + 1) * LANES, :] = jnp.sum(
                oh_blk * (base_lane + rank - 1.0), axis=1, keepdims=True).astype(I32)
            triu = (tok_r <= blk_c + t * LANES).astype(BF16)
            rank_t = jnp.dot(oht16, triu, preferred_element_type=F32)
            oht_blk = oht16[:, t * LANES:(t + 1) * LANES].astype(F32)
            slot_row.append(jnp.sum(oht_blk * (base_sub + rank_t - 1.0), axis=0,
                                    keepdims=True).astype(I32))
        slot_row = jnp.concatenate(slot_row, axis=1)

        hi = comb.astype(BF16)
        lo = (comb - hi.astype(F32)).astype(BF16)
        cw = jnp.concatenate([hi, lo], axis=1)
        h = h2_ref[0]
        for r in range(ns // LANES):
            rows = slice(r * LANES, (r + 1) * LANES)
            p = (blk_r + r * LANES == slot_row).astype(BF16)
            hs_ref[rows, :] = jnp.dot(p, h, preferred_element_type=F32).astype(BF16)
            cw_sorted = jnp.dot(p, cw, preferred_element_type=F32)
            cs_ref[rows, :] = cw_sorted[:, 0:LANES] + cw_sorted[:, LANES:2 * LANES]
        ys_ref[...] = jnp.zeros_like(ys_ref)

    base = seg_ref[g]

    def block(r, carry):
        rows = pl.ds(pl.multiple_of(base + r * rb, rb), rb)
        hb = hs_ref[rows, :]
        cb = cs_ref[rows, :]
        for e in range(EXP_PER_GROUP):
            gt = jnp.dot(hb, wg_ref[0, e], preferred_element_type=F32)
            up = jnp.dot(hb, wu_ref[0, e], preferred_element_type=F32)
            cols = slice(e * D_EXPERT, (e + 1) * D_EXPERT)
            scale = jnp.sum(jnp.where(lane == g * EXP_PER_GROUP + e, cb, 0.0), axis=1, keepdims=True)
            act_ref[:, cols] = (gt * jax.nn.sigmoid(gt) * up * scale).astype(BF16)
        ys_ref[rows, :] = jnp.dot(act_ref[...], wd_ref[0], preferred_element_type=F32).astype(BF16)
        return carry

    lax.fori_loop(0, seg_ref[N_GROUPS + g], block, 0)

    @pl.when(g == N_GROUPS - 1)
    def _combine():
        gate2 = mod_ref[0, 5:6, :]
        slot_c = lax.broadcasted_iota(I32, (1, ns), 1)
        half = tm // 2
        for hh in range(2):
            rows = slice(hh * half, (hh + 1) * half)
            pt = (slot_c == slot_ref[rows, :]).astype(BF16)
            y = jnp.dot(pt, ys_ref[...], preferred_element_type=F32)
            o_ref[0, rows, :] = x1_ref[0, rows, :] + gate2 * _rms(y, gpo_ref[...])


def _moe_call(h2, comb, x1, mod, wg, wu, wd, gpo):
    b, s, d = x1.shape
    tm = min(MOE_TILE, s)
    rb = MOE_ROWS
    ns = tm + N_GROUPS * rb
    tile = lambda n, **kw: pl.BlockSpec((1, tm, n), lambda bi, ti, gi: (bi, ti, 0), **kw)
    kern = functools.partial(_moe_kernel, tm=tm, ns=ns, rb=rb)
    return pl.pallas_call(
        kern,
        grid=(b, s // tm, N_GROUPS),
        in_specs=[tile(d), tile(LANES), tile(d, pipeline_mode=pl.Buffered(1)),
                  pl.BlockSpec((1, N_MOD, d), lambda bi, ti, gi: (bi, 0, 0)),
                  pl.BlockSpec((1, EXP_PER_GROUP, d, D_EXPERT), lambda bi, ti, gi: (gi, 0, 0, 0)),
                  pl.BlockSpec((1, EXP_PER_GROUP, d, D_EXPERT), lambda bi, ti, gi: (gi, 0, 0, 0)),
                  pl.BlockSpec((1, EXP_PER_GROUP * D_EXPERT, d), lambda bi, ti, gi: (gi, 0, 0)),
                  pl.BlockSpec((1, d), lambda bi, ti, gi: (0, 0))],
        out_specs=tile(d),
        out_shape=jax.ShapeDtypeStruct((b, s, d), F32),
        scratch_shapes=[pltpu.VMEM((ns, d), BF16),
                        pltpu.VMEM((ns, LANES), F32),
                        pltpu.VMEM((ns, d), BF16),
                        pltpu.VMEM((rb, EXP_PER_GROUP * D_EXPERT), BF16),
                        pltpu.VMEM((tm, 1), I32),
                        pltpu.SMEM((2 * N_GROUPS,), I32)],
        compiler_params=_cparams(("parallel", "parallel", "arbitrary")),
        name="moe",
    )(h2, comb, x1, mod, wg, wu, wd, gpo)


def _rope_tables(seq_len):
    pos = jnp.arange(seq_len, dtype=F32)
    inv = ROPE_THETA ** (-jnp.arange(0, HEAD_DIM, 2, dtype=F32) / HEAD_DIM)
    ang = pos[:, None] * inv[None, :]
    cos, sin = jnp.cos(ang), jnp.sin(ang)
    cos2 = jnp.concatenate([cos, cos, cos, cos], axis=1)
    sin2 = jnp.concatenate([-sin, sin, -sin, sin], axis=1)
    return cos2, sin2, cos2[:, :HEAD_DIM].T, sin2[:, :HEAD_DIM].T


def _block_diag(w):
    n, c, d = w.shape
    eye = jnp.eye(n, dtype=w.dtype)
    return (eye[:, None, :, None] * w[:, :, None, :]).reshape(n * c, n * d)


def _pad_cols(w, n):
    return jnp.pad(w, ((0, 0), (0, n - w.shape[1])))


def kernel(x, c, w_ada, b_ada, g_pre_mix, g_post_mix, g_pre_ffn, g_post_ffn, w_in, conv_w, conv_b, w_rg_a, b_rg_a, w_rg_x, b_rg_x, lru_lambda, w_out, w_router_group, b_router_group, w_router_expert, b_router_expert, w_gate, w_up, w_down):
    b, s, d = x.shape
    depth = w_ada.shape[0]
    cos2, sin2, cos_t, sin_t = _rope_tables(s)
    c_pad = jnp.pad(c, ((0, (-b) % SUBLANES), (0, 0)))
    o_q, o_k, o_v, o_qi, o_ki, o_wi, o_xr, o_xg = 0, 512, 1024, 1536, 2048, 2112, 2120, 2632
    for l in range(depth):
        wl = w_in[l]
        w_ki = wl[:, o_ki:o_wi]
        w_nat = jnp.concatenate([wl[:, o_k:o_v], w_ki, w_ki, wl[:, o_xr:o_xg], wl[:, o_xg:]],
                                axis=1).astype(BF16)
        w_tr = jnp.concatenate([wl[:, o_q:o_k], wl[:, o_qi:o_ki], wl[:, o_v:o_qi],
                                _pad_cols(wl[:, o_wi:o_xr], _R_END - _R_WI)], axis=1).T.astype(BF16)
        w_route = _pad_cols(jnp.concatenate([w_router_expert[l], w_router_group[l]], axis=1), LANES)
        w_route_hi = w_route.astype(BF16)
        w_route = jnp.concatenate(
            [w_route_hi, (w_route - w_route_hi.astype(F32)).astype(BF16)], axis=1)
        b_route = _pad_cols(jnp.concatenate([b_router_expert[l], b_router_group[l]])[None, :], LANES)

        mod = _mod_call(c_pad, w_ada[l], b_ada[l][None, :])[:b].reshape(b, N_MOD, d)
        qt, qit, wit, k, kk, vt, xr, xg = _in_proj_call(
            x, mod, g_pre_mix[l][None, :], w_nat, w_tr, cos2, sin2, cos_t, sin_t)
        y_attn = _attn_call(qt, qit, wit, kk, k, vt)
        y_rnn = _rglru_call(xr, xg, conv_w[l], conv_b[l][None, :],
                            _block_diag(w_rg_a[l]).astype(BF16), b_rg_a[l][None, :],
                            _block_diag(w_rg_x[l]).astype(BF16), b_rg_x[l][None, :],
                            lru_lambda[l][None, :])
        x1, h2, comb = _out_proj_call(y_attn, y_rnn, x, mod, w_out[l].astype(BF16),
                                      g_post_mix[l][None, :], g_pre_ffn[l][None, :], w_route, b_route)
        x = _moe_call(h2, comb, x1, mod, w_gate[l].astype(BF16), w_up[l].astype(BF16),
                      w_down[l].reshape(N_GROUPS, EXP_PER_GROUP * D_EXPERT, d).astype(BF16),
                      g_post_ffn[l][None, :])
    return x
```

```python
import functools
import math

import jax
import jax.numpy as jnp
from jax import lax
from jax.experimental import pallas as pl
from jax.experimental.pallas import tpu as pltpu

F32 = jnp.float32
BF16 = jnp.bfloat16
I32 = jnp.int32
I16 = jnp.int16

D_MODEL = 1024
CHUNK = 64
ROPE_THETA = 10000.0
EPS = 1e-6
N_HEADS = 8
HEAD_DIM = 64
D_ATTN = N_HEADS * HEAD_DIM
N_IDX_HEADS = 8
IDX_DIM = 64
TOPK_MAX = 256
D_RNN = 512
N_RNN_BLOCKS = 8
CONV_WIDTH = 4
RG_C = 8.0
N_GROUPS = 4
EXP_PER_GROUP = 8
N_EXPERTS = N_GROUPS * EXP_PER_GROUP
D_EXPERT = 256
N_MOD = 6

LANES = 128
SUBLANES = 8
INT_MIN = -2 ** 31
INT16_MIN = -2 ** 15
NEG_INF = float("-inf")

Q_BLOCK = 256
K_BLOCK = 512
K_SUB = 256
HEADS_PER_STAGE = 2
V_ROWS = HEAD_DIM + 16
IN_TILE = 1024
RNN_TILE = 512
OUT_TILE = 1024
MOE_TILE = 1024
MOE_ROWS = 128
ROUTE_GROUP_LANE = 64
VMEM_LIMIT = 60 * 1024 * 1024


def _cparams(sem):
    return pltpu.CompilerParams(dimension_semantics=sem, vmem_limit_bytes=VMEM_LIMIT)


def _nt_dot(a, b):
    return lax.dot_general(a, b, (((1,), (1,)), ((), ())), preferred_element_type=F32)


def _rms(x, g):
    return x * lax.rsqrt(jnp.mean(x * x, axis=-1, keepdims=True) + EPS) * g


def _mod_kernel(c_ref, w_ref, b_ref, o_ref):
    c = c_ref[...]
    sc = c * jax.nn.sigmoid(c)
    o_ref[...] = jnp.dot(sc, w_ref[...], preferred_element_type=F32,
                         precision=lax.Precision.HIGHEST) + b_ref[...]


def _mod_call(c_pad, w_ada, b_ada):
    rows, d = c_pad.shape
    n = w_ada.shape[1]
    bn = 1024
    return pl.pallas_call(
        _mod_kernel,
        grid=(n // bn,),
        in_specs=[pl.BlockSpec((rows, d), lambda j: (0, 0)),
                  pl.BlockSpec((d, bn), lambda j: (0, j)),
                  pl.BlockSpec((1, bn), lambda j: (0, j))],
        out_specs=pl.BlockSpec((rows, bn), lambda j: (0, j)),
        out_shape=jax.ShapeDtypeStruct((rows, n), F32),
        compiler_params=_cparams(("arbitrary",)),
        name="mod",
    )(c_pad, w_ada, b_ada)


_R_Q, _R_QI, _R_V, _R_WI, _R_END = 0, 512, 1024, 1536, 1552
_C_K, _C_KK, _C_XR, _C_XG, _C_END = 0, 512, 640, 1152, 1664


def _rope(y, cos, sin):
    lane = lax.broadcasted_iota(I32, (1, LANES), 1)
    low = (lane % HEAD_DIM) < (HEAD_DIM // 2)
    outs = []
    for p in range(y.shape[1] // LANES):
        s = y[:, p * LANES:(p + 1) * LANES]
        swapped = jnp.where(low, pltpu.roll(s, LANES - HEAD_DIM // 2, axis=1),
                            pltpu.roll(s, HEAD_DIM // 2, axis=1))
        outs.append(s * cos + swapped * sin)
    return outs[0] if len(outs) == 1 else jnp.concatenate(outs, axis=1)


def _rope_t(y, cos, sin):
    half = HEAD_DIM // 2
    outs = []
    for hd in range(y.shape[0] // HEAD_DIM):
        blk = y[hd * HEAD_DIM:(hd + 1) * HEAD_DIM, :]
        swapped = jnp.concatenate([blk[half:, :], blk[:half, :]], axis=0)
        outs.append(blk * cos + swapped * sin)
    return jnp.concatenate(outs, axis=0)


def _in_proj_kernel(x_ref, mod_ref, g_ref, wn_ref, wt_ref, cos_ref, sin_ref, cost_ref, sint_ref,
                    qt_ref, qit_ref, wit_ref, k_ref, kk_ref, vt_ref, xr_ref, xg_ref):
    x = x_ref[0]
    shift = mod_ref[0, 0:1, :]
    scale = mod_ref[0, 1:2, :]
    h = (_rms(x, g_ref[...]) * (1.0 + scale) + shift).astype(BF16)
    cos, sin = cos_ref[...], sin_ref[...]
    cos_t, sin_t = cost_ref[...], sint_ref[...]

    def proj(a, b):
        return jnp.dot(h, wn_ref[:, a:b], preferred_element_type=F32)

    def proj_t(a, b):
        return _nt_dot(wt_ref[a:b, :], h)

    qt_ref[0] = (_rope_t(proj_t(_R_Q, _R_QI), cos_t, sin_t)
                 * (HEAD_DIM ** -0.5 * math.log2(math.e))).astype(BF16)
    qit_ref[0] = (_rope_t(proj_t(_R_QI, _R_V), cos_t, sin_t) * (IDX_DIM ** -0.5)).astype(BF16)
    vt = proj_t(_R_V, _R_WI).astype(BF16)
    ones = jnp.ones((V_ROWS - HEAD_DIM, vt.shape[1]), BF16)
    kb = vt_ref.shape[3]
    for c in range(vt_ref.shape[1]):
        for hd in range(N_HEADS):
            vt_ref[0, c, hd * V_ROWS:hd * V_ROWS + HEAD_DIM, :] = (
                vt[hd * HEAD_DIM:(hd + 1) * HEAD_DIM, c * kb:(c + 1) * kb])
            vt_ref[0, c, hd * V_ROWS + HEAD_DIM:(hd + 1) * V_ROWS, :] = ones[:, 0:kb]
    wit_ref[0] = proj_t(_R_WI, _R_END)[0:N_IDX_HEADS, :] * (N_IDX_HEADS ** -0.5)
    k_ref[0] = _rope(proj(_C_K, _C_KK), cos, sin).astype(BF16)
    kk_ref[0] = _rope(proj(_C_KK, _C_XR), cos, sin).astype(BF16)
    xr_ref[0] = proj(_C_XR, _C_XG)
    xg_ref[0] = proj(_C_XG, _C_END)


def _in_proj_call(x, mod, g, w_nat, w_tr, cos2, sin2, cos_t, sin_t):
    b, s, d = x.shape
    kb = min(K_BLOCK, s)
    tm = min(IN_TILE, s)
    nkb = tm // kb
    nt = s // tm
    tile = lambda n: pl.BlockSpec((1, tm, n), lambda bi, ti: (bi, ti, 0))
    tile_t = lambda n: pl.BlockSpec((1, n, tm), lambda bi, ti: (bi, 0, ti))
    const = lambda shape: pl.BlockSpec(shape, lambda bi, ti: (0, 0))
    shp = lambda n, dt: jax.ShapeDtypeStruct((b, s, n), dt)
    shp_t = lambda n, dt: jax.ShapeDtypeStruct((b, n, s), dt)
    return pl.pallas_call(
        _in_proj_kernel,
        grid=(b, nt),
        in_specs=[tile(d),
                  pl.BlockSpec((1, N_MOD, d), lambda bi, ti: (bi, 0, 0)),
                  const((1, d)), const((d, _C_END)), const((_R_END, d)),
                  pl.BlockSpec((tm, LANES), lambda bi, ti: (ti, 0)),
                  pl.BlockSpec((tm, LANES), lambda bi, ti: (ti, 0)),
                  pl.BlockSpec((HEAD_DIM, tm), lambda bi, ti: (0, ti)),
                  pl.BlockSpec((HEAD_DIM, tm), lambda bi, ti: (0, ti))],
        out_specs=[tile_t(D_ATTN), tile_t(D_ATTN), tile_t(N_IDX_HEADS), tile(D_ATTN), tile(LANES),
                   pl.BlockSpec((1, nkb, N_HEADS * V_ROWS, kb), lambda bi, ti: (bi, ti, 0, 0)),
                   tile(D_RNN), tile(D_RNN)],
        out_shape=[shp_t(D_ATTN, BF16), shp_t(D_ATTN, BF16), shp_t(N_IDX_HEADS, F32),
                   shp(D_ATTN, BF16), shp(LANES, BF16),
                   jax.ShapeDtypeStruct((b, s // kb, N_HEADS * V_ROWS, kb), BF16),
                   shp(D_RNN, F32), shp(D_RNN, F32)],
        compiler_params=_cparams(("parallel", "parallel")),
        name="in_proj",
    )(x, mod, g, w_nat, w_tr, cos2, sin2, cos_t, sin_t)


def _sortable(score):
    bits = pltpu.bitcast(score, I32)
    return jnp.where(bits < 0, jnp.int32(INT_MIN) - bits, bits)


def _tile_fold(x, op, rows):
    parts = [x[t * rows:(t + 1) * rows, :] for t in range(x.shape[0] // rows)]
    while len(parts) > 1:
        nxt = [op(parts[t], parts[t + 1]) for t in range(0, len(parts) - 1, 2)]
        if len(parts) % 2:
            nxt.append(parts[-1])
        parts = nxt
    return parts[0]


def _row_fold(x, op):
    acc = x[0:SUBLANES, :]
    for t in range(1, x.shape[0] // SUBLANES):
        acc = op(acc, x[t * SUBLANES:(t + 1) * SUBLANES, :])
    return acc


def _fold16(x):
    return _tile_fold(x, jnp.add, 2 * SUBLANES)


def _attn_kernel(qt_ref, qit_ref, wit_ref, kk_ref, k_ref, vt_ref, o_ref,
                 keys_ref, bias_ref, s_ref, p_ref, qm_ref, qim_ref, m_ref, l_ref,
                 acc_ref, *, n_sel, qb, kb, sub):
    i = pl.program_id(1)
    n_ch = ((i + 1) * qb + kb - 1) // kb
    q_pos = i * qb + lax.broadcasted_iota(I32, (1, qb), 1)
    limit = (q_pos // CHUNK + 1) * CHUNK
    krow = lax.broadcasted_iota(I32, (kb, 1), 0)
    row128 = lax.broadcasted_iota(I32, (LANES, 1), 0)

    for h in range(N_HEADS):
        rows = slice((h // 2) * LANES, (h // 2 + 1) * LANES)
        keep = (row128 < HEAD_DIM) if h % 2 == 0 else (row128 >= HEAD_DIM)
        qs, qis = qt_ref[0, rows, :], qit_ref[0, rows, :]
        qm_ref[h] = jnp.where(keep, qs, jnp.zeros_like(qs))
        qim_ref[h] = jnp.where(keep, qis, jnp.zeros_like(qis))
    m_ref[...] = jnp.full(m_ref.shape, NEG_INF, F32)
    l_ref[...] = jnp.zeros(l_ref.shape, F32)
    acc_ref[...] = jnp.zeros(acc_ref.shape, F32)

    wi = wit_ref[0]

    def score_body(j, carry, masked):
        kk = kk_ref[0, pl.ds(pl.multiple_of(j * kb, kb), kb), :]
        acc = jnp.zeros((kb, qb), F32)
        for h in range(N_IDX_HEADS):
            d = jnp.dot(kk, qim_ref[h], preferred_element_type=F32)
            acc = acc + wi[h:h + 1, :] * jnp.maximum(d, 0.0)
        if masked:
            acc = jnp.where(j * kb + krow < limit, acc, NEG_INF)
        keys_ref[j] = acc
        return carry

    n_open = (i * qb + CHUNK) // kb

    def open_group(jj, carry, width, first):
        for u in range(width):
            score_body(first + width * jj + u, carry, masked=False)
        return carry

    n_quad = n_open // 4
    n_pair = (n_open - 4 * n_quad) // 2
    lax.fori_loop(0, n_quad, functools.partial(open_group, width=4, first=0), 0)
    lax.fori_loop(0, n_pair, functools.partial(open_group, width=2, first=4 * n_quad), 0)
    lax.fori_loop(4 * n_quad + 2 * n_pair, n_open, functools.partial(score_body, masked=False), 0)
    lax.fori_loop(n_open, n_ch, functools.partial(score_body, masked=True), 0)

    def count_ge(cand):
        def one(j):
            return _row_fold(jnp.where(keys_ref[j] >= cand, 1.0, 0.0), jnp.add)

        def body(jj, acc):
            return acc + (one(2 * jj) + one(2 * jj + 1))

        acc = lax.fori_loop(0, n_ch // 2, body, jnp.zeros((SUBLANES, qb), F32))
        acc = lax.cond(n_ch % 2 == 1, lambda a: a + one(n_ch - 1), lambda a: a, acc)
        return jnp.sum(acc, axis=0, keepdims=True).astype(I32)

    def as_float(key):
        return pltpu.bitcast(jnp.where(key < 0, jnp.int32(INT_MIN) - key, key), F32)

    c0 = count_ge(jnp.zeros((1, qb), F32))
    ok0 = c0 >= n_sel
    init = (jnp.where(ok0, jnp.int32(0), jnp.int32(INT_MIN)),
            jnp.where(ok0, c0, n_ch * kb), jnp.where(ok0, 0, c0))

    def bit_body(bi, carry):
        t, n_ge, n_gt = carry
        cand = t | (jnp.int32(1) << (30 - bi))
        c = count_ge(as_float(cand))
        ok = c >= n_sel
        return jnp.where(ok, cand, t), jnp.where(ok, c, n_ge), jnp.where(ok, n_gt, c)

    tau, n_ge, n_gt = lax.fori_loop(0, 31, bit_body, init)
    all_sel = limit <= n_sel
    need_i = jnp.where(all_sel, 0, n_sel - n_gt)
    need = need_i.astype(F32)
    ties = jnp.max(jnp.where(all_sel, 0, (n_ge - n_gt) - need_i)) > 0
    tau_f = jnp.where(all_sel, jnp.float32(jnp.finfo(jnp.float32).min), as_float(tau))

    n_sub = kb // sub
    no_max = jnp.full((SUBLANES, qb), NEG_INF, F32)

    def bias_ties(j, slot, run):
        r_i = lax.broadcasted_iota(I32, (kb, kb), 0)
        c_i = lax.broadcasted_iota(I32, (kb, kb), 1)
        tri = (c_i <= r_i).astype(BF16)
        kc = keys_ref[j]
        eq = kc == tau_f
        rank = jnp.dot(tri, eq.astype(BF16), preferred_element_type=F32) + run
        sel = (kc > tau_f) | (eq & (rank <= need))
        bias_ref[slot] = jnp.where(sel, 0.0, NEG_INF).astype(F32)
        return rank[kb - 1:kb, :]

    def bias_plain(j, slot, run):
        bias_ref[slot] = jnp.where(keys_ref[j] >= tau_f, 0.0, NEG_INF).astype(F32)
        return run

    gs = HEADS_PER_STAGE
    n_stage = N_HEADS // gs

    def logits(jb, bslot, h, r, mx):
        cols = slice((h // 2) * LANES, (h // 2 + 1) * LANES)
        rows = pl.ds(pl.multiple_of(jb * kb + r * sub, sub), sub)
        x = (jnp.dot(k_ref[0, rows, cols], qm_ref[h], preferred_element_type=F32)
             + bias_ref[bslot, r * sub:(r + 1) * sub, :])
        s_ref[(h // gs % 2) * gs + h % gs, r * sub:(r + 1) * sub, :] = x
        return jnp.maximum(mx, _row_fold(x, jnp.maximum))

    run0 = lax.cond(ties, bias_ties, bias_plain, 0, 0, jnp.zeros((1, qb), F32))
    mx0 = [no_max] * gs
    for r in range(n_sub):
        mx0 = [logits(0, 0, g, r, mx0[g]) for g in range(gs)]

    def attn_body(j, carry):
        run, mx = carry
        mx = list(mx)
        jn = jnp.minimum(j + 1, n_ch - 1)
        run = lax.cond(ties, bias_ties, bias_plain, jn, (j + 1) % 2, run)
        for st in range(n_stage):
            heads = [st * gs + g for g in range(gs)]
            m_new, m_safe, alpha = [], [], []
            for g, h in enumerate(heads):
                m_old = m_ref[h]
                m_new.append(jnp.maximum(m_old, jnp.max(mx[g], axis=0, keepdims=True)))
                m_safe.append(jnp.where(m_new[g] == NEG_INF, 0.0, m_new[g]))
                alpha.append(jnp.exp2(m_old - m_safe[g]))
            mx = [no_max] * gs
            for r in range(n_sub):
                tile = slice(r * sub, (r + 1) * sub)
                for g in range(gs):
                    if st + 1 < n_stage:
                        mx[g] = logits(j, j % 2, heads[g] + gs, r, mx[g])
                    else:
                        mx[g] = logits(jn, (j + 1) % 2, g, r, mx[g])
                for g in range(gs):
                    p_ref[g, tile, :] = jnp.exp2(
                        s_ref[(st % 2) * gs + g, tile, :] - m_safe[g]).astype(BF16)
            for g, h in enumerate(heads):
                pv = jnp.dot(vt_ref[0, j, h * V_ROWS:(h + 1) * V_ROWS, :], p_ref[g],
                             preferred_element_type=F32)
                l_ref[h] = alpha[g] * l_ref[h] + pv[HEAD_DIM:HEAD_DIM + 1, :]
                acc_ref[h] = alpha[g] * acc_ref[h] + pv[0:HEAD_DIM, :]
                m_ref[h] = m_new[g]
        return run, tuple(mx)

    lax.fori_loop(0, n_ch, attn_body, (run0, tuple(mx0)))

    outs = [acc_ref[h] / l_ref[h] for h in range(N_HEADS)]
    o_ref[0] = jnp.concatenate(outs, axis=0).T.astype(o_ref.dtype)


def _attn_call(qt, qit, wit, kk, k, vt):
    b, s, _ = k.shape
    qb = min(Q_BLOCK, s)
    kb = min(K_BLOCK, s)
    n_sel = min(TOPK_MAX, s // 4)
    blk_t = lambda n: pl.BlockSpec((1, n, qb), lambda bi, qi_: (bi, 0, qi_))
    once = pl.Buffered(1)
    kern = functools.partial(_attn_kernel, n_sel=n_sel, qb=qb, kb=kb, sub=min(K_SUB, kb))
    return pl.pallas_call(
        kern,
        grid=(b, s // qb),
        in_specs=[blk_t(D_ATTN), blk_t(D_ATTN), blk_t(N_IDX_HEADS),
                  pl.BlockSpec((1, s, LANES), lambda bi, qi_: (bi, 0, 0), pipeline_mode=once),
                  pl.BlockSpec((1, s, D_ATTN), lambda bi, qi_: (bi, 0, 0), pipeline_mode=once),
                  pl.BlockSpec((1, s // kb, N_HEADS * V_ROWS, kb), lambda bi, qi_: (bi, 0, 0, 0),
                               pipeline_mode=once)],
        out_specs=pl.BlockSpec((1, qb, D_ATTN), lambda bi, qi_: (bi, qi_, 0)),
        out_shape=jax.ShapeDtypeStruct((b, s, D_ATTN), BF16),
        scratch_shapes=[pltpu.VMEM((s // kb, kb, qb), F32),
                        pltpu.VMEM((2, kb, qb), F32),
                        pltpu.VMEM((2 * HEADS_PER_STAGE, kb, qb), F32),
                        pltpu.VMEM((HEADS_PER_STAGE, kb, qb), BF16),
                        pltpu.VMEM((N_HEADS, LANES, qb), BF16),
                        pltpu.VMEM((N_IDX_HEADS, LANES, qb), BF16),
                        pltpu.VMEM((N_HEADS, 1, qb), F32),
                        pltpu.VMEM((N_HEADS, 1, qb), F32),
                        pltpu.VMEM((N_HEADS, HEAD_DIM, qb), F32)],
        compiler_params=_cparams(("parallel", "arbitrary")),
        name="attn",
    )(qt, qit, wit, kk, k, vt)


def _gelu_tanh(x):
    return 0.5 * x * (1.0 + jnp.tanh(math.sqrt(2.0 / math.pi) * (x + 0.044715 * (x * x * x))))


def _rglru_kernel(xr_ref, xg_ref, cw_ref, cb_ref, wa_ref, ba_ref, wx_ref, bx_ref, lam_ref,
                  o_ref, xbuf_ref, h_ref, *, ts):
    t = pl.program_id(1)
    pad = 8

    @pl.when(t == 0)
    def _():
        xbuf_ref[0:pad, :] = jnp.zeros((pad, D_RNN), F32)
        h_ref[...] = jnp.zeros_like(h_ref)

    xr = xr_ref[0]
    xbuf_ref[pad:pad + ts, :] = xr
    conv = cb_ref[...] + cw_ref[CONV_WIDTH - 1:CONV_WIDTH, :] * xr
    for w in range(CONV_WIDTH - 1):
        off = pad - (CONV_WIDTH - 1) + w
        conv = conv + cw_ref[w:w + 1, :] * xbuf_ref[off:off + ts, :]
    xbuf_ref[0:pad, :] = xr[ts - pad:ts, :]

    cb16 = conv.astype(BF16)
    r = jax.nn.sigmoid(jnp.dot(cb16, wa_ref[...], preferred_element_type=F32) + ba_ref[...])
    ig = jax.nn.sigmoid(jnp.dot(cb16, wx_ref[...], preferred_element_type=F32) + bx_ref[...])
    z = -lam_ref[...]
    softplus = jnp.maximum(z, 0.0) + jnp.log1p(jnp.exp(-jnp.abs(z)))
    log_a = (-RG_C) * r * softplus
    a = jnp.exp(log_a)
    bb = jnp.sqrt(1.0 - a * a) * (ig * conv)

    row = lax.broadcasted_iota(I32, (ts, 1), 0)
    d = 1
    while d < ts:
        if d < SUBLANES:
            live = row >= d
            a_s = jnp.where(live, pltpu.roll(a, d, axis=0), 1.0)
            b_s = jnp.where(live, pltpu.roll(bb, d, axis=0), 0.0)
            bb = bb + a * b_s
            a = a * a_s
        else:
            lo_a, hi_a = a[:d, :], a[d:, :]
            bb = jnp.concatenate([bb[:d, :], bb[d:, :] + hi_a * bb[:ts - d, :]], axis=0)
            a = jnp.concatenate([lo_a, hi_a * a[:ts - d, :]], axis=0)
        d *= 2
    h = bb + a * h_ref[0:1, :]
    h_ref[0:1, :] = h[ts - 1:ts, :]
    o_ref[0] = (h * _gelu_tanh(xg_ref[0])).astype(o_ref.dtype)


def _rglru_call(xr, xg, conv_w, conv_b, wa, ba, wx, bx, lam):
    b, s, r = xr.shape
    ts = min(RNN_TILE, s)
    tile = pl.BlockSpec((1, ts, r), lambda bi, ti: (bi, ti, 0))
    const = lambda shape: pl.BlockSpec(shape, lambda bi, ti: (0, 0))
    return pl.pallas_call(
        functools.partial(_rglru_kernel, ts=ts),
        grid=(b, s // ts),
        in_specs=[tile, tile, const((CONV_WIDTH, r)), const((1, r)), const((r, r)), const((1, r)),
                  const((r, r)), const((1, r)), const((1, r))],
        out_specs=tile,
        out_shape=jax.ShapeDtypeStruct((b, s, r), BF16),
        scratch_shapes=[pltpu.VMEM((ts + 8, r), F32), pltpu.VMEM((8, r), F32)],
        compiler_params=_cparams(("parallel", "arbitrary")),
        name="rglru",
    )(xr, xg, conv_w, conv_b, wa, ba, wx, bx, lam)


def _route(logits):
    lane = lax.broadcasted_iota(I32, logits.shape, 1)
    is_g = (lane >= N_EXPERTS) & (lane < N_EXPERTS + N_GROUPS)
    big = jnp.int32(LANES)

    def first_lane(mask):
        return jnp.min(jnp.where(mask, lane, big), axis=1, keepdims=True)

    gl = jnp.where(is_g, logits, NEG_INF)
    ge = jnp.exp(gl - jnp.max(gl, axis=1, keepdims=True))
    p_groups = ge / jnp.sum(ge, axis=1, keepdims=True)
    p_g = jnp.max(p_groups, axis=1, keepdims=True)
    g_idx = first_lane(is_g & (p_groups == p_g)) - N_EXPERTS

    in_grp = (lane < N_EXPERTS) & ((lane // EXP_PER_GROUP) == g_idx)
    el = jnp.where(in_grp, logits, NEG_INF)
    ee = jnp.exp(el - jnp.max(el, axis=1, keepdims=True))
    es = ee / jnp.sum(ee, axis=1, keepdims=True)
    es = jnp.where(in_grp, es, NEG_INF)
    v1 = jnp.max(es, axis=1, keepdims=True)
    i1 = first_lane(in_grp & (es == v1))
    es2 = jnp.where(lane == i1, NEG_INF, es)
    v2 = jnp.max(es2, axis=1, keepdims=True)
    i2 = first_lane(in_grp & (es2 == v2))
    tot = v1 + v2
    e_w = jnp.where(lane == i1, v1 / tot, 0.0) + jnp.where(lane == i2, v2 / tot, 0.0)
    return p_g * e_w + jnp.where(lane == ROUTE_GROUP_LANE + g_idx, 1.0, 0.0)


def _out_proj_kernel(ya_ref, yr_ref, x_ref, mod_ref, wo_ref, gpm_ref, gpf_ref, wr_ref, br_ref,
                     x1_ref, h2_ref, comb_ref):
    mix = (jnp.dot(ya_ref[0], wo_ref[0:D_ATTN, :], preferred_element_type=F32)
           + jnp.dot(yr_ref[0], wo_ref[D_ATTN:D_ATTN + D_RNN, :], preferred_element_type=F32))
    gate1 = mod_ref[0, 2:3, :]
    shift2 = mod_ref[0, 3:4, :]
    scale2 = mod_ref[0, 4:5, :]
    x1 = x_ref[0] + gate1 * _rms(mix, gpm_ref[...])
    x1_ref[0] = x1
    h2 = _rms(x1, gpf_ref[...]) * (1.0 + scale2) + shift2
    h_hi = h2.astype(BF16)
    h2_ref[0] = h_hi
    h_lo = (h2 - h_hi.astype(F32)).astype(BF16)
    part = jnp.dot(h_hi, wr_ref[...], preferred_element_type=F32)
    logits = (part[:, 0:LANES] + (part[:, LANES:2 * LANES]
              + jnp.dot(h_lo, wr_ref[:, 0:LANES], preferred_element_type=F32))) + br_ref[...]
    comb_ref[0] = _route(logits)


def _out_proj_call(ya, yr, x, mod, wo, gpm, gpf, wr, br):
    b, s, d = x.shape
    tm = min(OUT_TILE, s)
    tile = lambda n: pl.BlockSpec((1, tm, n), lambda bi, ti: (bi, ti, 0))
    const = lambda shape: pl.BlockSpec(shape, lambda bi, ti: (0,) * len(shape))
    return pl.pallas_call(
        _out_proj_kernel,
        grid=(b, s // tm),
        in_specs=[tile(D_ATTN), tile(D_RNN), tile(d),
                  pl.BlockSpec((1, N_MOD, d), lambda bi, ti: (bi, 0, 0)),
                  const((D_ATTN + D_RNN, d)), const((1, d)), const((1, d)),
                  const((d, 2 * LANES)), const((1, LANES))],
        out_specs=[tile(d), tile(d), tile(LANES)],
        out_shape=[jax.ShapeDtypeStruct((b, s, d), F32), jax.ShapeDtypeStruct((b, s, d), BF16),
                   jax.ShapeDtypeStruct((b, s, LANES), F32)],
        compiler_params=_cparams(("parallel", "parallel")),
        name="out_proj",
    )(ya, yr, x, mod, wo, gpm, gpf, wr, br)


def _moe_kernel(h2_ref, comb_ref, x1_ref, mod_ref, wg_ref, wu_ref, wd_ref, gpo_ref,
                o_ref, hs_ref, cs_ref, ys_ref, act_ref, slot_ref, seg_ref, *, tm, ns, rb):
    g = pl.program_id(2)
    lane = lax.broadcasted_iota(I32, (1, LANES), 1)
    n_blk = tm // LANES

    @pl.when(g == 0)
    def _dispatch():
        comb = comb_ref[0]
        oh = jnp.where((lane >= ROUTE_GROUP_LANE) & (lane < ROUTE_GROUP_LANE + N_GROUPS), comb, 0.0)
        oh16 = oh.astype(BF16)
        sub = lax.broadcasted_iota(I32, (LANES, 1), 0)
        base = jnp.int32(0)
        base_lane = jnp.zeros((1, LANES), F32)
        base_sub = jnp.zeros((LANES, 1), F32)
        for gg in range(N_GROUPS):
            here = ROUTE_GROUP_LANE + gg
            n = jnp.sum(jnp.where(lane == here, oh, 0.0)).astype(I32)
            nb = (n + rb - 1) // rb
            seg_ref[gg] = base
            seg_ref[N_GROUPS + gg] = nb
            base_f = base.astype(F32)
            base_lane = base_lane + jnp.where(lane == here, base_f, 0.0)
            base_sub = base_sub + jnp.where(sub == here, base_f, 0.0)
            base = base + nb * rb

        tok_c = lax.broadcasted_iota(I32, (1, tm), 1)
        tok_r = lax.broadcasted_iota(I32, (tm, 1), 0)
        blk_r = lax.broadcasted_iota(I32, (LANES, 1), 0)
        blk_c = lax.broadcasted_iota(I32, (1, LANES), 1)
        eye = (blk_r == blk_c).astype(BF16)
        oht16 = _nt_dot(eye, oh16).astype(BF16)
        slot_row = []
        for t in range(n_blk):
            tril = (tok_c <= blk_r + t * LANES).astype(BF16)
            rank = jnp.dot(tril, oh16, preferred_element_type=F32)
            oh_blk = oh[t * LANES:(t + 1) * LANES, :]
            slot_ref[t * LANES:(t + 1) * LANES, :] = jnp.sum(
                oh_blk * (base_lane + rank - 1.0), axis=1, keepdims=True).astype(I32)
            triu = (tok_r <= blk_c + t * LANES).astype(BF16)
            rank_t = jnp.dot(oht16, triu, preferred_element_type=F32)
            oht_blk = oht16[:, t * LANES:(t + 1) * LANES].astype(F32)
            slot_row.append(jnp.sum(oht_blk * (base_sub + rank_t - 1.0), axis=0,
                                    keepdims=True).astype(I32))
        slot_row = jnp.concatenate(slot_row, axis=1)

        hi = comb.astype(BF16)
        lo = (comb - hi.astype(F32)).astype(BF16)
        cw = jnp.concatenate([hi, lo], axis=1)
        h = h2_ref[0]
        for r in range(ns // LANES):
            rows = slice(r * LANES, (r + 1) * LANES)
            p = (blk_r + r * LANES == slot_row).astype(BF16)
            hs_ref[rows, :] = jnp.dot(p, h, preferred_element_type=F32).astype(BF16)
            cw_sorted = jnp.dot(p, cw, preferred_element_type=F32)
            cs_ref[rows, :] = cw_sorted[:, 0:LANES] + cw_sorted[:, LANES:2 * LANES]
        ys_ref[...] = jnp.zeros_like(ys_ref)

    base = seg_ref[g]

    def block(r, carry):
        rows = pl.ds(pl.multiple_of(base + r * rb, rb), rb)
        hb = hs_ref[rows, :]
        cb = cs_ref[rows, :]
        for e in range(EXP_PER_GROUP):
            gt = jnp.dot(hb, wg_ref[0, e], preferred_element_type=F32)
            up = jnp.dot(hb, wu_ref[0, e], preferred_element_type=F32)
            cols = slice(e * D_EXPERT, (e + 1) * D_EXPERT)
            scale = jnp.sum(jnp.where(lane == g * EXP_PER_GROUP + e, cb, 0.0), axis=1, keepdims=True)
            act_ref[:, cols] = (gt * jax.nn.sigmoid(gt) * up * scale).astype(BF16)
        ys_ref[rows, :] = jnp.dot(act_ref[...], wd_ref[0], preferred_element_type=F32).astype(BF16)
        return carry

    lax.fori_loop(0, seg_ref[N_GROUPS + g], block, 0)

    @pl.when(g == N_GROUPS - 1)
    def _combine():
        gate2 = mod_ref[0, 5:6, :]
        slot_c = lax.broadcasted_iota(I32, (1, ns), 1)
        half = tm // 2
        for hh in range(2):
            rows = slice(hh * half, (hh + 1) * half)
            pt = (slot_c == slot_ref[rows, :]).astype(BF16)
            y = jnp.dot(pt, ys_ref[...], preferred_element_type=F32)
            o_ref[0, rows, :] = x1_ref[0, rows, :] + gate2 * _rms(y, gpo_ref[...])


def _moe_call(h2, comb, x1, mod, wg, wu, wd, gpo):
    b, s, d = x1.shape
    tm = min(MOE_TILE, s)
    rb = MOE_ROWS
    ns = tm + N_GROUPS * rb
    tile = lambda n, **kw: pl.BlockSpec((1, tm, n), lambda bi, ti, gi: (bi, ti, 0), **kw)
    kern = functools.partial(_moe_kernel, tm=tm, ns=ns, rb=rb)
    return pl.pallas_call(
        kern,
        grid=(b, s // tm, N_GROUPS),
        in_specs=[tile(d), tile(LANES), tile(d, pipeline_mode=pl.Buffered(1)),
                  pl.BlockSpec((1, N_MOD, d), lambda bi, ti, gi: (bi, 0, 0)),
                  pl.BlockSpec((1, EXP_PER_GROUP, d, D_EXPERT), lambda bi, ti, gi: (gi, 0, 0, 0)),
                  pl.BlockSpec((1, EXP_PER_GROUP, d, D_EXPERT), lambda bi, ti, gi: (gi, 0, 0, 0)),
                  pl.BlockSpec((1, EXP_PER_GROUP * D_EXPERT, d), lambda bi, ti, gi: (gi, 0, 0)),
                  pl.BlockSpec((1, d), lambda bi, ti, gi: (0, 0))],
        out_specs=tile(d),
        out_shape=jax.ShapeDtypeStruct((b, s, d), F32),
        scratch_shapes=[pltpu.VMEM((ns, d), BF16),
                        pltpu.VMEM((ns, LANES), F32),
                        pltpu.VMEM((ns, d), BF16),
                        pltpu.VMEM((rb, EXP_PER_GROUP * D_EXPERT), BF16),
                        pltpu.VMEM((tm, 1), I32),
                        pltpu.SMEM((2 * N_GROUPS,), I32)],
        compiler_params=_cparams(("parallel", "parallel", "arbitrary")),
        name="moe",
    )(h2, comb, x1, mod, wg, wu, wd, gpo)


def _moe_dense_kernel(h2_ref, comb_ref, x1_ref, mod_ref, wg_ref, wu_ref, wd_ref, gpo_ref,
                      o_ref, acc_ref, act_ref):
    g = pl.program_id(2)

    @pl.when(g == 0)
    def _():
        acc_ref[...] = jnp.zeros_like(acc_ref)

    h = h2_ref[0]
    comb = comb_ref[0]
    hi = comb.astype(BF16)
    lo = (comb - hi.astype(F32)).astype(BF16)
    n_act = EXP_PER_GROUP * D_EXPERT
    r_i = lax.broadcasted_iota(I32, (2 * LANES, n_act), 0) % LANES
    c_i = lax.broadcasted_iota(I32, (2 * LANES, n_act), 1) // D_EXPERT
    expand = (r_i == g * EXP_PER_GROUP + c_i).astype(BF16)
    cexp = jnp.dot(jnp.concatenate([hi, lo], axis=1), expand, preferred_element_type=F32)
    for e in range(EXP_PER_GROUP):
        gt = jnp.dot(h, wg_ref[0, e], preferred_element_type=F32)
        up = jnp.dot(h, wu_ref[0, e], preferred_element_type=F32)
        cols = slice(e * D_EXPERT, (e + 1) * D_EXPERT)
        act_ref[:, cols] = (gt * jax.nn.sigmoid(gt) * up * cexp[:, cols]).astype(BF16)
    acc_ref[...] += jnp.dot(act_ref[...], wd_ref[0], preferred_element_type=F32)

    @pl.when(g == N_GROUPS - 1)
    def _():
        gate2 = mod_ref[0, 5:6, :]
        o_ref[0] = x1_ref[0] + gate2 * _rms(acc_ref[...], gpo_ref[...])


def _moe_dense_call(h2, comb, x1, mod, wg, wu, wd, gpo):
    b, s, d = x1.shape
    tm = min(RNN_TILE, s)
    tile = lambda n: pl.BlockSpec((1, tm, n), lambda bi, ti, gi: (bi, ti, 0))
    return pl.pallas_call(
        _moe_dense_kernel,
        grid=(b, s // tm, N_GROUPS),
        in_specs=[tile(d), tile(LANES), tile(d),
                  pl.BlockSpec((1, N_MOD, d), lambda bi, ti, gi: (bi, 0, 0)),
                  pl.BlockSpec((1, EXP_PER_GROUP, d, D_EXPERT), lambda bi, ti, gi: (gi, 0, 0, 0)),
                  pl.BlockSpec((1, EXP_PER_GROUP, d, D_EXPERT), lambda bi, ti, gi: (gi, 0, 0, 0)),
                  pl.BlockSpec((1, EXP_PER_GROUP * D_EXPERT, d), lambda bi, ti, gi: (gi, 0, 0)),
                  pl.BlockSpec((1, d), lambda bi, ti, gi: (0, 0))],
        out_specs=tile(d),
        out_shape=jax.ShapeDtypeStruct((b, s, d), F32),
        scratch_shapes=[pltpu.VMEM((tm, d), F32),
                        pltpu.VMEM((tm, EXP_PER_GROUP * D_EXPERT), BF16)],
        compiler_params=_cparams(("parallel", "parallel", "arbitrary")),
        name="moe",
    )(h2, comb, x1, mod, wg, wu, wd, gpo)


def _rope_tables(seq_len):
    pos = jnp.arange(seq_len, dtype=F32)
    inv = ROPE_THETA ** (-jnp.arange(0, HEAD_DIM, 2, dtype=F32) / HEAD_DIM)
    ang = pos[:, None] * inv[None, :]
    cos, sin = jnp.cos(ang), jnp.sin(ang)
    cos2 = jnp.concatenate([cos, cos, cos, cos], axis=1)
    sin2 = jnp.concatenate([-sin, sin, -sin, sin], axis=1)
    return cos2, sin2, cos2[:, :HEAD_DIM].T, sin2[:, :HEAD_DIM].T


def _block_diag(w):
    n, c, d = w.shape
    eye = jnp.eye(n, dtype=w.dtype)
    return (eye[:, None, :, None] * w[:, :, None, :]).reshape(n * c, n * d)


def _pad_cols(w, n):
    return jnp.pad(w, ((0, 0), (0, n - w.shape[1])))


def kernel(x, c, w_ada, b_ada, g_pre_mix, g_post_mix, g_pre_ffn, g_post_ffn, w_in, conv_w, conv_b, w_rg_a, b_rg_a, w_rg_x, b_rg_x, lru_lambda, w_out, w_router_group, b_router_group, w_router_expert, b_router_expert, w_gate, w_up, w_down):
    b, s, d = x.shape
    depth = w_ada.shape[0]
    cos2, sin2, cos_t, sin_t = _rope_tables(s)
    c_pad = jnp.pad(c, ((0, (-b) % SUBLANES), (0, 0)))
    o_q, o_k, o_v, o_qi, o_ki, o_wi, o_xr, o_xg = 0, 512, 1024, 1536, 2048, 2112, 2120, 2632
    for l in range(depth):
        wl = w_in[l]
        w_ki = wl[:, o_ki:o_wi]
        w_nat = jnp.concatenate([wl[:, o_k:o_v], w_ki, w_ki, wl[:, o_xr:o_xg], wl[:, o_xg:]],
                                axis=1).astype(BF16)
        w_tr = jnp.concatenate([wl[:, o_q:o_k], wl[:, o_qi:o_ki], wl[:, o_v:o_qi],
                                _pad_cols(wl[:, o_wi:o_xr], _R_END - _R_WI)], axis=1).T.astype(BF16)
        w_route = _pad_cols(jnp.concatenate([w_router_expert[l], w_router_group[l]], axis=1), LANES)
        w_route_hi = w_route.astype(BF16)
        w_route = jnp.concatenate(
            [w_route_hi, (w_route - w_route_hi.astype(F32)).astype(BF16)], axis=1)
        b_route = _pad_cols(jnp.concatenate([b_router_expert[l], b_router_group[l]])[None, :], LANES)

        mod = _mod_call(c_pad, w_ada[l], b_ada[l][None, :])[:b].reshape(b, N_MOD, d)
        qt, qit, wit, k, kk, vt, xr, xg = _in_proj_call(
            x, mod, g_pre_mix[l][None, :], w_nat, w_tr, cos2, sin2, cos_t, sin_t)
        y_attn = _attn_call(qt, qit, wit, kk, k, vt)
        y_rnn = _rglru_call(xr, xg, conv_w[l], conv_b[l][None, :],
                            _block_diag(w_rg_a[l]).astype(BF16), b_rg_a[l][None, :],
                            _block_diag(w_rg_x[l]).astype(BF16), b_rg_x[l][None, :],
                            lru_lambda[l][None, :])
        x1, h2, comb = _out_proj_call(y_attn, y_rnn, x, mod, w_out[l].astype(BF16),
                                      g_post_mix[l][None, :], g_pre_ffn[l][None, :], w_route, b_route)
        x = _moe_dense_call(h2, comb, x1, mod, w_gate[l].astype(BF16), w_up[l].astype(BF16),
                      w_down[l].reshape(N_GROUPS, EXP_PER_GROUP * D_EXPERT, d).astype(BF16),
                      g_post_ffn[l][None, :])
    return x
```
